```python
import math
import jax
import jax.numpy as jnp
from jax import lax
import numpy as np

D_MODEL = 1024
BATCH = 8
SEQ = 2048
DEPTH = 4

D_HEAD = 64
N_MIXERS = 4
MIX_WIDTH = 256
NSA_HEADS = 4
NSA_ROT = D_HEAD // 4
CMP_BLOCK = 32
CMP_STRIDE = 16
SEL_BLOCK = 64
SEL_TOPN = 16
WINDOW = 512
FOX_HEADS = 4
MLA_HEADS = 4
MLA_Q_RANK = 256
MLA_KV_RANK = 128
MLA_NOPE = 64
MLA_ROPE = 32
MLA_V = 64
DIFF_HEADS = 4
DIFF_QK = 32
DIFF_V = 64
DIFF_ROT = DIFF_QK // 4
D_FF = 2816
CONV_WIDTH = 3
Q_BLOCK = 128
ROPE_THETA = 500000.0
EPS = 1e-6
FORCE_SCORE = 1e4
NEG_BIG = -1e30

IN_SPLITS = (
    NSA_HEADS * D_HEAD, D_HEAD, D_HEAD, D_HEAD, D_HEAD, D_HEAD, D_HEAD, 3 * NSA_HEADS,
    FOX_HEADS * D_HEAD, FOX_HEADS * D_HEAD, FOX_HEADS * D_HEAD, FOX_HEADS,
    MLA_Q_RANK, MLA_KV_RANK, MLA_ROPE,
    DIFF_HEADS * 2 * DIFF_QK, DIFF_HEADS * 2 * DIFF_QK, DIFF_HEADS * DIFF_V,
)
IN_WIDTH = sum(IN_SPLITS)

kernel_name = 'hybrid_nsa_fox_mla_diff_trunk'


def rms_norm(x, g=None, eps=EPS):
    xf = x.astype(jnp.float32)
    y = xf * lax.rsqrt(jnp.mean(xf * xf, axis=-1, keepdims=True) + eps)
    if g is not None:
        y = y * g.astype(jnp.float32)
    return y.astype(x.dtype)


def rope(x, pos, n_rot):
    half = n_rot // 2
    inv_freq = ROPE_THETA ** (-jnp.arange(half, dtype=jnp.float32) / half)
    ang = pos.astype(jnp.float32)[:, None, :, None] * inv_freq
    cos = jnp.cos(ang).astype(x.dtype)
    sin = jnp.sin(ang).astype(x.dtype)
    x1 = x[..., :half]
    x2 = x[..., half:n_rot]
    return jnp.concatenate([x1 * cos - x2 * sin, x2 * cos + x1 * sin, x[..., n_rot:]], axis=-1)


def heads(t, n):
    b, s, _ = t.shape
    return t.reshape(b, s, n, -1).transpose(0, 2, 1, 3)


def merge_heads(t):
    b, h, s, d = t.shape
    return t.transpose(0, 2, 1, 3).reshape(b, s, h * d)


def split_cols(t, sizes):
    idx = []
    acc = 0
    for s in sizes[:-1]:
        acc += s
        idx.append(acc)
    return jnp.split(t, idx, axis=-1)


def chunk_seq(t, axis):
    s = t.shape
    t = t.reshape(s[:axis] + (s[axis] // Q_BLOCK, Q_BLOCK) + s[axis + 1:])
    return jnp.moveaxis(t, axis, 0)


def unchunk_seq(t, axis):
    t = jnp.moveaxis(t, 0, axis)
    s = t.shape
    return t.reshape(s[:axis] + (s[axis] * s[axis + 1],) + s[axis + 2:])


def causal_attention(q, k, v, q_cum=None, k_cum=None):
    S = q.shape[2]
    scale = q.shape[-1] ** -0.5
    kpos = jnp.arange(S)

    def body(args):
        i, qb = args[0], args[1]
        s = jnp.einsum('bhqd,bhkd->bhqk', qb, k).astype(jnp.float32) * scale
        if q_cum is not None:
            s = s + args[2][..., None] - k_cum[:, :, None, :]
        qpos = i * Q_BLOCK + jnp.arange(Q_BLOCK)
        s = jnp.where(kpos[None, :] <= qpos[:, None], s, -jnp.inf)
        p = jax.nn.softmax(s, axis=-1).astype(v.dtype)
        return jnp.einsum('bhqk,bhkd->bhqd', p, v)

    xs = (jnp.arange(S // Q_BLOCK), chunk_seq(q, 2))
    if q_cum is not None:
        xs = xs + (chunk_seq(q_cum, 2),)
    return unchunk_seq(lax.map(body, xs), 2)


def compress_blocks(k, pe, w1, w2):
    b, s, d = k.shape
    halves = k.reshape(b, s // CMP_STRIDE, CMP_STRIDE, d)
    blocks = jnp.concatenate([halves[:, :-1], halves[:, 1:]], axis=2) + pe
    hid = jax.nn.silu(blocks.reshape(b, blocks.shape[1], CMP_BLOCK * d) @ w1)
    return hid @ w2


def selected_block_attention(q, k_blk, v_blk, sel_idx):
    b, h, S, d = q.shape
    scale = d ** -0.5
    gather = jax.vmap(lambda blocks, ix: blocks[ix])

    def body(args):
        i, qb, ib = args
        ksel = gather(k_blk, ib)
        vsel = gather(v_blk, ib)
        s = jnp.einsum('bhqd,bqnkd->bhqnk', qb, ksel).astype(jnp.float32) * scale
        qpos = i * Q_BLOCK + jnp.arange(Q_BLOCK)
        kpos = ib[..., None] * SEL_BLOCK + jnp.arange(SEL_BLOCK)
        mask = (kpos <= qpos[None, :, None, None])[:, None]
        s = jnp.where(mask, s, -jnp.inf)
        shp = s.shape
        p = jax.nn.softmax(s.reshape(shp[0], shp[1], shp[2], -1), axis=-1).reshape(shp).astype(vsel.dtype)
        return jnp.einsum('bhqnk,bqnkd->bhqd', p, vsel)

    out = lax.map(body, (jnp.arange(S // Q_BLOCK), chunk_seq(q, 2), chunk_seq(sel_idx, 1)))
    return unchunk_seq(out, 2)


def window_attention(q, k, v):
    b, h, S, d = q.shape
    scale = d ** -0.5
    kp = jnp.pad(k, ((0, 0), (WINDOW, 0), (0, 0)))
    vp = jnp.pad(v, ((0, 0), (WINDOW, 0), (0, 0)))
    span = Q_BLOCK + WINDOW

    def body(args):
        i, qb = args
        start = i * Q_BLOCK
        kb = lax.dynamic_slice_in_dim(kp, start, span, axis=1)
        vb = lax.dynamic_slice_in_dim(vp, start, span, axis=1)
        s = jnp.einsum('bhqd,bkd->bhqk', qb, kb).astype(jnp.float32) * scale
        qpos = start + jnp.arange(Q_BLOCK)
        kpos = start - WINDOW + jnp.arange(span)
        dist = qpos[:, None] - kpos[None, :]
        mask = (dist >= 0) & (dist < WINDOW) & (kpos[None, :] >= 0)
        p = jax.nn.softmax(jnp.where(mask, s, -jnp.inf), axis=-1).astype(vb.dtype)
        return jnp.einsum('bhqk,bkd->bhqd', p, vb)

    out = lax.map(body, (jnp.arange(S // Q_BLOCK), chunk_seq(q, 2)))
    return unchunk_seq(out, 2)


def nsa_mixer(parts, pos, qk_g, cmp_pe, cmp_w1, cmp_w2):
    q, k_c, v_c, k_s, v_s, k_w, v_w, g = parts
    b, S, _ = q.shape
    t = jnp.arange(S)
    q = rope(rms_norm(heads(q, NSA_HEADS), qk_g[0]), pos, NSA_ROT)
    k_cmp = rms_norm(compress_blocks(k_c, cmp_pe[0], cmp_w1[0], cmp_w2[0]), qk_g[1])
    v_cmp = compress_blocks(v_c, cmp_pe[1], cmp_w1[1], cmp_w2[1])
    n_cmp = k_cmp.shape[1]
    cmp_end = jnp.arange(n_cmp) * CMP_STRIDE + (CMP_BLOCK - 1)
    vis = cmp_end[None, :] <= t[:, None]
    s = jnp.einsum('bhsd,bcd->bhsc', q, k_cmp).astype(jnp.float32) * (D_HEAD ** -0.5)
    p_cmp = jax.nn.softmax(jnp.where(vis, s, NEG_BIG), axis=-1)
    p_cmp = jnp.where(jnp.any(vis, axis=-1)[:, None], p_cmp, 0.0)
    o_cmp = jnp.einsum('bhsc,bcd->bhsd', p_cmp.astype(v_cmp.dtype), v_cmp)
    n_blk = S // SEL_BLOCK
    n_sel = min(SEL_TOPN, n_blk)
    cs = np.arange(n_cmp)[:, None] * CMP_STRIDE
    bs = np.arange(n_blk)[None, :] * SEL_BLOCK
    overlap = jnp.asarray(((cs < bs + SEL_BLOCK) & (cs + CMP_BLOCK > bs)).astype(np.float32))
    importance = jnp.einsum('bhsc,cj->bsj', p_cmp, overlap)
    blk = jnp.arange(n_blk)[None, :]
    cur = (t // SEL_BLOCK)[:, None]
    forced = (blk == 0) | (blk == cur) | (blk == cur - 1)
    score = jnp.where(blk * SEL_BLOCK > t[:, None], -1.0, jnp.where(forced, FORCE_SCORE, importance))
    _, sel_idx = lax.top_k(score, n_sel)
    k_s = rope(rms_norm(k_s[:, None], qk_g[2]), pos, NSA_ROT)[:, 0]
    o_slc = selected_block_attention(q, k_s.reshape(b, n_blk, SEL_BLOCK, D_HEAD),
                                     v_s.reshape(b, n_blk, SEL_BLOCK, D_HEAD), sel_idx)
    k_w = rope(rms_norm(k_w[:, None], qk_g[3]), pos, NSA_ROT)[:, 0]
    o_win = window_attention(q, k_w, v_w)
    gt = jax.nn.sigmoid(g.astype(jnp.float32)).astype(q.dtype)
    gt = gt.reshape(b, S, 3, NSA_HEADS).transpose(2, 0, 3, 1)[..., None]
    return merge_heads(gt[0] * o_cmp + gt[1] * o_slc + gt[2] * o_win)


def fox_mixer(parts, qk_g, f_b):
    q, k, v, f = parts
    q = rms_norm(heads(q, FOX_HEADS), qk_g[0])
    k = rms_norm(heads(k, FOX_HEADS), qk_g[1])
    v = heads(v, FOX_HEADS)
    log_f = jax.nn.log_sigmoid(f.astype(jnp.float32) + f_b.astype(jnp.float32))
    cum = lax.cumsum(log_f, axis=1).transpose(0, 2, 1)
    return merge_heads(causal_attention(q, k, v, cum, cum))


def mla_mixer(parts, pos, cq_g, ckv_g, w_uq, w_ukv, qk_g):
    c_q, c_kv, k_rope = parts
    b, S, _ = c_q.shape
    q = heads(rms_norm(c_q, cq_g) @ w_uq, MLA_HEADS)
    kv = heads(rms_norm(c_kv, ckv_g) @ w_ukv, MLA_HEADS)
    k_nope, v = kv[..., :MLA_NOPE], kv[..., MLA_NOPE:]
    k = jnp.concatenate([jnp.broadcast_to(k_rope[:, None], (b, MLA_HEADS, S, MLA_ROPE)), k_nope], axis=-1)
    q = rope(rms_norm(q, qk_g[0]), pos, MLA_ROPE)
    k = rope(rms_norm(k, qk_g[1]), pos, MLA_ROPE)
    return merge_heads(causal_attention(q, k, v))


def diff_mixer(parts, pos, lam_init, qk_g, lam_vec, out_g):
    q, k, v = parts
    b, S, _ = q.shape

    def halves(t):
        return t.reshape(b, S, DIFF_HEADS, 2, DIFF_QK).transpose(3, 0, 2, 1, 4)

    q = rope(rms_norm(halves(q), qk_g[0]), pos, DIFF_ROT)
    k = rope(rms_norm(halves(k), qk_g[1]), pos, DIFF_ROT)
    v = heads(v, DIFF_HEADS)
    lv = lam_vec.astype(jnp.float32)
    lam = jnp.exp(jnp.sum(lv[0] * lv[1])) - jnp.exp(jnp.sum(lv[2] * lv[3])) + lam_init
    o = causal_attention(q[0], k[0], v) - lam.astype(v.dtype) * causal_attention(q[1], k[1], v)
    return merge_heads(rms_norm(o, out_g) * (1.0 - lam_init))


def token_mixer(h, pos, lam_init, w_in, nsa_qk_g, nsa_cmp_pe, nsa_cmp_w1, nsa_cmp_w2, fox_qk_g, fox_f_b,
                mla_cq_g, mla_ckv_g, mla_w_uq, mla_w_ukv, mla_qk_g, diff_qk_g, diff_lambda, diff_out_g,
                br_w, gate_w, gate_b, w_out):
    b, S, D = h.shape
    parts = split_cols(h @ w_in, IN_SPLITS)
    o_nsa = nsa_mixer(parts[0:8], pos, nsa_qk_g, nsa_cmp_pe, nsa_cmp_w1, nsa_cmp_w2)
    o_fox = fox_mixer(parts[8:12], fox_qk_g, fox_f_b)
    o_mla = mla_mixer(parts[12:15], pos, mla_cq_g, mla_ckv_g, mla_w_uq, mla_w_ukv, mla_qk_g)
    o_diff = diff_mixer(parts[15:18], pos, lam_init, diff_qk_g, diff_lambda, diff_out_g)
    branches = jnp.stack([o_nsa, o_fox, o_mla, o_diff], axis=2)
    y = jnp.einsum('bsmc,mcd->bsmd', branches, br_w)
    gates = jax.nn.sigmoid((h @ gate_w + gate_b).astype(jnp.float32)).astype(h.dtype)
    gates = gates.reshape(b, S, N_MIXERS, D)
    return jnp.sum(gates * y, axis=2) @ w_out


def conv_ffn(h, w_up, conv_w, conv_b, w_down):
    g, v = jnp.split(h @ w_up, 2, axis=-1)
    g = lax.conv_general_dilated(g, conv_w[:, None, :].astype(g.dtype), window_strides=(1,),
                                 padding=[(CONV_WIDTH - 1, 0)], dimension_numbers=('NWC', 'WIO', 'NWC'),
                                 feature_group_count=g.shape[-1]) + conv_b
    return (jax.nn.silu(g) * v) @ w_down


def setup_inputs(seed: int = 0) -> dict:
    key = jax.random.key(seed)
    ks = iter(jax.random.split(key, 40))

    def nrm(shape, scale):
        return jax.random.normal(next(ks), shape, jnp.float32) * scale

    def gain(shape):
        return 1.0 + nrm(shape, 0.02)

    L, D = DEPTH, D_MODEL
    x = nrm((BATCH, SEQ, D), 1.0)
    c = nrm((BATCH, D), 1.0)
    positions = (jax.random.randint(next(ks), (BATCH, 1), 0, 4096, dtype=jnp.int32)
                 + jnp.arange(SEQ, dtype=jnp.int32)[None, :])
    return {
        'x': x,
        'c': c,
        'positions': positions,
        'ada_w': nrm((L, D, 6 * D), 0.5 * D ** -0.5),
        'ada_b': nrm((L, 6 * D), 0.02),
        'w_in': nrm((L, D, IN_WIDTH), D ** -0.5),
        'nsa_qk_g': gain((L, 4, D_HEAD)),
        'nsa_cmp_pe': nrm((L, 2, CMP_BLOCK, D_HEAD), 0.1),
        'nsa_cmp_w1': nrm((L, 2, CMP_BLOCK * D_HEAD, D_HEAD), (CMP_BLOCK * D_HEAD) ** -0.5),
        'nsa_cmp_w2': nrm((L, 2, D_HEAD, D_HEAD), D_HEAD ** -0.5),
        'fox_qk_g': gain((L, 2, D_HEAD)),
        'fox_f_b': 3.0 + nrm((L, FOX_HEADS), 0.5),
        'mla_cq_g': gain((L, MLA_Q_RANK)),
        'mla_ckv_g': gain((L, MLA_KV_RANK)),
        'mla_w_uq': nrm((L, MLA_Q_RANK, MLA_HEADS * (MLA_ROPE + MLA_NOPE)), MLA_Q_RANK ** -0.5),
        'mla_w_ukv': nrm((L, MLA_KV_RANK, MLA_HEADS * (MLA_NOPE + MLA_V)), MLA_KV_RANK ** -0.5),
        'mla_qk_g': gain((L, 2, MLA_ROPE + MLA_NOPE)),
        'diff_qk_g': gain((L, 2, DIFF_QK)),
        'diff_lambda': nrm((L, 4, DIFF_QK), 0.1),
        'diff_out_g': gain((L, DIFF_V)),
        'br_w': nrm((L, N_MIXERS, MIX_WIDTH, D), MIX_WIDTH ** -0.5),
        'gate_w': nrm((L, D, N_MIXERS * D), D ** -0.5),
        'gate_b': nrm((L, N_MIXERS * D), 0.02),
        'w_out': nrm((L, D, D), D ** -0.5),
        'ffn_w_up': nrm((L, D, 2 * D_FF), D ** -0.5),
        'ffn_conv_w': nrm((L, CONV_WIDTH, D_FF), CONV_WIDTH ** -0.5),
        'ffn_conv_b': nrm((L, D_FF), 0.02),
        'ffn_w_down': nrm((L, D_FF, D), D_FF ** -0.5),
    }


def reference(x, c, positions, ada_w, ada_b, w_in, nsa_qk_g, nsa_cmp_pe, nsa_cmp_w1, nsa_cmp_w2,
              fox_qk_g, fox_f_b, mla_cq_g, mla_ckv_g, mla_w_uq, mla_w_ukv, mla_qk_g,
              diff_qk_g, diff_lambda, diff_out_g, br_w, gate_w, gate_b, w_out,
              ffn_w_up, ffn_conv_w, ffn_conv_b, ffn_w_down):
    for l in range(DEPTH):
        mod = jax.nn.silu(c) @ ada_w[l] + ada_b[l]
        sh1, sc1, g1, sh2, sc2, g2 = jnp.split(mod[:, None, :], 6, axis=-1)
        lam_init = 0.8 - 0.6 * math.exp(-0.3 * l)
        h = rms_norm(x) * (1.0 + sc1) + sh1
        x = x + g1 * token_mixer(h, positions, lam_init, w_in[l], nsa_qk_g[l], nsa_cmp_pe[l], nsa_cmp_w1[l],
                                 nsa_cmp_w2[l], fox_qk_g[l], fox_f_b[l], mla_cq_g[l], mla_ckv_g[l],
                                 mla_w_uq[l], mla_w_ukv[l], mla_qk_g[l], diff_qk_g[l], diff_lambda[l],
                                 diff_out_g[l], br_w[l], gate_w[l], gate_b[l], w_out[l])
        h = rms_norm(x) * (1.0 + sc2) + sh2
        x = x + g2 * conv_ffn(h, ffn_w_up[l], ffn_conv_w[l], ffn_conv_b[l], ffn_w_down[l])
    return x
```

```python
import functools
import math

import jax
import jax.numpy as jnp
from jax import lax
from jax.experimental import pallas as pl
from jax.experimental.pallas import tpu as pltpu

F32 = jnp.float32
BF16 = jnp.bfloat16

D_MODEL = 1024
D_HEAD = 64
N_HEADS = 4
MIX_WIDTH = 256
NSA_ROT = 16
CMP_BLOCK = 32
CMP_STRIDE = 16
SEL_BLOCK = 64
SEL_TOPN = 16
WINDOW = 512
MLA_Q_RANK = 256
MLA_KV_RANK = 128
MLA_NOPE = 64
MLA_ROPE = 32
MLA_QK = MLA_ROPE + MLA_NOPE
DIFF_QK = 32
DIFF_ROT = 8
D_FF = 2816
ROPE_THETA = 500000.0
EPS = 1e-6
FORCE_SCORE = 1e4
NEG_BIG = -1e30

LANES = 128
VMEM_LIMIT = 56 * 1024 * 1024

C_NQ, C_NKSW, C_NVSW, C_NKVC, C_NG = 0, 256, 384, 512, 640
C_FQ, C_FK, C_FV, C_FF = 768, 1024, 1280, 1536
C_MCQ, C_MCKV, C_MKR = 1664, 1920, 2048
C_DQ, C_DK, C_DV = 2176, 2432, 2688
W_IN_PAD = 2944

V_NQ, V_NK, V_FQ, V_FK, V_MQ, V_MK, V_DQ, V_DK, V_FB, V_CQG, V_CKVG = range(11)
N_VEC = 16

TS_PROJ = 256
TS_DENSE = 512
TQ_ATT = 256
TQ_NSA = 128
TK_SLC = 256
TK_WIN = 128
FF_CHUNK = 1408


def _dot(a, b):
    return jnp.dot(a, b, preferred_element_type=F32)


def _dot_nt(a, b):
    return lax.dot_general(a, b, (((1,), (1,)), ((), ())), preferred_element_type=F32)


def _split2(x):
    hi = x.astype(BF16)
    lo = (x - hi.astype(F32)).astype(BF16)
    return hi, lo


def _split3(x):
    hi = x.astype(BF16)
    r = x - hi.astype(F32)
    mid = r.astype(BF16)
    lo = (r - mid.astype(F32)).astype(BF16)
    return hi, mid, lo


def _rms_rows(x):
    return x * lax.rsqrt(jnp.mean(x * x, axis=-1, keepdims=True) + EPS)


def _group_rms(xc, bd, inv_n):
    hi, lo = _split2(xc * xc)
    ss = _dot(hi, bd) + _dot(lo, bd)
    return xc * lax.rsqrt(ss * inv_n + EPS)


def _rope(xc, tab_ref, cfg, half):
    c = tab_ref[3 * cfg, 0]
    sa = tab_ref[3 * cfg + 1, 0]
    sb = tab_ref[3 * cfg + 2, 0]
    return xc * c + pltpu.roll(xc, LANES - half, 1) * sa + pltpu.roll(xc, half, 1) * sb


def _log_sigmoid(x):
    return jnp.minimum(x, 0.0) - jnp.log1p(jnp.exp(-jnp.abs(x)))


def _sigmoid(x):
    return 1.0 / (1.0 + jnp.exp(-x))


def _silu(x):
    return x * _sigmoid(x)


def _softmax_step(s, v, m, l, acc):
    m_new = jnp.maximum(m, jnp.max(s, axis=-1, keepdims=True))
    alpha = jnp.exp(m - m_new)
    p = jnp.exp(s - m_new)
    l = alpha * l + jnp.sum(p, axis=-1, keepdims=True)
    acc = alpha * acc + _dot(p.astype(BF16), v)
    return m_new, l, acc


def _mod_kernel(c_ref, w_ref, b_ref, o_ref):
    c = c_ref[...]
    a = _silu(c).astype(BF16)
    o_ref[0] = _dot(a, w_ref[0].astype(BF16)) + b_ref[0]


def _modulation(c, ada_w, ada_b):
    L, D, N = ada_w.shape
    B = c.shape[0]
    tn = 1536
    return pl.pallas_call(
        _mod_kernel,
        out_shape=jax.ShapeDtypeStruct((L, B, N), F32),
        grid=(L, N // tn),
        in_specs=[pl.BlockSpec((B, D), lambda l, n: (0, 0)),
                  pl.BlockSpec((1, D, tn), lambda l, n: (l, 0, n)),
                  pl.BlockSpec((1, 1, tn), lambda l, n: (l, 0, n))],
        out_specs=pl.BlockSpec((1, B, tn), lambda l, n: (l, 0, n)),
        compiler_params=pltpu.CompilerParams(vmem_limit_bytes=VMEM_LIMIT),
        name="adaln_mod",
    )(c, ada_w, ada_b.reshape(L, 1, N))


def _rope_tab_kernel(pos_ref, cst_ref, o_ref):
    pos = pos_ref[0].astype(F32)
    for cfg in range(3):
        ang = pos * cst_ref[cfg:cfg + 1, :]
        cs = jnp.cos(ang)
        sn = jnp.sin(ang)
        o_ref[3 * cfg, 0] = cs
        o_ref[3 * cfg + 1, 0] = sn * cst_ref[3 + cfg:4 + cfg, :]
        o_ref[3 * cfg + 2, 0] = sn * cst_ref[6 + cfg:7 + cfg, :]


def _rope_consts():
    rows_f, rows_a, rows_b = [], [], []
    for group, n_rot in ((D_HEAD, NSA_ROT), (LANES, MLA_ROPE), (DIFF_QK, DIFF_ROT)):
        half = n_rot // 2
        inv_freq = ROPE_THETA ** (-jnp.arange(half, dtype=F32) / half)
        d = jnp.arange(LANES) % group
        f = jnp.where(d < n_rot, inv_freq[d % half], 0.0)
        rows_f.append(f)
        rows_a.append(jnp.where(d < half, -1.0, 0.0))
        rows_b.append(jnp.where((d >= half) & (d < n_rot), 1.0, 0.0))
    return jnp.stack(rows_f + rows_a + rows_b).astype(F32)


def _rope_tables(positions):
    B, S = positions.shape
    ts = 512
    return pl.pallas_call(
        _rope_tab_kernel,
        out_shape=jax.ShapeDtypeStruct((9, B, S, LANES), F32),
        grid=(B, S // ts),
        in_specs=[pl.BlockSpec((1, ts, 1), lambda b, i: (b, i, 0)),
                  pl.BlockSpec((9, LANES), lambda b, i: (0, 0))],
        out_specs=pl.BlockSpec((9, 1, ts, LANES), lambda b, i: (0, b, i, 0)),
        name="rope_tables",
    )(positions.reshape(B, S, 1), _rope_consts())


def _proj_kernel(x_ref, mod_ref, tab_ref, w_ref, vec_ref, bd_ref, tri_ref, wuq_ref, wuk_ref, wuv_ref,
                 hbf_ref, nq_ref, nk_ref, nv_ref, nkc_ref, nvc_ref, ng_ref,
                 fq_ref, fk_ref, fv_ref, fcc_ref, fcr_ref,
                 mq_ref, mk_ref, mv_ref, dq_ref, dk_ref, dv_ref, carry_ref):
    i = pl.program_id(1)
    x = x_ref[0]
    sh1 = mod_ref[0, 0, 0:1, :]
    sc1 = mod_ref[0, 0, 1:2, :]
    hb = (_rms_rows(x) * (1.0 + sc1) + sh1).astype(BF16)
    hbf_ref[0] = hb

    def proj(c0, n):
        return _dot(hb, w_ref[0, :, c0:c0 + n])

    def vec(r, n=LANES):
        return vec_ref[0, r:r + 1, 0:n]

    bd64, bd128, bd32 = bd_ref[0], bd_ref[1], bd_ref[2]

    def head_pair(seg, gain_row, bd, inv_n, cfg, half, out_ref):
        for c in range(2):
            y = _group_rms(seg[:, c * LANES:(c + 1) * LANES], bd, inv_n) * vec(gain_row)
            if cfg is not None:
                y = _rope(y, tab_ref, cfg, half)
            yb = y.astype(BF16)
            out_ref[0, 2 * c] = yb[:, :D_HEAD]
            out_ref[0, 2 * c + 1] = yb[:, D_HEAD:]

    def head_store(seg, out_ref):
        sb = seg.astype(BF16)
        for hd in range(seg.shape[1] // D_HEAD):
            out_ref[0, hd] = sb[:, hd * D_HEAD:(hd + 1) * D_HEAD]

    head_pair(proj(C_NQ, 256), V_NQ, bd64, 1.0 / D_HEAD, 0, NSA_ROT // 2, nq_ref)
    ksw = _group_rms(proj(C_NKSW, LANES), bd64, 1.0 / D_HEAD) * vec(V_NK)
    ksw = _rope(ksw, tab_ref, 0, NSA_ROT // 2).astype(BF16)
    nk_ref[0, 0] = ksw[:, :D_HEAD]
    nk_ref[0, 1] = ksw[:, D_HEAD:]
    head_store(proj(C_NVSW, LANES), nv_ref)
    kvc = proj(C_NKVC, LANES)
    nkc_ref[0] = kvc[:, :D_HEAD]
    nvc_ref[0] = kvc[:, D_HEAD:]
    ng_ref[0] = _sigmoid(proj(C_NG, LANES))

    head_pair(proj(C_FQ, 256), V_FQ, bd64, 1.0 / D_HEAD, None, 0, fq_ref)
    head_pair(proj(C_FK, 256), V_FK, bd64, 1.0 / D_HEAD, None, 0, fk_ref)
    head_store(proj(C_FV, 256), fv_ref)

    @pl.when(i == 0)
    def _():
        carry_ref[...] = jnp.zeros_like(carry_ref)

    lf = _log_sigmoid(proj(C_FF, LANES) + vec(V_FB))
    tri = tri_ref[...]
    p0, p1, p2 = _split3(lf)
    cum = _dot(tri, p0) + _dot(tri, p1) + _dot(tri, p2) + carry_ref[0:1, :]
    ts = cum.shape[0]
    carry_ref[0:1, :] = cum[ts - 1:ts, :]
    fcc_ref[0] = cum
    fcr_ref[0] = cum.T[0:8, :]

    cq = proj(C_MCQ, MLA_Q_RANK)
    cqn = (_rms_rows(cq) * vec(V_CQG, MLA_Q_RANK)).astype(BF16)
    qm = _dot(cqn, wuq_ref[0])
    ckv = proj(C_MCKV, MLA_KV_RANK)
    ckvn = (_rms_rows(ckv) * vec(V_CKVG)).astype(BF16)
    kk = _dot(ckvn, wuk_ref[0])
    vv = _dot(ckvn, wuv_ref[0])
    kr = proj(C_MKR, LANES)
    for hd in range(N_HEADS):
        sl = slice(hd * LANES, (hd + 1) * LANES)
        yq = _group_rms(qm[:, sl], bd128, 1.0 / MLA_QK) * vec(V_MQ)
        mq_ref[0, hd] = _rope(yq, tab_ref, 1, MLA_ROPE // 2).astype(BF16)
        yk = _group_rms(kk[:, sl] + kr, bd128, 1.0 / MLA_QK) * vec(V_MK)
        mk_ref[0, hd] = _rope(yk, tab_ref, 1, MLA_ROPE // 2).astype(BF16)
    head_store(vv, mv_ref)

    head_pair(proj(C_DQ, 256), V_DQ, bd32, 1.0 / DIFF_QK, 2, DIFF_ROT // 2, dq_ref)
    head_pair(proj(C_DK, 256), V_DK, bd32, 1.0 / DIFF_QK, 2, DIFF_ROT // 2, dk_ref)
    head_store(proj(C_DV, 256), dv_ref)


def _projection(l, x, mod, tab, w1, vecs, bd, tri, wuq, wuk, wuv):
    B, S, D = x.shape
    ts = TS_PROJ
    H = N_HEADS
    hm = lambda d, dt: jax.ShapeDtypeStruct((B, H, S, d), dt)
    hm_spec = lambda nh, d: pl.BlockSpec((1, nh, ts, d), lambda b, i: (b, 0, i, 0))
    row_spec = lambda d: pl.BlockSpec((1, ts, d), lambda b, i: (b, i, 0))
    out_shape = [
        jax.ShapeDtypeStruct((B, S, D), BF16),
        hm(D_HEAD, BF16),
        jax.ShapeDtypeStruct((B, 2, S, D_HEAD), BF16),
        jax.ShapeDtypeStruct((B, 2, S, D_HEAD), BF16),
        jax.ShapeDtypeStruct((B, S, D_HEAD), F32),
        jax.ShapeDtypeStruct((B, S, D_HEAD), F32),
        jax.ShapeDtypeStruct((B, S, LANES), F32),
        hm(D_HEAD, BF16), hm(D_HEAD, BF16), hm(D_HEAD, BF16),
        jax.ShapeDtypeStruct((B, S, LANES), F32),
        jax.ShapeDtypeStruct((B, 8, S), F32),
        hm(LANES, BF16), hm(LANES, BF16), hm(D_HEAD, BF16),
        hm(D_HEAD, BF16), hm(D_HEAD, BF16), hm(D_HEAD, BF16),
    ]
    out_specs = [
        row_spec(D), hm_spec(H, D_HEAD), hm_spec(2, D_HEAD), hm_spec(2, D_HEAD),
        row_spec(D_HEAD), row_spec(D_HEAD), row_spec(LANES),
        hm_spec(H, D_HEAD), hm_spec(H, D_HEAD), hm_spec(H, D_HEAD),
        row_spec(LANES), pl.BlockSpec((1, 8, ts), lambda b, i: (b, 0, i)),
        hm_spec(H, LANES), hm_spec(H, LANES), hm_spec(H, D_HEAD),
        hm_spec(H, D_HEAD), hm_spec(H, D_HEAD), hm_spec(H, D_HEAD),
    ]
    in_specs = [
        row_spec(D),
        pl.BlockSpec((1, 1, 6, D), lambda b, i: (l, b, 0, 0)),
        pl.BlockSpec((9, 1, ts, LANES), lambda b, i: (0, b, i, 0)),
        pl.BlockSpec((1, D, W_IN_PAD), lambda b, i: (l, 0, 0)),
        pl.BlockSpec((1, N_VEC, 256), lambda b, i: (l, 0, 0)),
        pl.BlockSpec((3, LANES, LANES), lambda b, i: (0, 0, 0)),
        pl.BlockSpec((ts, ts), lambda b, i: (0, 0)),
        pl.BlockSpec((1, MLA_Q_RANK, 512), lambda b, i: (l, 0, 0)),
        pl.BlockSpec((1, MLA_KV_RANK, 512), lambda b, i: (l, 0, 0)),
        pl.BlockSpec((1, MLA_KV_RANK, 256), lambda b, i: (l, 0, 0)),
    ]
    return pl.pallas_call(
        _proj_kernel,
        out_shape=out_shape,
        grid=(B, S // ts),
        in_specs=in_specs,
        out_specs=out_specs,
        scratch_shapes=[pltpu.VMEM((8, LANES), F32)],
        compiler_params=pltpu.CompilerParams(dimension_semantics=("arbitrary", "arbitrary"),
                                             vmem_limit_bytes=VMEM_LIMIT),
        name="in_proj",
    )(x, mod, tab, w1, vecs, bd, tri, wuq, wuk, wuv)


def _cmp_kernel(kc_ref, vc_ref, pe_ref, w1_ref, w2_ref, g_ref, ko_ref, vo_ref):
    half = CMP_STRIDE * D_HEAD
    n = kc_ref.shape[1]

    def compress(x2, j):
        xa = _dot((x2 + pe_ref[0, j:j + 1, 0:half]).astype(BF16), w1_ref[0, j, 0:half, :])
        xb = _dot((x2 + pe_ref[0, j:j + 1, half:2 * half]).astype(BF16), w1_ref[0, j, half:2 * half, :])
        hid = _silu(xa + pltpu.roll(xb, n - 1, 0))
        return _dot(hid.astype(BF16), w2_ref[0, j])

    kcmp = _rms_rows(compress(kc_ref[0], 0)) * g_ref[0, 0:1, 0:D_HEAD]
    ko_ref[0] = kcmp.astype(BF16)
    vo_ref[0] = compress(vc_ref[0], 1).astype(BF16)


def _compress(l, kc2, vc2, pe, w1, w2, g):
    B, n, wide = kc2.shape
    return pl.pallas_call(
        _cmp_kernel,
        out_shape=[jax.ShapeDtypeStruct((B, n, D_HEAD), BF16)] * 2,
        grid=(B,),
        in_specs=[pl.BlockSpec((1, n, wide), lambda b: (b, 0, 0)),
                  pl.BlockSpec((1, n, wide), lambda b: (b, 0, 0)),
                  pl.BlockSpec((1, 2, 2 * wide), lambda b: (l, 0, 0)),
                  pl.BlockSpec((1, 2, 2 * wide, D_HEAD), lambda b: (l, 0, 0, 0)),
                  pl.BlockSpec((1, 2, D_HEAD, D_HEAD), lambda b: (l, 0, 0, 0)),
                  pl.BlockSpec((1, 1, LANES), lambda b: (l, 0, 0))],
        out_specs=[pl.BlockSpec((1, n, D_HEAD), lambda b: (b, 0, 0))] * 2,
        name="nsa_compress",
    )(kc2, vc2, pe, w1, w2, g)


def _nsa_kernel(q_ref, kc_ref, vc_ref, k_ref, v_ref, g_ref, ov_ref, o_ref):
    i = pl.program_id(1)
    tq = TQ_NSA
    H = N_HEADS
    rows = H * tq
    n_blk_lanes = k_ref.shape[2] // SEL_BLOCK
    q = q_ref[0].reshape(rows, D_HEAD)
    t1 = i * tq + lax.broadcasted_iota(jnp.int32, (tq, 1), 0)
    t4 = i * tq + lax.broadcasted_iota(jnp.int32, (rows, 1), 0) % tq

    nc = kc_ref.shape[1]
    sc = _dot_nt(q, kc_ref[0])
    cend = lax.broadcasted_iota(jnp.int32, (1, nc), 1) * CMP_STRIDE + (CMP_BLOCK - 1)
    sc = jnp.where(cend <= t4, sc, NEG_BIG)
    e = jnp.exp(sc - jnp.max(sc, axis=-1, keepdims=True))
    p = e / jnp.sum(e, axis=-1, keepdims=True)
    p = jnp.where(t4 >= CMP_BLOCK - 1, p, 0.0)
    o_cmp = _dot(p.astype(BF16), vc_ref[0])
    psum = p[0:tq] + p[tq:2 * tq] + p[2 * tq:3 * tq] + p[3 * tq:4 * tq]
    p0, p1, p2 = _split3(psum)
    ov = ov_ref[...]
    imp = _dot(p0, ov) + _dot(p1, ov) + _dot(p2, ov)

    lane = lax.broadcasted_iota(jnp.int32, (1, LANES), 1)
    blk = lane % n_blk_lanes
    cur = t1 // SEL_BLOCK
    forced = (blk == 0) | (blk == cur) | (blk == cur - 1)
    score = jnp.where(blk * SEL_BLOCK > t1, -1.0, jnp.where(forced, FORCE_SCORE, imp))
    rank = jnp.zeros((tq, LANES), F32)
    for r in range(1, n_blk_lanes):
        other = pltpu.roll(score, r, 1)
        ahead = (other > score) | ((other == score) & (blk >= r))
        rank = rank + jnp.where(ahead, 1.0, 0.0)
    sel = jnp.where(rank < float(min(SEL_TOPN, n_blk_lanes)), 1.0, 0.0).astype(BF16)
    sel4 = jnp.concatenate([sel] * H, axis=0)

    zero_m = jnp.full((rows, 1), NEG_BIG, F32)
    zero_l = jnp.zeros((rows, 1), F32)
    zero_a = jnp.zeros((rows, D_HEAD), F32)

    def slc_step(j, carry):
        k = k_ref[0, 0, pl.ds(j * TK_SLC, TK_SLC), :]
        v = v_ref[0, 0, pl.ds(j * TK_SLC, TK_SLC), :]
        s = _dot_nt(q, k)
        erow = lax.broadcasted_iota(jnp.int32, (LANES, TK_SLC), 0)
        ecol = lax.broadcasted_iota(jnp.int32, (LANES, TK_SLC), 1)
        expand = jnp.where(erow == j * (TK_SLC // SEL_BLOCK) + ecol // SEL_BLOCK, 1.0, 0.0).astype(BF16)
        chosen = _dot(sel4, expand)
        kpos = j * TK_SLC + lax.broadcasted_iota(jnp.int32, (1, TK_SLC), 1)
        s = jnp.where((chosen > 0.5) & (kpos <= t4), s, NEG_BIG)
        return _softmax_step(s, v, *carry)

    n_slc = (i * tq + tq - 1) // TK_SLC + 1
    _, l_s, a_s = lax.fori_loop(0, n_slc, slc_step, (zero_m, zero_l, zero_a))
    o_slc = a_s / l_s

    def win_step(j, carry):
        k = k_ref[0, 1, pl.ds(j * TK_WIN, TK_WIN), :]
        v = v_ref[0, 1, pl.ds(j * TK_WIN, TK_WIN), :]
        s = _dot_nt(q, k)
        dist = t4 - (j * TK_WIN + lax.broadcasted_iota(jnp.int32, (1, TK_WIN), 1))
        s = jnp.where((dist >= 0) & (dist < WINDOW), s, NEG_BIG)
        return _softmax_step(s, v, *carry)

    lo = jnp.maximum(i * tq - WINDOW, 0) // TK_WIN
    hi = (i * tq + tq - 1) // TK_WIN + 1
    _, l_w, a_w = lax.fori_loop(lo, hi, win_step, (zero_m, zero_l, zero_a))
    o_win = a_w / l_w

    g = g_ref[0]
    outs = []
    for hd in range(H):
        r = slice(hd * tq, (hd + 1) * tq)
        outs.append(g[:, hd:hd + 1] * o_cmp[r] + g[:, H + hd:H + hd + 1] * o_slc[r]
                    + g[:, 2 * H + hd:2 * H + hd + 1] * o_win[r])
    o_ref[0] = jnp.concatenate(outs, axis=1).astype(BF16)


def _nsa_attention(q, kcmp, vcmp, ksw, vsw, gates, overlap):
    B, H, S, _ = q.shape
    tq = TQ_NSA
    nc = kcmp.shape[1]
    return pl.pallas_call(
        _nsa_kernel,
        out_shape=jax.ShapeDtypeStruct((B, S, MIX_WIDTH), BF16),
        grid=(B, S // tq),
        in_specs=[pl.BlockSpec((1, H, tq, D_HEAD), lambda b, i: (b, 0, i, 0)),
                  pl.BlockSpec((1, nc, D_HEAD), lambda b, i: (b, 0, 0)),
                  pl.BlockSpec((1, nc, D_HEAD), lambda b, i: (b, 0, 0)),
                  pl.BlockSpec((1, 2, S, D_HEAD), lambda b, i: (b, 0, 0, 0)),
                  pl.BlockSpec((1, 2, S, D_HEAD), lambda b, i: (b, 0, 0, 0)),
                  pl.BlockSpec((1, tq, LANES), lambda b, i: (b, i, 0)),
                  pl.BlockSpec((LANES, LANES), lambda b, i: (0, 0))],
        out_specs=pl.BlockSpec((1, tq, MIX_WIDTH), lambda b, i: (b, i, 0)),
        compiler_params=pltpu.CompilerParams(vmem_limit_bytes=VMEM_LIMIT),
        name="nsa_attention",
    )(q, kcmp, vcmp, ksw, vsw, gates, overlap)


def _causal_head(q, k_tile, v_tile, bias_tile, i, tq):
    def scores(j):
        s = _dot_nt(q, k_tile(j))
        b = bias_tile(j)
        return s if b is None else s + b

    dv = v_tile(0).shape[-1]
    rr = lax.broadcasted_iota(jnp.int32, (tq, tq), 0)
    cc = lax.broadcasted_iota(jnp.int32, (tq, tq), 1)
    s = jnp.where(cc <= rr, scores(i), NEG_BIG)
    m = jnp.max(s, axis=-1, keepdims=True)
    p = jnp.exp(s - m)
    l = jnp.sum(p, axis=-1, keepdims=True)
    acc = _dot(p.astype(BF16), v_tile(i))

    def step(j, carry):
        return _softmax_step(scores(j), v_tile(j), *carry)

    _, l, acc = lax.fori_loop(0, i, step, (m, l, acc))
    return acc / l


def _attn_kernel(mode, lam_init, *refs):
    if mode == "fox":
        q_ref, k_ref, v_ref, cc_ref, cr_ref, o_ref = refs
    elif mode == "diff":
        q_ref, k_ref, v_ref, lam_ref, og_ref, o_ref = refs
    else:
        q_ref, k_ref, v_ref, o_ref = refs
    i = pl.program_id(1)
    tq = TQ_ATT
    outs = []
    for hd in range(N_HEADS):
        k_tile = lambda j, hd=hd: k_ref[0, hd, pl.ds(j * tq, tq), :]
        v_tile = lambda j, hd=hd: v_ref[0, hd, pl.ds(j * tq, tq), :]
        q = q_ref[0, hd]
        if mode == "fox":
            qc = cc_ref[0][:, hd:hd + 1]
            bias = lambda j, hd=hd, qc=qc: qc - cr_ref[0, hd:hd + 1, pl.ds(j * tq, tq)]
            outs.append(_causal_head(q, k_tile, v_tile, bias, i, tq))
        elif mode == "diff":
            no_bias = lambda j: None
            lane = lax.broadcasted_iota(jnp.int32, (1, D_HEAD), 1)
            qa = jnp.where(lane < DIFF_QK, q, jnp.zeros_like(q))
            qb = jnp.where(lane >= DIFF_QK, q, jnp.zeros_like(q))
            lv = lam_ref[0]
            lam = (jnp.exp(jnp.sum(lv[0:1] * lv[1:2], axis=-1, keepdims=True))
                   - jnp.exp(jnp.sum(lv[2:3] * lv[3:4], axis=-1, keepdims=True)) + lam_init)
            o = (_causal_head(qa, k_tile, v_tile, no_bias, i, tq)
                 - lam * _causal_head(qb, k_tile, v_tile, no_bias, i, tq))
            outs.append(_rms_rows(o) * og_ref[0] * (1.0 - lam_init))
        else:
            outs.append(_causal_head(q, k_tile, v_tile, lambda j: None, i, tq))
    o_ref[0] = jnp.concatenate(outs, axis=1).astype(BF16)


def _dense_attention(mode, l, lam_init, q, k, v, *extra):
    B, H, S, dk = q.shape
    dv = v.shape[-1]
    tq = TQ_ATT
    in_specs = [pl.BlockSpec((1, H, tq, dk), lambda b, i: (b, 0, i, 0)),
                pl.BlockSpec((1, H, S, dk), lambda b, i: (b, 0, 0, 0)),
                pl.BlockSpec((1, H, S, dv), lambda b, i: (b, 0, 0, 0))]
    if mode == "fox":
        in_specs += [pl.BlockSpec((1, tq, LANES), lambda b, i: (b, i, 0)),
                     pl.BlockSpec((1, 8, S), lambda b, i: (b, 0, 0))]
    elif mode == "diff":
        in_specs += [pl.BlockSpec((1, 4, DIFF_QK), lambda b, i: (l, 0, 0)),
                     pl.BlockSpec((1, 1, D_HEAD), lambda b, i: (l, 0, 0))]
    return pl.pallas_call(
        functools.partial(_attn_kernel, mode, lam_init),
        out_shape=jax.ShapeDtypeStruct((B, S, H * dv), BF16),
        grid=(B, S // tq),
        in_specs=in_specs,
        out_specs=pl.BlockSpec((1, tq, H * dv), lambda b, i: (b, i, 0)),
        compiler_params=pltpu.CompilerParams(vmem_limit_bytes=VMEM_LIMIT),
        name=mode + "_attention",
    )(q, k, v, *extra)


def _merge_kernel(x_ref, h_ref, o0_ref, o1_ref, o2_ref, o3_ref, mod_ref, brw_ref, gw_ref, gb_ref, wo_ref,
                  out_ref):
    hb = h_ref[0]
    merged = None
    for m, o_ref in enumerate((o0_ref, o1_ref, o2_ref, o3_ref)):
        y = _dot(o_ref[0], brw_ref[0, m])
        cols = slice(m * D_MODEL, (m + 1) * D_MODEL)
        gate = _sigmoid(_dot(hb, gw_ref[0, :, cols]) + gb_ref[0, :, cols])
        merged = gate * y if merged is None else merged + gate * y
    out = _dot(merged.astype(BF16), wo_ref[0])
    out_ref[0] = x_ref[0] + mod_ref[0, 0, 2:3, :] * out


def _merge(l, x, hbf, o_nsa, o_fox, o_mla, o_diff, mod, brw, gw, gb, wo):
    B, S, D = x.shape
    ts = TS_DENSE
    row = lambda d: pl.BlockSpec((1, ts, d), lambda b, i: (b, i, 0))
    return pl.pallas_call(
        _merge_kernel,
        out_shape=jax.ShapeDtypeStruct((B, S, D), F32),
        grid=(B, S // ts),
        in_specs=[row(D), row(D), row(MIX_WIDTH), row(MIX_WIDTH), row(MIX_WIDTH), row(MIX_WIDTH),
                  pl.BlockSpec((1, 1, 6, D), lambda b, i: (l, b, 0, 0)),
                  pl.BlockSpec((1, 4, MIX_WIDTH, D), lambda b, i: (l, 0, 0, 0)),
                  pl.BlockSpec((1, D, 4 * D), lambda b, i: (l, 0, 0)),
                  pl.BlockSpec((1, 1, 4 * D), lambda b, i: (l, 0, 0)),
                  pl.BlockSpec((1, D, D), lambda b, i: (l, 0, 0))],
        out_specs=row(D),
        compiler_params=pltpu.CompilerParams(vmem_limit_bytes=VMEM_LIMIT),
        name="merge_out",
    )(x, hbf, o_nsa, o_fox, o_mla, o_diff, mod, brw, gw, gb, wo)


def _ffn_kernel(x_ref, mod_ref, wup_ref, cw_ref, cb_ref, wd_ref, out_ref, carry_ref):
    i = pl.program_id(1)
    x = x_ref[0]
    ts = x.shape[0]
    sh2 = mod_ref[0, 0, 3:4, :]
    sc2 = mod_ref[0, 0, 4:5, :]
    hb = (_rms_rows(x) * (1.0 + sc2) + sh2).astype(BF16)

    @pl.when(i == 0)
    def _():
        carry_ref[...] = jnp.zeros_like(carry_ref)

    row = lax.broadcasted_iota(jnp.int32, (ts, 1), 0)
    acc = None
    for c in range(D_FF // FF_CHUNK):
        cols = slice(c * FF_CHUNK, (c + 1) * FF_CHUNK)
        g = _dot(hb, wup_ref[0, :, cols])
        v = _dot(hb, wup_ref[0, :, D_FF + c * FF_CHUNK:D_FF + (c + 1) * FF_CHUNK])
        prev = carry_ref[c]
        g1 = jnp.where(row == 0, prev[7:8, :], pltpu.roll(g, 1, 0))
        g2 = jnp.where(row == 0, prev[6:7, :], jnp.where(row == 1, prev[7:8, :], pltpu.roll(g, 2, 0)))
        carry_ref[c] = g[ts - 8:ts, :]
        conv = (cw_ref[0, 0:1, cols] * g2 + cw_ref[0, 1:2, cols] * g1 + cw_ref[0, 2:3, cols] * g
                + cb_ref[0, :, cols])
        a = (_silu(conv) * v).astype(BF16)
        part = _dot(a, wd_ref[0, cols, :])
        acc = part if acc is None else acc + part
    out_ref[0] = x + mod_ref[0, 0, 5:6, :] * acc


def _ffn(l, x, mod, wup, cw, cb, wd):
    B, S, D = x.shape
    ts = TS_DENSE
    const = pl.Buffered(1)
    return pl.pallas_call(
        _ffn_kernel,
        out_shape=jax.ShapeDtypeStruct((B, S, D), F32),
        grid=(B, S // ts),
        in_specs=[pl.BlockSpec((1, ts, D), lambda b, i: (b, i, 0)),
                  pl.BlockSpec((1, 1, 6, D), lambda b, i: (l, b, 0, 0)),
                  pl.BlockSpec((1, D, 2 * D_FF), lambda b, i: (l, 0, 0), pipeline_mode=const),
                  pl.BlockSpec((1, 3, D_FF), lambda b, i: (l, 0, 0)),
                  pl.BlockSpec((1, 1, D_FF), lambda b, i: (l, 0, 0)),
                  pl.BlockSpec((1, D_FF, D), lambda b, i: (l, 0, 0), pipeline_mode=const)],
        out_specs=pl.BlockSpec((1, ts, D), lambda b, i: (b, i, 0)),
        scratch_shapes=[pltpu.VMEM((D_FF // FF_CHUNK, 8, FF_CHUNK), F32)],
        compiler_params=pltpu.CompilerParams(dimension_semantics=("arbitrary", "arbitrary"),
                                             vmem_limit_bytes=VMEM_LIMIT),
        name="conv_ffn",
    )(x, mod, wup, cw, cb, wd)


def _pad_cols(w, n):
    return jnp.pad(w, [(0, 0)] * (w.ndim - 1) + [(0, n - w.shape[-1])])


def _layout_w_in(w_in):
    o = 0
    seg = {}
    for name, n in (("nq", 256), ("nkc", 64), ("nvc", 64), ("nks", 64), ("nvs", 64), ("nkw", 64), ("nvw", 64),
                    ("ng", 12), ("fq", 256), ("fk", 256), ("fv", 256), ("ff", 4),
                    ("mcq", 256), ("mckv", 128), ("mkr", 32), ("dq", 256), ("dk", 256), ("dv", 256)):
        seg[name] = w_in[..., o:o + n]
        o += n
    parts = [seg["nq"], seg["nks"], seg["nkw"], seg["nvs"], seg["nvw"], seg["nkc"], seg["nvc"],
             _pad_cols(seg["ng"], LANES),
             seg["fq"], seg["fk"], seg["fv"], _pad_cols(seg["ff"], LANES),
             seg["mcq"], seg["mckv"], _pad_cols(seg["mkr"], LANES),
             seg["dq"], seg["dk"], seg["dv"]]
    return jnp.concatenate(parts, axis=-1).astype(BF16)


def _pack_vecs(nsa_qk_g, fox_qk_g, fox_f_b, mla_cq_g, mla_ckv_g, mla_qk_g, diff_qk_g):
    L = nsa_qk_g.shape[0]
    t2 = lambda g: jnp.concatenate([g, g], axis=-1)
    mla_pad = lambda g: _pad_cols(g, LANES)
    rows = [None] * N_VEC
    rows[V_NQ] = t2(nsa_qk_g[:, 0]) * (D_HEAD ** -0.5)
    rows[V_NK] = jnp.concatenate([nsa_qk_g[:, 2], nsa_qk_g[:, 3]], axis=-1)
    rows[V_FQ] = t2(fox_qk_g[:, 0]) * (D_HEAD ** -0.5)
    rows[V_FK] = t2(fox_qk_g[:, 1])
    rows[V_MQ] = mla_pad(mla_qk_g[:, 0]) * (MLA_QK ** -0.5)
    rows[V_MK] = mla_pad(mla_qk_g[:, 1])
    rows[V_DQ] = jnp.tile(diff_qk_g[:, 0], (1, 4)) * (DIFF_QK ** -0.5)
    rows[V_DK] = jnp.tile(diff_qk_g[:, 1], (1, 4))
    rows[V_FB] = fox_f_b
    rows[V_CQG] = mla_cq_g
    rows[V_CKVG] = mla_ckv_g
    rows = [jnp.zeros((L, 256), F32) if r is None else _pad_cols(r.astype(F32), 256) for r in rows]
    return jnp.stack(rows, axis=1)


def _block_diag_ones():
    d = jnp.arange(LANES)
    mats = [(d[:, None] // g == d[None, :] // g) for g in (D_HEAD, LANES, DIFF_QK)]
    return jnp.stack(mats).astype(BF16)


def _overlap_matrix(n_cmp_rows, n_blk):
    c = jnp.arange(n_cmp_rows)[:, None] * CMP_STRIDE
    b = (jnp.arange(LANES)[None, :] % n_blk) * SEL_BLOCK
    ov = (c < b + SEL_BLOCK) & (c + CMP_BLOCK > b)
    return ov.astype(BF16)


def kernel(x, c, positions, ada_w, ada_b, w_in, nsa_qk_g, nsa_cmp_pe, nsa_cmp_w1, nsa_cmp_w2, fox_qk_g, fox_f_b,
           mla_cq_g, mla_ckv_g, mla_w_uq, mla_w_ukv, mla_qk_g, diff_qk_g, diff_lambda, diff_out_g, br_w, gate_w,
           gate_b, w_out, ffn_w_up, ffn_conv_w, ffn_conv_b, ffn_w_down):
    B, S, D = x.shape
    L = ada_w.shape[0]
    H = N_HEADS
    n_half = S // CMP_STRIDE

    mod = _modulation(c, ada_w, ada_b).reshape(L, B, 6, D)
    tab = _rope_tables(positions)

    w1 = _layout_w_in(w_in)
    vecs = _pack_vecs(nsa_qk_g, fox_qk_g, fox_f_b, mla_cq_g, mla_ckv_g, mla_qk_g, diff_qk_g)
    bd = _block_diag_ones()
    tri = (jnp.arange(TS_PROJ)[None, :] <= jnp.arange(TS_PROJ)[:, None]).astype(BF16)
    overlap = _overlap_matrix(n_half, S // SEL_BLOCK)
    wuq = _pad_cols(mla_w_uq.reshape(L, MLA_Q_RANK, H, MLA_QK), LANES).reshape(L, MLA_Q_RANK, H * LANES)
    ukv = mla_w_ukv.reshape(L, MLA_KV_RANK, H, MLA_NOPE + D_HEAD)
    wuk = jnp.pad(ukv[..., :MLA_NOPE], [(0, 0), (0, 0), (0, 0), (MLA_ROPE, LANES - MLA_QK)])
    wuk = wuk.reshape(L, MLA_KV_RANK, H * LANES).astype(BF16)
    wuv = ukv[..., MLA_NOPE:].reshape(L, MLA_KV_RANK, H * D_HEAD).astype(BF16)
    wuq = wuq.astype(BF16)
    pe = nsa_cmp_pe.reshape(L, 2, CMP_BLOCK * D_HEAD)
    cw1 = nsa_cmp_w1.astype(BF16)
    cw2 = nsa_cmp_w2.astype(BF16)
    kcg = _pad_cols(nsa_qk_g[:, 1], LANES).reshape(L, 1, LANES)
    brw = br_w.astype(BF16)
    gw = gate_w.astype(BF16)
    gb = gate_b.reshape(L, 1, 4 * D)
    wo = w_out.astype(BF16)
    wup = ffn_w_up.astype(BF16)
    wd = ffn_w_down.astype(BF16)
    cb = ffn_conv_b.reshape(L, 1, D_FF)
    og = diff_out_g.reshape(L, 1, D_HEAD)

    for l in range(L):
        lam_init = 0.8 - 0.6 * math.exp(-0.3 * l)
        (hbf, nq, nk, nv, nkc, nvc, ng, fq, fk, fv, fcc, fcr, mq, mk, mv, dq, dk, dv) = _projection(
            l, x, mod, tab, w1, vecs, bd, tri, wuq, wuk, wuv)
        kcmp, vcmp = _compress(l, nkc.reshape(B, n_half, CMP_STRIDE * D_HEAD),
                               nvc.reshape(B, n_half, CMP_STRIDE * D_HEAD), pe, cw1, cw2, kcg)
        o_nsa = _nsa_attention(nq, kcmp, vcmp, nk, nv, ng, overlap)
        o_fox = _dense_attention("fox", l, lam_init, fq, fk, fv, fcc, fcr)
        o_mla = _dense_attention("mla", l, lam_init, mq, mk, mv)
        o_diff = _dense_attention("diff", l, lam_init, dq, dk, dv, diff_lambda, og)
        x = _merge(l, x, hbf, o_nsa, o_fox, o_mla, o_diff, mod, brw, gw, gb, wo)
        x = _ffn(l, x, mod, wup, ffn_conv_w, cb, wd)
    return x
```

```python
import functools
import math

import jax
import jax.numpy as jnp
from jax import lax
from jax.experimental import pallas as pl
from jax.experimental.pallas import tpu as pltpu

F32 = jnp.float32
BF16 = jnp.bfloat16

D_MODEL = 1024
D_HEAD = 64
N_HEADS = 4
MIX_WIDTH = 256
NSA_ROT = 16
CMP_BLOCK = 32
CMP_STRIDE = 16
SEL_BLOCK = 64
SEL_TOPN = 16
WINDOW = 512
MLA_Q_RANK = 256
MLA_KV_RANK = 128
MLA_NOPE = 64
MLA_ROPE = 32
MLA_QK = MLA_ROPE + MLA_NOPE
DIFF_QK = 32
DIFF_ROT = 8
D_FF = 2816
ROPE_THETA = 500000.0
EPS = 1e-6
FORCE_SCORE = 1e4
NEG_BIG = -1e30

LANES = 128
VMEM_LIMIT = 56 * 1024 * 1024

C_NQ, C_NKSW, C_NVSW, C_NKVC, C_NG = 0, 256, 384, 512, 640
C_FQ, C_FK, C_FV, C_FF = 768, 1024, 1280, 1536
C_MCQ, C_MCKV, C_MKR = 1664, 1920, 2048
C_DQ, C_DK, C_DV = 2176, 2432, 2688
W_IN_PAD = 2944

V_NQ, V_NK, V_FQ, V_FK, V_MQ, V_MK, V_DQ, V_DK, V_FB, V_CQG, V_CKVG = range(11)
N_VEC = 16

TS_PROJ = 256
TS_DENSE = 512
TQ_ATT = 512
TQ_NSA = 256
TK_NSA = 256
FF_CHUNK = 1408
LOG2E = math.log2(math.e)


def _dot(a, b):
    return jnp.dot(a, b, preferred_element_type=F32)


def _dot_nt(a, b):
    return lax.dot_general(a, b, (((1,), (1,)), ((), ())), preferred_element_type=F32)


def _split2(x):
    hi = x.astype(BF16)
    lo = (x - hi.astype(F32)).astype(BF16)
    return hi, lo


def _split3(x):
    hi = x.astype(BF16)
    r = x - hi.astype(F32)
    mid = r.astype(BF16)
    lo = (r - mid.astype(F32)).astype(BF16)
    return hi, mid, lo


def _rms_rows(x):
    return x * lax.rsqrt(jnp.mean(x * x, axis=-1, keepdims=True) + EPS)


def _group_rms(xc, bd, inv_n):
    hi, lo = _split2(xc * xc)
    ss = _dot(hi, bd) + _dot(lo, bd)
    return xc * lax.rsqrt(ss * inv_n + EPS)


def _rope(xc, tab_ref, cfg, half):
    c = tab_ref[3 * cfg, 0]
    sa = tab_ref[3 * cfg + 1, 0]
    sb = tab_ref[3 * cfg + 2, 0]
    return xc * c + pltpu.roll(xc, LANES - half, 1) * sa + pltpu.roll(xc, half, 1) * sb


def _log_sigmoid(x):
    return jnp.minimum(x, 0.0) - jnp.log1p(jnp.exp(-jnp.abs(x)))


def _sigmoid(x):
    return 1.0 / (1.0 + jnp.exp(-x))


def _silu(x):
    return x * _sigmoid(x)


def _softmax_step(s, v1, m, acc):
    m_new = jnp.maximum(m, jnp.max(s, axis=-1, keepdims=True))
    p = jnp.exp2(s - m_new).astype(BF16)
    return m_new, jnp.exp2(m - m_new) * acc + _dot(p, v1)


def _softmax_finish(acc):
    return acc[:, :D_HEAD] / acc[:, D_HEAD:D_HEAD + 1]


def _mod_kernel(c_ref, w_ref, b_ref, o_ref):
    c = c_ref[...]
    a = _silu(c).astype(BF16)
    o_ref[0] = _dot(a, w_ref[0].astype(BF16)) + b_ref[0]


def _modulation(c, ada_w, ada_b):
    L, D, N = ada_w.shape
    B = c.shape[0]
    tn = 1536
    return pl.pallas_call(
        _mod_kernel,
        out_shape=jax.ShapeDtypeStruct((L, B, N), F32),
        grid=(L, N // tn),
        in_specs=[pl.BlockSpec((B, D), lambda l, n: (0, 0)),
                  pl.BlockSpec((1, D, tn), lambda l, n: (l, 0, n)),
                  pl.BlockSpec((1, 1, tn), lambda l, n: (l, 0, n))],
        out_specs=pl.BlockSpec((1, B, tn), lambda l, n: (l, 0, n)),
        compiler_params=pltpu.CompilerParams(vmem_limit_bytes=VMEM_LIMIT),
        name="adaln_mod",
    )(c, ada_w, ada_b.reshape(L, 1, N))


def _rope_tab_kernel(pos_ref, cst_ref, o_ref):
    pos = pos_ref[0].astype(F32)
    for cfg in range(3):
        ang = pos * cst_ref[cfg:cfg + 1, :]
        cs = jnp.cos(ang)
        sn = jnp.sin(ang)
        o_ref[3 * cfg, 0] = cs
        o_ref[3 * cfg + 1, 0] = sn * cst_ref[3 + cfg:4 + cfg, :]
        o_ref[3 * cfg + 2, 0] = sn * cst_ref[6 + cfg:7 + cfg, :]


def _rope_consts():
    rows_f, rows_a, rows_b = [], [], []
    for group, n_rot in ((D_HEAD, NSA_ROT), (LANES, MLA_ROPE), (DIFF_QK, DIFF_ROT)):
        half = n_rot // 2
        inv_freq = ROPE_THETA ** (-jnp.arange(half, dtype=F32) / half)
        d = jnp.arange(LANES) % group
        f = jnp.where(d < n_rot, inv_freq[d % half], 0.0)
        rows_f.append(f)
        rows_a.append(jnp.where(d < half, -1.0, 0.0))
        rows_b.append(jnp.where((d >= half) & (d < n_rot), 1.0, 0.0))
    return jnp.stack(rows_f + rows_a + rows_b).astype(F32)


def _rope_tables(positions):
    B, S = positions.shape
    ts = 512
    return pl.pallas_call(
        _rope_tab_kernel,
        out_shape=jax.ShapeDtypeStruct((9, B, S, LANES), F32),
        grid=(B, S // ts),
        in_specs=[pl.BlockSpec((1, ts, 1), lambda b, i: (b, i, 0)),
                  pl.BlockSpec((9, LANES), lambda b, i: (0, 0))],
        out_specs=pl.BlockSpec((9, 1, ts, LANES), lambda b, i: (0, b, i, 0)),
        name="rope_tables",
    )(positions.reshape(B, S, 1), _rope_consts())


def _proj_kernel(x_ref, mod_ref, tab_ref, w_ref, vec_ref, bd_ref, tri_ref, wuq_ref, wuk_ref, wuv_ref,
                 hbf_ref, nq_ref, nk_ref, nv_ref, nkc_ref, nvc_ref, ng_ref,
                 fq_ref, fk_ref, fv_ref, fcr_ref,
                 mq_ref, mk_ref, mv_ref, dq_ref, dk_ref, dv_ref, carry_ref):
    i = pl.program_id(1)
    x = x_ref[0]
    sh1 = mod_ref[0, 0, 0:1, :]
    sc1 = mod_ref[0, 0, 1:2, :]
    hb = (_rms_rows(x) * (1.0 + sc1) + sh1).astype(BF16)
    hbf_ref[0] = hb

    def proj(c0, n):
        return _dot(hb, w_ref[0, :, c0:c0 + n])

    def vec(r, n=LANES):
        return vec_ref[0, r:r + 1, 0:n]

    bd64, bd128, bd32 = bd_ref[0], bd_ref[1], bd_ref[2]

    def head_pair(seg, gain_row, bd, inv_n, cfg, half, out_ref):
        for c in range(2):
            y = _group_rms(seg[:, c * LANES:(c + 1) * LANES], bd, inv_n) * vec(gain_row)
            if cfg is not None:
                y = _rope(y, tab_ref, cfg, half)
            yb = y.astype(BF16)
            out_ref[0, 2 * c] = yb[:, :D_HEAD]
            out_ref[0, 2 * c + 1] = yb[:, D_HEAD:]

    def value_store(seg, out_ref):
        lane = lax.broadcasted_iota(jnp.int32, (1, LANES), 1)
        ones_col = jnp.where(lane == D_HEAD, 1.0, 0.0)
        for c in range(seg.shape[1] // LANES):
            ch = seg[:, c * LANES:(c + 1) * LANES]
            out_ref[0, 2 * c] = jnp.where(lane < D_HEAD, ch, ones_col).astype(BF16)
            out_ref[0, 2 * c + 1] = jnp.where(lane < D_HEAD, pltpu.roll(ch, D_HEAD, 1), ones_col).astype(BF16)

    head_pair(proj(C_NQ, 256), V_NQ, bd64, 1.0 / D_HEAD, 0, NSA_ROT // 2, nq_ref)
    ksw = _group_rms(proj(C_NKSW, LANES), bd64, 1.0 / D_HEAD) * vec(V_NK)
    ksw = _rope(ksw, tab_ref, 0, NSA_ROT // 2).astype(BF16)
    nk_ref[0, 0] = ksw[:, :D_HEAD]
    nk_ref[0, 1] = ksw[:, D_HEAD:]
    value_store(proj(C_NVSW, LANES), nv_ref)
    kvc = proj(C_NKVC, LANES)
    nkc_ref[0] = kvc[:, :D_HEAD]
    nvc_ref[0] = kvc[:, D_HEAD:]
    ng_ref[0] = _sigmoid(proj(C_NG, LANES))

    head_pair(proj(C_FQ, 256), V_FQ, bd64, 1.0 / D_HEAD, None, 0, fq_ref)
    head_pair(proj(C_FK, 256), V_FK, bd64, 1.0 / D_HEAD, None, 0, fk_ref)
    value_store(proj(C_FV, 256), fv_ref)

    @pl.when(i == 0)
    def _():
        carry_ref[...] = jnp.zeros_like(carry_ref)

    lf = _log_sigmoid(proj(C_FF, LANES) + vec(V_FB))
    tri = tri_ref[...]
    p0, p1, p2 = _split3(lf)
    cum = _dot(tri, p0) + _dot(tri, p1) + _dot(tri, p2) + carry_ref[0:1, :]
    ts = cum.shape[0]
    carry_ref[0:1, :] = cum[ts - 1:ts, :]
    fcr_ref[0] = (cum * (-LOG2E)).T[0:8, :]

    cq = proj(C_MCQ, MLA_Q_RANK)
    cqn = (_rms_rows(cq) * vec(V_CQG, MLA_Q_RANK)).astype(BF16)
    qm = _dot(cqn, wuq_ref[0])
    ckv = proj(C_MCKV, MLA_KV_RANK)
    ckvn = (_rms_rows(ckv) * vec(V_CKVG)).astype(BF16)
    kk = _dot(ckvn, wuk_ref[0])
    vv = _dot(ckvn, wuv_ref[0])
    kr = proj(C_MKR, LANES)
    for hd in range(N_HEADS):
        sl = slice(hd * LANES, (hd + 1) * LANES)
        yq = _group_rms(qm[:, sl], bd128, 1.0 / MLA_QK) * vec(V_MQ)
        mq_ref[0, hd] = _rope(yq, tab_ref, 1, MLA_ROPE // 2).astype(BF16)
        yk = _group_rms(kk[:, sl] + kr, bd128, 1.0 / MLA_QK) * vec(V_MK)
        mk_ref[0, hd] = _rope(yk, tab_ref, 1, MLA_ROPE // 2).astype(BF16)
    value_store(vv, mv_ref)

    head_pair(proj(C_DQ, 256), V_DQ, bd32, 1.0 / DIFF_QK, 2, DIFF_ROT // 2, dq_ref)
    head_pair(proj(C_DK, 256), V_DK, bd32, 1.0 / DIFF_QK, 2, DIFF_ROT // 2, dk_ref)
    value_store(proj(C_DV, 256), dv_ref)


def _projection(l, x, mod, tab, w1, vecs, bd, tri, wuq, wuk, wuv):
    B, S, D = x.shape
    ts = TS_PROJ
    H = N_HEADS
    hm = lambda d, dt: jax.ShapeDtypeStruct((B, H, S, d), dt)
    hm_spec = lambda nh, d: pl.BlockSpec((1, nh, ts, d), lambda b, i: (b, 0, i, 0))
    row_spec = lambda d: pl.BlockSpec((1, ts, d), lambda b, i: (b, i, 0))
    out_shape = [
        jax.ShapeDtypeStruct((B, S, D), BF16),
        hm(D_HEAD, BF16),
        jax.ShapeDtypeStruct((B, 2, S, D_HEAD), BF16),
        jax.ShapeDtypeStruct((B, 2, S, LANES), BF16),
        jax.ShapeDtypeStruct((B, S, D_HEAD), F32),
        jax.ShapeDtypeStruct((B, S, D_HEAD), F32),
        jax.ShapeDtypeStruct((B, S, LANES), F32),
        hm(D_HEAD, BF16), hm(D_HEAD, BF16), hm(LANES, BF16),
        jax.ShapeDtypeStruct((B, 8, S), F32),
        hm(LANES, BF16), hm(LANES, BF16), hm(LANES, BF16),
        hm(D_HEAD, BF16), hm(D_HEAD, BF16), hm(LANES, BF16),
    ]
    out_specs = [
        row_spec(D), hm_spec(H, D_HEAD), hm_spec(2, D_HEAD), hm_spec(2, LANES),
        row_spec(D_HEAD), row_spec(D_HEAD), row_spec(LANES),
        hm_spec(H, D_HEAD), hm_spec(H, D_HEAD), hm_spec(H, LANES),
        pl.BlockSpec((1, 8, ts), lambda b, i: (b, 0, i)),
        hm_spec(H, LANES), hm_spec(H, LANES), hm_spec(H, LANES),
        hm_spec(H, D_HEAD), hm_spec(H, D_HEAD), hm_spec(H, LANES),
    ]
    in_specs = [
        row_spec(D),
        pl.BlockSpec((1, 1, 6, D), lambda b, i: (l, b, 0, 0)),
        pl.BlockSpec((9, 1, ts, LANES), lambda b, i: (0, b, i, 0)),
        pl.BlockSpec((1, D, W_IN_PAD), lambda b, i: (l, 0, 0)),
        pl.BlockSpec((1, N_VEC, 256), lambda b, i: (l, 0, 0)),
        pl.BlockSpec((3, LANES, LANES), lambda b, i: (0, 0, 0)),
        pl.BlockSpec((ts, ts), lambda b, i: (0, 0)),
        pl.BlockSpec((1, MLA_Q_RANK, 512), lambda b, i: (l, 0, 0)),
        pl.BlockSpec((1, MLA_KV_RANK, 512), lambda b, i: (l, 0, 0)),
        pl.BlockSpec((1, MLA_KV_RANK, 256), lambda b, i: (l, 0, 0)),
    ]
    return pl.pallas_call(
        _proj_kernel,
        out_shape=out_shape,
        grid=(B, S // ts),
        in_specs=in_specs,
        out_specs=out_specs,
        scratch_shapes=[pltpu.VMEM((8, LANES), F32)],
        compiler_params=pltpu.CompilerParams(dimension_semantics=("arbitrary", "arbitrary"),
                                             vmem_limit_bytes=VMEM_LIMIT),
        name="in_proj",
    )(x, mod, tab, w1, vecs, bd, tri, wuq, wuk, wuv)


def _cmp_kernel(kc_ref, vc_ref, pe_ref, w1_ref, w2_ref, g_ref, ko_ref, vo_ref):
    half = CMP_STRIDE * D_HEAD
    n = kc_ref.shape[1]

    def compress(x2, j):
        xa = _dot((x2 + pe_ref[0, j:j + 1, 0:half]).astype(BF16), w1_ref[0, j, 0:half, :])
        xb = _dot((x2 + pe_ref[0, j:j + 1, half:2 * half]).astype(BF16), w1_ref[0, j, half:2 * half, :])
        hid = _silu(xa + pltpu.roll(xb, n - 1, 0))
        return _dot(hid.astype(BF16), w2_ref[0, j])

    kcmp = _rms_rows(compress(kc_ref[0], 0)) * g_ref[0, 0:1, 0:D_HEAD]
    ko_ref[0] = kcmp.astype(BF16)
    vo_ref[0] = compress(vc_ref[0], 1).astype(BF16)


def _compress(l, kc2, vc2, pe, w1, w2, g):
    B, n, wide = kc2.shape
    return pl.pallas_call(
        _cmp_kernel,
        out_shape=[jax.ShapeDtypeStruct((B, n, D_HEAD), BF16)] * 2,
        grid=(B,),
        in_specs=[pl.BlockSpec((1, n, wide), lambda b: (b, 0, 0)),
                  pl.BlockSpec((1, n, wide), lambda b: (b, 0, 0)),
                  pl.BlockSpec((1, 2, 2 * wide), lambda b: (l, 0, 0)),
                  pl.BlockSpec((1, 2, 2 * wide, D_HEAD), lambda b: (l, 0, 0, 0)),
                  pl.BlockSpec((1, 2, D_HEAD, D_HEAD), lambda b: (l, 0, 0, 0)),
                  pl.BlockSpec((1, 1, LANES), lambda b: (l, 0, 0))],
        out_specs=[pl.BlockSpec((1, n, D_HEAD), lambda b: (b, 0, 0))] * 2,
        name="nsa_compress",
    )(kc2, vc2, pe, w1, w2, g)


def _nsa_kernel(q_ref, kc_ref, vc_ref, k_ref, v_ref, g_ref, ov_ref, o_ref):
    i = pl.program_id(1)
    tq, tk = TQ_NSA, TK_NSA
    H = N_HEADS
    rows = H * tq
    n_blk_lanes = k_ref.shape[2] // SEL_BLOCK
    q = q_ref[0].reshape(rows, D_HEAD)
    t1 = i * tq + lax.broadcasted_iota(jnp.int32, (tq, 1), 0)
    t4 = i * tq + lax.broadcasted_iota(jnp.int32, (rows, 1), 0) % tq

    nc = kc_ref.shape[1]
    sc = _dot_nt(q, kc_ref[0])
    cend = lax.broadcasted_iota(jnp.int32, (1, nc), 1) * CMP_STRIDE + (CMP_BLOCK - 1)
    sc = jnp.where(cend <= t4, sc, NEG_BIG)
    e = jnp.exp2(sc - jnp.max(sc, axis=-1, keepdims=True))
    p = e / jnp.sum(e, axis=-1, keepdims=True)
    p = jnp.where(t4 >= CMP_BLOCK - 1, p, 0.0)
    o_cmp = _dot(p.astype(BF16), vc_ref[0])
    psum = p[0:tq] + p[tq:2 * tq] + p[2 * tq:3 * tq] + p[3 * tq:4 * tq]
    p0, p1, p2 = _split3(psum)
    ov = ov_ref[...]
    imp = _dot(p0, ov) + _dot(p1, ov) + _dot(p2, ov)

    lane = lax.broadcasted_iota(jnp.int32, (1, LANES), 1)
    blk = lane % n_blk_lanes
    cur = t1 // SEL_BLOCK
    forced = (blk == 0) | (blk == cur) | (blk == cur - 1)
    score = jnp.where(blk * SEL_BLOCK > t1, -1.0, jnp.where(forced, FORCE_SCORE, imp))
    rank = jnp.zeros((tq, LANES), F32)
    for r in range(1, n_blk_lanes):
        other = pltpu.roll(score, r, 1)
        ahead = (other > score) | ((other == score) & (blk >= r))
        rank = rank + jnp.where(ahead, 1.0, 0.0)
    drop = jnp.where(rank < float(min(SEL_TOPN, n_blk_lanes)), 0.0, NEG_BIG).astype(BF16)
    drop4 = jnp.concatenate([drop] * H, axis=0)

    def slc_scores(j):
        erow = lax.broadcasted_iota(jnp.int32, (LANES, tk), 0)
        ecol = lax.broadcasted_iota(jnp.int32, (LANES, tk), 1)
        expand = jnp.where(erow == j * (tk // SEL_BLOCK) + ecol // SEL_BLOCK, 1.0, 0.0).astype(BF16)
        return _dot_nt(q, k_ref[0, 0, pl.ds(j * tk, tk), :]) + _dot(drop4, expand)

    init = (jnp.full((rows, 1), NEG_BIG, F32), jnp.zeros((rows, LANES), F32))

    def far_step(j, carry):
        return _softmax_step(slc_scores(j), v_ref[0, 0, pl.ds(j * tk, tk), :], *carry)

    def near_step(j, carry):
        dist = t4 - (j * tk + lax.broadcasted_iota(jnp.int32, (1, tk), 1))
        s_s = jnp.where(dist >= 0, slc_scores(j), NEG_BIG)
        s_w = _dot_nt(q, k_ref[0, 1, pl.ds(j * tk, tk), :])
        s_w = jnp.where((dist >= 0) & (dist < WINDOW), s_w, NEG_BIG)
        c_s = _softmax_step(s_s, v_ref[0, 0, pl.ds(j * tk, tk), :], *carry[0])
        c_w = _softmax_step(s_w, v_ref[0, 1, pl.ds(j * tk, tk), :], *carry[1])
        return c_s, c_w

    lo = jnp.maximum(i - WINDOW // tk, 0)
    c_s = lax.fori_loop(0, lo, far_step, init)
    (_, a_s), (_, a_w) = lax.fori_loop(lo, i + 1, near_step, (c_s, init))
    o_slc = _softmax_finish(a_s)
    o_win = _softmax_finish(a_w)

    g = g_ref[0]
    outs = []
    for hd in range(H):
        r = slice(hd * tq, (hd + 1) * tq)
        outs.append(g[:, hd:hd + 1] * o_cmp[r] + g[:, H + hd:H + hd + 1] * o_slc[r]
                    + g[:, 2 * H + hd:2 * H + hd + 1] * o_win[r])
    o_ref[0] = jnp.concatenate(outs, axis=1).astype(BF16)


def _nsa_attention(q, kcmp, vcmp, ksw, vsw, gates, overlap):
    B, H, S, _ = q.shape
    tq = TQ_NSA
    nc = kcmp.shape[1]
    return pl.pallas_call(
        _nsa_kernel,
        out_shape=jax.ShapeDtypeStruct((B, S, MIX_WIDTH), BF16),
        grid=(B, S // tq),
        in_specs=[pl.BlockSpec((1, H, tq, D_HEAD), lambda b, i: (b, 0, i, 0)),
                  pl.BlockSpec((1, nc, D_HEAD), lambda b, i: (b, 0, 0)),
                  pl.BlockSpec((1, nc, D_HEAD), lambda b, i: (b, 0, 0)),
                  pl.BlockSpec((1, 2, S, D_HEAD), lambda b, i: (b, 0, 0, 0)),
                  pl.BlockSpec((1, 2, S, LANES), lambda b, i: (b, 0, 0, 0)),
                  pl.BlockSpec((1, tq, LANES), lambda b, i: (b, i, 0)),
                  pl.BlockSpec((LANES, LANES), lambda b, i: (0, 0))],
        out_specs=pl.BlockSpec((1, tq, MIX_WIDTH), lambda b, i: (b, i, 0)),
        compiler_params=pltpu.CompilerParams(vmem_limit_bytes=VMEM_LIMIT),
        name="nsa_attention",
    )(q, kcmp, vcmp, ksw, vsw, gates, overlap)


def _attn_kernel(mode, lam_init, *refs):
    if mode == "fox":
        q_ref, k_ref, v_ref, kb_ref, o_ref = refs
    elif mode == "diff":
        q_ref, k_ref, v_ref, lam_ref, og_ref, o_ref = refs
    else:
        q_ref, k_ref, v_ref, o_ref = refs
    i = pl.program_id(1)
    t = TQ_ATT
    H = N_HEADS
    qs = []
    for hd in range(H):
        q = q_ref[0, hd]
        if mode == "diff":
            lane = lax.broadcasted_iota(jnp.int32, (1, D_HEAD), 1)
            zero = jnp.zeros_like(q)
            q = jnp.concatenate([jnp.where(lane < DIFF_QK, q, zero), jnp.where(lane >= DIFF_QK, q, zero)], axis=0)
        qs.append(q)
    rows = qs[0].shape[0]

    def step(j, carry, mask=None):
        out = []
        for hd in range(H):
            s = _dot_nt(qs[hd], k_ref[0, hd, pl.ds(j * t, t), :])
            if mode == "fox":
                s = s + kb_ref[0, hd:hd + 1, pl.ds(j * t, t)]
            if mask is not None:
                s = jnp.where(mask, s, NEG_BIG)
            out.append(_softmax_step(s, v_ref[0, hd, pl.ds(j * t, t), :], *carry[hd]))
        return tuple(out)

    init = tuple((jnp.full((rows, 1), NEG_BIG, F32), jnp.zeros((rows, LANES), F32)) for _ in range(H))
    carry = lax.fori_loop(0, i, step, init)
    rr = lax.broadcasted_iota(jnp.int32, (rows, 1), 0) % t
    cc = lax.broadcasted_iota(jnp.int32, (1, t), 1)
    carry = step(i, carry, mask=cc <= rr)

    outs = []
    for hd in range(H):
        o = _softmax_finish(carry[hd][1])
        if mode == "diff":
            lv = lam_ref[0]
            lam = (jnp.exp(jnp.sum(lv[0:1] * lv[1:2], axis=-1, keepdims=True))
                   - jnp.exp(jnp.sum(lv[2:3] * lv[3:4], axis=-1, keepdims=True)) + lam_init)
            o = o[0:t] - lam * o[t:2 * t]
            o = _rms_rows(o) * og_ref[0] * (1.0 - lam_init)
        outs.append(o)
    o_ref[0] = jnp.concatenate(outs, axis=1).astype(BF16)


def _dense_attention(mode, l, lam_init, q, k, v, *extra):
    B, H, S, dk = q.shape
    t = TQ_ATT
    in_specs = [pl.BlockSpec((1, H, t, dk), lambda b, i: (b, 0, i, 0)),
                pl.BlockSpec((1, H, S, dk), lambda b, i: (b, 0, 0, 0)),
                pl.BlockSpec((1, H, S, LANES), lambda b, i: (b, 0, 0, 0))]
    if mode == "fox":
        in_specs += [pl.BlockSpec((1, 8, S), lambda b, i: (b, 0, 0))]
    elif mode == "diff":
        in_specs += [pl.BlockSpec((1, 4, DIFF_QK), lambda b, i: (l, 0, 0)),
                     pl.BlockSpec((1, 1, D_HEAD), lambda b, i: (l, 0, 0))]
    return pl.pallas_call(
        functools.partial(_attn_kernel, mode, lam_init),
        out_shape=jax.ShapeDtypeStruct((B, S, MIX_WIDTH), BF16),
        grid=(B, S // t),
        in_specs=in_specs,
        out_specs=pl.BlockSpec((1, t, MIX_WIDTH), lambda b, i: (b, i, 0)),
        compiler_params=pltpu.CompilerParams(vmem_limit_bytes=VMEM_LIMIT),
        name=mode + "_attention",
    )(q, k, v, *extra)


def _merge_kernel(x_ref, h_ref, o0_ref, o1_ref, o2_ref, o3_ref, mod_ref, brw_ref, gw_ref, gb_ref, wo_ref,
                  out_ref):
    hb = h_ref[0]
    merged = None
    for m, o_ref in enumerate((o0_ref, o1_ref, o2_ref, o3_ref)):
        y = _dot(o_ref[0], brw_ref[0, m])
        cols = slice(m * D_MODEL, (m + 1) * D_MODEL)
        gate = _sigmoid(_dot(hb, gw_ref[0, :, cols]) + gb_ref[0, :, cols])
        merged = gate * y if merged is None else merged + gate * y
    out = _dot(merged.astype(BF16), wo_ref[0])
    out_ref[0] = x_ref[0] + mod_ref[0, 0, 2:3, :] * out


def _merge(l, x, hbf, o_nsa, o_fox, o_mla, o_diff, mod, brw, gw, gb, wo):
    B, S, D = x.shape
    ts = TS_DENSE
    row = lambda d: pl.BlockSpec((1, ts, d), lambda b, i: (b, i, 0))
    return pl.pallas_call(
        _merge_kernel,
        out_shape=jax.ShapeDtypeStruct((B, S, D), F32),
        grid=(B, S // ts),
        in_specs=[row(D), row(D), row(MIX_WIDTH), row(MIX_WIDTH), row(MIX_WIDTH), row(MIX_WIDTH),
                  pl.BlockSpec((1, 1, 6, D), lambda b, i: (l, b, 0, 0)),
                  pl.BlockSpec((1, 4, MIX_WIDTH, D), lambda b, i: (l, 0, 0, 0)),
                  pl.BlockSpec((1, D, 4 * D), lambda b, i: (l, 0, 0)),
                  pl.BlockSpec((1, 1, 4 * D), lambda b, i: (l, 0, 0)),
                  pl.BlockSpec((1, D, D), lambda b, i: (l, 0, 0))],
        out_specs=row(D),
        compiler_params=pltpu.CompilerParams(vmem_limit_bytes=VMEM_LIMIT),
        name="merge_out",
    )(x, hbf, o_nsa, o_fox, o_mla, o_diff, mod, brw, gw, gb, wo)


def _ffn_kernel(x_ref, mod_ref, wup_ref, cw_ref, cb_ref, wd_ref, out_ref, carry_ref):
    i = pl.program_id(1)
    x = x_ref[0]
    ts = x.shape[0]
    sh2 = mod_ref[0, 0, 3:4, :]
    sc2 = mod_ref[0, 0, 4:5, :]
    hb = (_rms_rows(x) * (1.0 + sc2) + sh2).astype(BF16)

    @pl.when(i == 0)
    def _():
        carry_ref[...] = jnp.zeros_like(carry_ref)

    row = lax.broadcasted_iota(jnp.int32, (ts, 1), 0)
    acc = None
    for c in range(D_FF // FF_CHUNK):
        cols = slice(c * FF_CHUNK, (c + 1) * FF_CHUNK)
        g = _dot(hb, wup_ref[0, :, cols])
        v = _dot(hb, wup_ref[0, :, D_FF + c * FF_CHUNK:D_FF + (c + 1) * FF_CHUNK])
        prev = carry_ref[c]
        g1 = jnp.where(row == 0, prev[7:8, :], pltpu.roll(g, 1, 0))
        g2 = jnp.where(row == 0, prev[6:7, :], jnp.where(row == 1, prev[7:8, :], pltpu.roll(g, 2, 0)))
        carry_ref[c] = g[ts - 8:ts, :]
        conv = (cw_ref[0, 0:1, cols] * g2 + cw_ref[0, 1:2, cols] * g1 + cw_ref[0, 2:3, cols] * g
                + cb_ref[0, :, cols])
        a = (_silu(conv) * v).astype(BF16)
        part = _dot(a, wd_ref[0, cols, :])
        acc = part if acc is None else acc + part
    out_ref[0] = x + mod_ref[0, 0, 5:6, :] * acc


def _ffn(l, x, mod, wup, cw, cb, wd):
    B, S, D = x.shape
    ts = TS_DENSE
    const = pl.Buffered(1)
    return pl.pallas_call(
        _ffn_kernel,
        out_shape=jax.ShapeDtypeStruct((B, S, D), F32),
        grid=(B, S // ts),
        in_specs=[pl.BlockSpec((1, ts, D), lambda b, i: (b, i, 0)),
                  pl.BlockSpec((1, 1, 6, D), lambda b, i: (l, b, 0, 0)),
                  pl.BlockSpec((1, D, 2 * D_FF), lambda b, i: (l, 0, 0), pipeline_mode=const),
                  pl.BlockSpec((1, 3, D_FF), lambda b, i: (l, 0, 0)),
                  pl.BlockSpec((1, 1, D_FF), lambda b, i: (l, 0, 0)),
                  pl.BlockSpec((1, D_FF, D), lambda b, i: (l, 0, 0), pipeline_mode=const)],
        out_specs=pl.BlockSpec((1, ts, D), lambda b, i: (b, i, 0)),
        scratch_shapes=[pltpu.VMEM((D_FF // FF_CHUNK, 8, FF_CHUNK), F32)],
        compiler_params=pltpu.CompilerParams(dimension_semantics=("arbitrary", "arbitrary"),
                                             vmem_limit_bytes=VMEM_LIMIT),
        name="conv_ffn",
    )(x, mod, wup, cw, cb, wd)


def _pad_cols(w, n):
    return jnp.pad(w, [(0, 0)] * (w.ndim - 1) + [(0, n - w.shape[-1])])


def _layout_w_in(w_in):
    o = 0
    seg = {}
    for name, n in (("nq", 256), ("nkc", 64), ("nvc", 64), ("nks", 64), ("nvs", 64), ("nkw", 64), ("nvw", 64),
                    ("ng", 12), ("fq", 256), ("fk", 256), ("fv", 256), ("ff", 4),
                    ("mcq", 256), ("mckv", 128), ("mkr", 32), ("dq", 256), ("dk", 256), ("dv", 256)):
        seg[name] = w_in[..., o:o + n]
        o += n
    parts = [seg["nq"], seg["nks"], seg["nkw"], seg["nvs"], seg["nvw"], seg["nkc"], seg["nvc"],
             _pad_cols(seg["ng"], LANES),
             seg["fq"], seg["fk"], seg["fv"], _pad_cols(seg["ff"], LANES),
             seg["mcq"], seg["mckv"], _pad_cols(seg["mkr"], LANES),
             seg["dq"], seg["dk"], seg["dv"]]
    return jnp.concatenate(parts, axis=-1).astype(BF16)


def _pack_vecs(nsa_qk_g, fox_qk_g, fox_f_b, mla_cq_g, mla_ckv_g, mla_qk_g, diff_qk_g):
    L = nsa_qk_g.shape[0]
    t2 = lambda g: jnp.concatenate([g, g], axis=-1)
    mla_pad = lambda g: _pad_cols(g, LANES)
    rows = [None] * N_VEC
    rows[V_NQ] = t2(nsa_qk_g[:, 0]) * (LOG2E * D_HEAD ** -0.5)
    rows[V_NK] = jnp.concatenate([nsa_qk_g[:, 2], nsa_qk_g[:, 3]], axis=-1)
    rows[V_FQ] = t2(fox_qk_g[:, 0]) * (LOG2E * D_HEAD ** -0.5)
    rows[V_FK] = t2(fox_qk_g[:, 1])
    rows[V_MQ] = mla_pad(mla_qk_g[:, 0]) * (LOG2E * MLA_QK ** -0.5)
    rows[V_MK] = mla_pad(mla_qk_g[:, 1])
    rows[V_DQ] = jnp.tile(diff_qk_g[:, 0], (1, 4)) * (LOG2E * DIFF_QK ** -0.5)
    rows[V_DK] = jnp.tile(diff_qk_g[:, 1], (1, 4))
    rows[V_FB] = fox_f_b
    rows[V_CQG] = mla_cq_g
    rows[V_CKVG] = mla_ckv_g
    rows = [jnp.zeros((L, 256), F32) if r is None else _pad_cols(r.astype(F32), 256) for r in rows]
    return jnp.stack(rows, axis=1)


def _block_diag_ones():
    d = jnp.arange(LANES)
    mats = [(d[:, None] // g == d[None, :] // g) for g in (D_HEAD, LANES, DIFF_QK)]
    return jnp.stack(mats).astype(BF16)


def _overlap_matrix(n_cmp_rows, n_blk):
    c = jnp.arange(n_cmp_rows)[:, None] * CMP_STRIDE
    b = (jnp.arange(LANES)[None, :] % n_blk) * SEL_BLOCK
    ov = (c < b + SEL_BLOCK) & (c + CMP_BLOCK > b)
    return ov.astype(BF16)


def kernel(x, c, positions, ada_w, ada_b, w_in, nsa_qk_g, nsa_cmp_pe, nsa_cmp_w1, nsa_cmp_w2, fox_qk_g, fox_f_b,
           mla_cq_g, mla_ckv_g, mla_w_uq, mla_w_ukv, mla_qk_g, diff_qk_g, diff_lambda, diff_out_g, br_w, gate_w,
           gate_b, w_out, ffn_w_up, ffn_conv_w, ffn_conv_b, ffn_w_down):
    B, S, D = x.shape
    L = ada_w.shape[0]
    H = N_HEADS
    n_half = S // CMP_STRIDE

    mod = _modulation(c, ada_w, ada_b).reshape(L, B, 6, D)
    tab = _rope_tables(positions)

    w1 = _layout_w_in(w_in)
    vecs = _pack_vecs(nsa_qk_g, fox_qk_g, fox_f_b, mla_cq_g, mla_ckv_g, mla_qk_g, diff_qk_g)
    bd = _block_diag_ones()
    tri = (jnp.arange(TS_PROJ)[None, :] <= jnp.arange(TS_PROJ)[:, None]).astype(BF16)
    overlap = _overlap_matrix(n_half, S // SEL_BLOCK)
    wuq = _pad_cols(mla_w_uq.reshape(L, MLA_Q_RANK, H, MLA_QK), LANES).reshape(L, MLA_Q_RANK, H * LANES)
    ukv = mla_w_ukv.reshape(L, MLA_KV_RANK, H, MLA_NOPE + D_HEAD)
    wuk = jnp.pad(ukv[..., :MLA_NOPE], [(0, 0), (0, 0), (0, 0), (MLA_ROPE, LANES - MLA_QK)])
    wuk = wuk.reshape(L, MLA_KV_RANK, H * LANES).astype(BF16)
    wuv = ukv[..., MLA_NOPE:].reshape(L, MLA_KV_RANK, H * D_HEAD).astype(BF16)
    wuq = wuq.astype(BF16)
    pe = nsa_cmp_pe.reshape(L, 2, CMP_BLOCK * D_HEAD)
    cw1 = nsa_cmp_w1.astype(BF16)
    cw2 = nsa_cmp_w2.astype(BF16)
    kcg = _pad_cols(nsa_qk_g[:, 1], LANES).reshape(L, 1, LANES)
    brw = br_w.astype(BF16)
    gw = gate_w.astype(BF16)
    gb = gate_b.reshape(L, 1, 4 * D)
    wo = w_out.astype(BF16)
    wup = ffn_w_up.astype(BF16)
    wd = ffn_w_down.astype(BF16)
    cb = ffn_conv_b.reshape(L, 1, D_FF)
    og = diff_out_g.reshape(L, 1, D_HEAD)

    for l in range(L):
        lam_init = 0.8 - 0.6 * math.exp(-0.3 * l)
        (hbf, nq, nk, nv, nkc, nvc, ng, fq, fk, fv, fcr, mq, mk, mv, dq, dk, dv) = _projection(
            l, x, mod, tab, w1, vecs, bd, tri, wuq, wuk, wuv)
        kcmp, vcmp = _compress(l, nkc.reshape(B, n_half, CMP_STRIDE * D_HEAD),
                               nvc.reshape(B, n_half, CMP_STRIDE * D_HEAD), pe, cw1, cw2, kcg)
        o_nsa = _nsa_attention(nq, kcmp, vcmp, nk, nv, ng, overlap)
        o_fox = _dense_attention("fox", l, lam_init, fq, fk, fv, fcr)
        o_mla = _dense_attention("mla", l, lam_init, mq, mk, mv)
        o_diff = _dense_attention("diff", l, lam_init, dq, dk, dv, diff_lambda, og)
        x = _merge(l, x, hbf, o_nsa, o_fox, o_mla, o_diff, mod, brw, gw, gb, wo)
        x = _ffn(l, x, mod, wup, ffn_conv_w, cb, wd)
    return x
```

```python
import functools
import math

import jax
import jax.numpy as jnp
from jax import lax
from jax.experimental import pallas as pl
from jax.experimental.pallas import tpu as pltpu

F32 = jnp.float32
BF16 = jnp.bfloat16

D_MODEL = 1024
D_HEAD = 64
N_HEADS = 4
MIX_WIDTH = 256
NSA_ROT = 16
CMP_BLOCK = 32
CMP_STRIDE = 16
SEL_BLOCK = 64
SEL_TOPN = 16
WINDOW = 512
MLA_Q_RANK = 256
MLA_KV_RANK = 128
MLA_NOPE = 64
MLA_ROPE = 32
MLA_QK = MLA_ROPE + MLA_NOPE
DIFF_QK = 32
DIFF_ROT = 8
D_FF = 2816
ROPE_THETA = 500000.0
EPS = 1e-6
FORCE_SCORE = 1e4
NEG_BIG = -1e30

LANES = 128
VMEM_LIMIT = 56 * 1024 * 1024

C_NQ, C_NKSW, C_NVSW, C_NKVC, C_NG = 0, 256, 384, 512, 640
C_FQ, C_FK, C_FV, C_FF = 768, 1024, 1280, 1536
C_MCQ, C_MCKV, C_MKR = 1664, 1920, 2048
C_DQ, C_DK, C_DV = 2176, 2432, 2688
W_IN_PAD = 2944

V_NQ, V_NK, V_FQ, V_FK, V_MQ, V_MK, V_DQ, V_DK, V_FB, V_CQG, V_CKVG = range(11)
N_VEC = 16

TS_PROJ = 512
TS_DENSE = 512
TQ_ATT = 512
TQ_NSA = 256
TK_NSA = 256
FF_CHUNK = 1408
LOG2E = math.log2(math.e)


def _dot(a, b):
    return jnp.dot(a, b, preferred_element_type=F32)


def _dot_nt(a, b):
    return lax.dot_general(a, b, (((1,), (1,)), ((), ())), preferred_element_type=F32)


def _split2(x):
    hi = x.astype(BF16)
    lo = (x - hi.astype(F32)).astype(BF16)
    return hi, lo


def _split3(x):
    hi = x.astype(BF16)
    r = x - hi.astype(F32)
    mid = r.astype(BF16)
    lo = (r - mid.astype(F32)).astype(BF16)
    return hi, mid, lo


def _rms_rows(x):
    return x * lax.rsqrt(jnp.mean(x * x, axis=-1, keepdims=True) + EPS)


def _group_rms(x, bd, inv_n):
    ss = _dot((x * x).astype(BF16), bd)
    return x * lax.rsqrt(ss * inv_n + EPS)


def _rope(xc, tab_ref, cfg, half):
    c = tab_ref[3 * cfg, 0]
    sa = tab_ref[3 * cfg + 1, 0]
    sb = tab_ref[3 * cfg + 2, 0]
    return xc * c + pltpu.roll(xc, LANES - half, 1) * sa + pltpu.roll(xc, half, 1) * sb


def _log_sigmoid(x):
    return jnp.minimum(x, 0.0) - jnp.log1p(jnp.exp(-jnp.abs(x)))


def _sigmoid(x):
    return 1.0 / (1.0 + jnp.exp(-x))


def _silu(x):
    return x * _sigmoid(x)


def _softmax_step(s, v1, m, acc):
    m_new = jnp.maximum(m, jnp.max(s, axis=-1, keepdims=True))
    p = jnp.exp2(s - m_new).astype(BF16)
    return m_new, jnp.exp2(m - m_new) * acc + _dot(p, v1)


def _softmax_finish(acc):
    return acc[:, :D_HEAD] / acc[:, D_HEAD:D_HEAD + 1]


def _mod_kernel(c_ref, w_ref, b_ref, o_ref):
    c = c_ref[...]
    a = _silu(c).astype(BF16)
    o_ref[0] = _dot(a, w_ref[0].astype(BF16)) + b_ref[0]


def _modulation(c, ada_w, ada_b):
    L, D, N = ada_w.shape
    B = c.shape[0]
    tn = 1536
    return pl.pallas_call(
        _mod_kernel,
        out_shape=jax.ShapeDtypeStruct((L, B, N), F32),
        grid=(L, N // tn),
        in_specs=[pl.BlockSpec((B, D), lambda l, n: (0, 0)),
                  pl.BlockSpec((1, D, tn), lambda l, n: (l, 0, n)),
                  pl.BlockSpec((1, 1, tn), lambda l, n: (l, 0, n))],
        out_specs=pl.BlockSpec((1, B, tn), lambda l, n: (l, 0, n)),
        compiler_params=pltpu.CompilerParams(vmem_limit_bytes=VMEM_LIMIT),
        name="adaln_mod",
    )(c, ada_w, ada_b.reshape(L, 1, N))


def _rope_tab_kernel(pos_ref, cst_ref, o_ref):
    pos = pos_ref[0].astype(F32)
    for cfg in range(3):
        ang = pos * cst_ref[cfg:cfg + 1, :]
        cs = jnp.cos(ang)
        sn = jnp.sin(ang)
        o_ref[3 * cfg, 0] = cs
        o_ref[3 * cfg + 1, 0] = sn * cst_ref[3 + cfg:4 + cfg, :]
        o_ref[3 * cfg + 2, 0] = sn * cst_ref[6 + cfg:7 + cfg, :]


def _rope_consts():
    rows_f, rows_a, rows_b = [], [], []
    for group, n_rot in ((D_HEAD, NSA_ROT), (LANES, MLA_ROPE), (DIFF_QK, DIFF_ROT)):
        half = n_rot // 2
        inv_freq = ROPE_THETA ** (-jnp.arange(half, dtype=F32) / half)
        d = jnp.arange(LANES) % group
        f = jnp.where(d < n_rot, inv_freq[d % half], 0.0)
        rows_f.append(f)
        rows_a.append(jnp.where(d < half, -1.0, 0.0))
        rows_b.append(jnp.where((d >= half) & (d < n_rot), 1.0, 0.0))
    return jnp.stack(rows_f + rows_a + rows_b).astype(F32)


def _rope_tables(positions):
    B, S = positions.shape
    ts = 512
    return pl.pallas_call(
        _rope_tab_kernel,
        out_shape=jax.ShapeDtypeStruct((9, B, S, LANES), F32),
        grid=(B, S // ts),
        in_specs=[pl.BlockSpec((1, ts, 1), lambda b, i: (b, i, 0)),
                  pl.BlockSpec((9, LANES), lambda b, i: (0, 0))],
        out_specs=pl.BlockSpec((9, 1, ts, LANES), lambda b, i: (0, b, i, 0)),
        name="rope_tables",
    )(positions.reshape(B, S, 1), _rope_consts())


def _proj_kernel(x_ref, mod_ref, tab_ref, w_ref, vec_ref, bd_ref, tri_ref, wuq_ref, wuk_ref, wuv_ref,
                 hbf_ref, nq_ref, nk_ref, nv_ref, nkc_ref, nvc_ref, ng_ref,
                 fq_ref, fk_ref, fv_ref, fcr_ref,
                 mq_ref, mk_ref, mv_ref, dq_ref, dk_ref, dv_ref, carry_ref):
    i = pl.program_id(1)
    x = x_ref[0]
    sh1 = mod_ref[0, 0, 0:1, :]
    sc1 = mod_ref[0, 0, 1:2, :]
    hb = (_rms_rows(x) * (1.0 + sc1) + sh1).astype(BF16)
    hbf_ref[0] = hb

    def proj(c0, n):
        return _dot(hb, w_ref[0, :, c0:c0 + n])

    def vec(r, n=LANES):
        return vec_ref[0, r:r + 1, 0:n]

    bd64, bd128, bd32 = bd_ref[0], bd_ref[1], bd_ref[2]
    row = lax.broadcasted_iota(jnp.int32, (x.shape[0], 1), 0)
    lane = lax.broadcasted_iota(jnp.int32, (1, LANES), 1)

    def head_pair(seg, gain_row, bd, inv_n, cfg, half, out_ref):
        yn = _group_rms(seg, bd, inv_n) * vec(gain_row, 256)
        for c in range(2):
            y = yn[:, c * LANES:(c + 1) * LANES]
            if cfg is not None:
                y = _rope(y, tab_ref, cfg, half)
            yb = y.astype(BF16)
            out_ref[0, 2 * c] = yb[:, :D_HEAD]
            out_ref[0, 2 * c + 1] = yb[:, D_HEAD:]

    def value_store(seg, out_ref):
        ones_col = jnp.where(lane == D_HEAD, 1.0, 0.0)
        for c in range(seg.shape[1] // LANES):
            ch = seg[:, c * LANES:(c + 1) * LANES]
            out_ref[0, 2 * c] = jnp.where(lane < D_HEAD, ch, ones_col).astype(BF16)
            out_ref[0, 2 * c + 1] = jnp.where(lane < D_HEAD, pltpu.roll(ch, D_HEAD, 1), ones_col).astype(BF16)

    head_pair(proj(C_NQ, 256), V_NQ, bd64, 1.0 / D_HEAD, 0, NSA_ROT // 2, nq_ref)
    ksw = _group_rms(proj(C_NKSW, LANES), bd64[0:LANES, 0:LANES], 1.0 / D_HEAD) * vec(V_NK)
    ksw = _rope(ksw, tab_ref, 0, NSA_ROT // 2)
    blk_hot = jnp.where(lane - D_HEAD == (i * x.shape[0] + row) // SEL_BLOCK, 1.0, 0.0)
    nk_ref[0, 0] = jnp.where(lane < D_HEAD, ksw, blk_hot).astype(BF16)
    nk_ref[0, 1] = jnp.where(lane < D_HEAD, pltpu.roll(ksw, D_HEAD, 1), 0.0).astype(BF16)
    value_store(proj(C_NVSW, LANES), nv_ref)
    kvc = proj(C_NKVC, LANES)
    nkc_ref[0] = kvc[:, :D_HEAD]
    nvc_ref[0] = kvc[:, D_HEAD:]
    ng_ref[0] = _sigmoid(proj(C_NG, LANES))

    head_pair(proj(C_FQ, 256), V_FQ, bd64, 1.0 / D_HEAD, None, 0, fq_ref)
    head_pair(proj(C_FK, 256), V_FK, bd64, 1.0 / D_HEAD, None, 0, fk_ref)
    value_store(proj(C_FV, 256), fv_ref)

    @pl.when(i == 0)
    def _():
        carry_ref[...] = jnp.zeros_like(carry_ref)

    lf = _log_sigmoid(proj(C_FF, LANES) + vec(V_FB))
    tri = tri_ref[...]
    p0, p1, p2 = _split3(lf)
    cum = _dot(tri, p0) + _dot(tri, p1) + _dot(tri, p2) + carry_ref[0:1, :]
    ts = cum.shape[0]
    carry_ref[0:1, :] = cum[ts - 1:ts, :]
    fcr_ref[0] = (cum * (-LOG2E)).T[0:8, :]

    cq = proj(C_MCQ, MLA_Q_RANK)
    cqn = (_rms_rows(cq) * vec(V_CQG, MLA_Q_RANK)).astype(BF16)
    qm = _dot(cqn, wuq_ref[0])
    ckv = proj(C_MCKV, MLA_KV_RANK)
    ckvn = (_rms_rows(ckv) * vec(V_CKVG)).astype(BF16)
    kk = _dot(ckvn, wuk_ref[0])
    vv = _dot(ckvn, wuv_ref[0])
    kr = proj(C_MKR, LANES)
    kr2 = jnp.concatenate([kr, kr], axis=1)
    for pair in range(N_HEADS // 2):
        sl = slice(pair * 256, (pair + 1) * 256)
        yq = _group_rms(qm[:, sl], bd128, 1.0 / MLA_QK) * vec(V_MQ, 256)
        yk = _group_rms(kk[:, sl] + kr2, bd128, 1.0 / MLA_QK) * vec(V_MK, 256)
        for c in range(2):
            cs = slice(c * LANES, (c + 1) * LANES)
            mq_ref[0, 2 * pair + c] = _rope(yq[:, cs], tab_ref, 1, MLA_ROPE // 2).astype(BF16)
            mk_ref[0, 2 * pair + c] = _rope(yk[:, cs], tab_ref, 1, MLA_ROPE // 2).astype(BF16)
    value_store(vv, mv_ref)

    head_pair(proj(C_DQ, 256), V_DQ, bd32, 1.0 / DIFF_QK, 2, DIFF_ROT // 2, dq_ref)
    head_pair(proj(C_DK, 256), V_DK, bd32, 1.0 / DIFF_QK, 2, DIFF_ROT // 2, dk_ref)
    value_store(proj(C_DV, 256), dv_ref)


def _projection(l, x, mod, tab, w1, vecs, bd, tri, wuq, wuk, wuv):
    B, S, D = x.shape
    ts = TS_PROJ
    H = N_HEADS
    hm = lambda d, dt: jax.ShapeDtypeStruct((B, H, S, d), dt)
    hm_spec = lambda nh, d: pl.BlockSpec((1, nh, ts, d), lambda b, i: (b, 0, i, 0))
    row_spec = lambda d: pl.BlockSpec((1, ts, d), lambda b, i: (b, i, 0))
    out_shape = [
        jax.ShapeDtypeStruct((B, S, D), BF16),
        hm(D_HEAD, BF16),
        jax.ShapeDtypeStruct((B, 2, S, LANES), BF16),
        jax.ShapeDtypeStruct((B, 2, S, LANES), BF16),
        jax.ShapeDtypeStruct((B, S, D_HEAD), F32),
        jax.ShapeDtypeStruct((B, S, D_HEAD), F32),
        jax.ShapeDtypeStruct((B, S, LANES), F32),
        hm(D_HEAD, BF16), hm(D_HEAD, BF16), hm(LANES, BF16),
        jax.ShapeDtypeStruct((B, 8, S), F32),
        hm(LANES, BF16), hm(LANES, BF16), hm(LANES, BF16),
        hm(D_HEAD, BF16), hm(D_HEAD, BF16), hm(LANES, BF16),
    ]
    out_specs = [
        row_spec(D), hm_spec(H, D_HEAD), hm_spec(2, LANES), hm_spec(2, LANES),
        row_spec(D_HEAD), row_spec(D_HEAD), row_spec(LANES),
        hm_spec(H, D_HEAD), hm_spec(H, D_HEAD), hm_spec(H, LANES),
        pl.BlockSpec((1, 8, ts), lambda b, i: (b, 0, i)),
        hm_spec(H, LANES), hm_spec(H, LANES), hm_spec(H, LANES),
        hm_spec(H, D_HEAD), hm_spec(H, D_HEAD), hm_spec(H, LANES),
    ]
    in_specs = [
        row_spec(D),
        pl.BlockSpec((1, 1, 6, D), lambda b, i: (l, b, 0, 0)),
        pl.BlockSpec((9, 1, ts, LANES), lambda b, i: (0, b, i, 0)),
        pl.BlockSpec((1, D, W_IN_PAD), lambda b, i: (l, 0, 0)),
        pl.BlockSpec((1, N_VEC, 256), lambda b, i: (l, 0, 0)),
        pl.BlockSpec((3, 256, 256), lambda b, i: (0, 0, 0)),
        pl.BlockSpec((ts, ts), lambda b, i: (0, 0)),
        pl.BlockSpec((1, MLA_Q_RANK, 512), lambda b, i: (l, 0, 0)),
        pl.BlockSpec((1, MLA_KV_RANK, 512), lambda b, i: (l, 0, 0)),
        pl.BlockSpec((1, MLA_KV_RANK, 256), lambda b, i: (l, 0, 0)),
    ]
    return pl.pallas_call(
        _proj_kernel,
        out_shape=out_shape,
        grid=(B, S // ts),
        in_specs=in_specs,
        out_specs=out_specs,
        scratch_shapes=[pltpu.VMEM((8, LANES), F32)],
        compiler_params=pltpu.CompilerParams(dimension_semantics=("arbitrary", "arbitrary"),
                                             vmem_limit_bytes=VMEM_LIMIT),
        name="in_proj",
    )(x, mod, tab, w1, vecs, bd, tri, wuq, wuk, wuv)


def _cmp_kernel(kc_ref, vc_ref, pe_ref, w1_ref, w2_ref, g_ref, ko_ref, vo_ref):
    half = CMP_STRIDE * D_HEAD
    n = kc_ref.shape[1]

    def compress(x2, j):
        xa = _dot((x2 + pe_ref[0, j:j + 1, 0:half]).astype(BF16), w1_ref[0, j, 0:half, :])
        xb = _dot((x2 + pe_ref[0, j:j + 1, half:2 * half]).astype(BF16), w1_ref[0, j, half:2 * half, :])
        hid = _silu(xa + pltpu.roll(xb, n - 1, 0))
        return _dot(hid.astype(BF16), w2_ref[0, j])

    kcmp = _rms_rows(compress(kc_ref[0], 0)) * g_ref[0, 0:1, 0:D_HEAD]
    ko_ref[0] = kcmp.astype(BF16)
    vo_ref[0] = compress(vc_ref[0], 1).astype(BF16)


def _compress(l, kc2, vc2, pe, w1, w2, g):
    B, n, wide = kc2.shape
    return pl.pallas_call(
        _cmp_kernel,
        out_shape=[jax.ShapeDtypeStruct((B, n, D_HEAD), BF16)] * 2,
        grid=(B,),
        in_specs=[pl.BlockSpec((1, n, wide), lambda b: (b, 0, 0)),
                  pl.BlockSpec((1, n, wide), lambda b: (b, 0, 0)),
                  pl.BlockSpec((1, 2, 2 * wide), lambda b: (l, 0, 0)),
                  pl.BlockSpec((1, 2, 2 * wide, D_HEAD), lambda b: (l, 0, 0, 0)),
                  pl.BlockSpec((1, 2, D_HEAD, D_HEAD), lambda b: (l, 0, 0, 0)),
                  pl.BlockSpec((1, 1, LANES), lambda b: (l, 0, 0))],
        out_specs=[pl.BlockSpec((1, n, D_HEAD), lambda b: (b, 0, 0))] * 2,
        name="nsa_compress",
    )(kc2, vc2, pe, w1, w2, g)


def _nsa_kernel(q_ref, kc_ref, vc_ref, k_ref, v_ref, g_ref, ov_ref, o_ref):
    i = pl.program_id(1)
    tq, tk = TQ_NSA, TK_NSA
    H = N_HEADS
    rows = H * tq
    n_blk = ov_ref.shape[0]
    q = q_ref[0].reshape(rows, D_HEAD)
    t4 = i * tq + lax.broadcasted_iota(jnp.int32, (rows, 1), 0) % tq

    nc = kc_ref.shape[1]
    sc = _dot_nt(q, kc_ref[0])
    cend = lax.broadcasted_iota(jnp.int32, (1, nc), 1) * CMP_STRIDE + (CMP_BLOCK - 1)
    sc = jnp.where(cend <= t4, sc, NEG_BIG)
    e = jnp.exp2(sc - jnp.max(sc, axis=-1, keepdims=True))
    p = e / jnp.sum(e, axis=-1, keepdims=True)
    p = jnp.where(t4 >= CMP_BLOCK - 1, p, 0.0)
    o_cmp = _dot(p.astype(BF16), vc_ref[0])
    psum = p[0:tq] + p[tq:2 * tq] + p[2 * tq:3 * tq] + p[3 * tq:4 * tq]
    p0, p1, p2 = _split3(psum)
    ovt = ov_ref[...]
    imp = _dot_nt(ovt, p0) + _dot_nt(ovt, p1) + _dot_nt(ovt, p2)

    blk = lax.broadcasted_iota(jnp.int32, (n_blk, 1), 0)
    tl = i * tq + lax.broadcasted_iota(jnp.int32, (1, tq), 1)
    cur = tl // SEL_BLOCK
    forced = (blk == 0) | (blk == cur) | (blk == cur - 1)
    score = jnp.where(blk * SEL_BLOCK > tl, -1.0, jnp.where(forced, FORCE_SCORE, imp))
    rank = jnp.zeros((n_blk, tq), F32)
    for r in range(n_blk):
        other = score[r:r + 1, :]
        ahead = (other > score) | ((other == score) & (blk > r))
        rank = rank + jnp.where(ahead, 1.0, 0.0)
    drop_t = jnp.where(rank < float(min(SEL_TOPN, n_blk)), 0.0, NEG_BIG)
    drop = jnp.concatenate([drop_t, jnp.zeros((LANES - n_blk, tq), F32)], axis=0).T.astype(BF16)
    qx = jnp.concatenate([q, jnp.concatenate([drop[:, 0:D_HEAD]] * H, axis=0)], axis=1)

    def slc_scores(j):
        return _dot_nt(qx, k_ref[0, 0, pl.ds(j * tk, tk), :])

    init = (jnp.full((rows, 1), NEG_BIG, F32), jnp.zeros((rows, LANES), F32))

    def far_step(j, carry):
        return _softmax_step(slc_scores(j), v_ref[0, 0, pl.ds(j * tk, tk), :], *carry)

    def near_step(j, carry):
        dist = t4 - (j * tk + lax.broadcasted_iota(jnp.int32, (1, tk), 1))
        s_s = jnp.where(dist >= 0, slc_scores(j), NEG_BIG)
        s_w = _dot_nt(qx, k_ref[0, 1, pl.ds(j * tk, tk), :])
        s_w = jnp.where((dist >= 0) & (dist < WINDOW), s_w, NEG_BIG)
        c_s = _softmax_step(s_s, v_ref[0, 0, pl.ds(j * tk, tk), :], *carry[0])
        c_w = _softmax_step(s_w, v_ref[0, 1, pl.ds(j * tk, tk), :], *carry[1])
        return c_s, c_w

    lo = jnp.maximum(i - WINDOW // tk, 0)
    c_s = lax.fori_loop(0, lo, far_step, init)
    (_, a_s), (_, a_w) = lax.fori_loop(lo, i + 1, near_step, (c_s, init))
    o_slc = _softmax_finish(a_s)
    o_win = _softmax_finish(a_w)

    g = g_ref[0]
    outs = []
    for hd in range(H):
        r = slice(hd * tq, (hd + 1) * tq)
        outs.append(g[:, hd:hd + 1] * o_cmp[r] + g[:, H + hd:H + hd + 1] * o_slc[r]
                    + g[:, 2 * H + hd:2 * H + hd + 1] * o_win[r])
    o_ref[0] = jnp.concatenate(outs, axis=1).astype(BF16)


def _nsa_attention(q, kcmp, vcmp, ksw, vsw, gates, overlap):
    B, H, S, _ = q.shape
    tq = TQ_NSA
    nc = kcmp.shape[1]
    return pl.pallas_call(
        _nsa_kernel,
        out_shape=jax.ShapeDtypeStruct((B, S, MIX_WIDTH), BF16),
        grid=(B, S // tq),
        in_specs=[pl.BlockSpec((1, H, tq, D_HEAD), lambda b, i: (b, 0, i, 0)),
                  pl.BlockSpec((1, nc, D_HEAD), lambda b, i: (b, 0, 0)),
                  pl.BlockSpec((1, nc, D_HEAD), lambda b, i: (b, 0, 0)),
                  pl.BlockSpec((1, 2, S, LANES), lambda b, i: (b, 0, 0, 0)),
                  pl.BlockSpec((1, 2, S, LANES), lambda b, i: (b, 0, 0, 0)),
                  pl.BlockSpec((1, tq, LANES), lambda b, i: (b, i, 0)),
                  pl.BlockSpec(overlap.shape, lambda b, i: (0, 0))],
        out_specs=pl.BlockSpec((1, tq, MIX_WIDTH), lambda b, i: (b, i, 0)),
        compiler_params=pltpu.CompilerParams(vmem_limit_bytes=VMEM_LIMIT),
        name="nsa_attention",
    )(q, kcmp, vcmp, ksw, vsw, gates, overlap)


def _attn_kernel(mode, lam_init, *refs):
    if mode == "fox":
        q_ref, k_ref, v_ref, kb_ref, o_ref = refs
    elif mode == "diff":
        q_ref, k_ref, v_ref, lam_ref, og_ref, o_ref = refs
    else:
        q_ref, k_ref, v_ref, o_ref = refs
    i = pl.program_id(1)
    t = TQ_ATT
    H = N_HEADS
    qs = []
    for hd in range(H):
        q = q_ref[0, hd]
        if mode == "diff":
            lane = lax.broadcasted_iota(jnp.int32, (1, D_HEAD), 1)
            zero = jnp.zeros_like(q)
            q = jnp.concatenate([jnp.where(lane < DIFF_QK, q, zero), jnp.where(lane >= DIFF_QK, q, zero)], axis=0)
        qs.append(q)
    rows = qs[0].shape[0]

    def step(j, carry, mask=None):
        out = []
        for hd in range(H):
            s = _dot_nt(qs[hd], k_ref[0, hd, pl.ds(j * t, t), :])
            if mode == "fox":
                s = s + kb_ref[0, hd:hd + 1, pl.ds(j * t, t)]
            if mask is not None:
                s = jnp.where(mask, s, NEG_BIG)
            out.append(_softmax_step(s, v_ref[0, hd, pl.ds(j * t, t), :], *carry[hd]))
        return tuple(out)

    init = tuple((jnp.full((rows, 1), NEG_BIG, F32), jnp.zeros((rows, LANES), F32)) for _ in range(H))
    carry = lax.fori_loop(0, i, step, init)
    rr = lax.broadcasted_iota(jnp.int32, (rows, 1), 0) % t
    cc = lax.broadcasted_iota(jnp.int32, (1, t), 1)
    carry = step(i, carry, mask=cc <= rr)

    outs = []
    for hd in range(H):
        o = _softmax_finish(carry[hd][1])
        if mode == "diff":
            lv = lam_ref[0]
            lam = (jnp.exp(jnp.sum(lv[0:1] * lv[1:2], axis=-1, keepdims=True))
                   - jnp.exp(jnp.sum(lv[2:3] * lv[3:4], axis=-1, keepdims=True)) + lam_init)
            o = o[0:t] - lam * o[t:2 * t]
            o = _rms_rows(o) * og_ref[0] * (1.0 - lam_init)
        outs.append(o)
    o_ref[0] = jnp.concatenate(outs, axis=1).astype(BF16)


def _dense_attention(mode, l, lam_init, q, k, v, *extra):
    B, H, S, dk = q.shape
    t = TQ_ATT
    in_specs = [pl.BlockSpec((1, H, t, dk), lambda b, i: (b, 0, i, 0)),
                pl.BlockSpec((1, H, S, dk), lambda b, i: (b, 0, 0, 0)),
                pl.BlockSpec((1, H, S, LANES), lambda b, i: (b, 0, 0, 0))]
    if mode == "fox":
        in_specs += [pl.BlockSpec((1, 8, S), lambda b, i: (b, 0, 0))]
    elif mode == "diff":
        in_specs += [pl.BlockSpec((1, 4, DIFF_QK), lambda b, i: (l, 0, 0)),
                     pl.BlockSpec((1, 1, D_HEAD), lambda b, i: (l, 0, 0))]
    return pl.pallas_call(
        functools.partial(_attn_kernel, mode, lam_init),
        out_shape=jax.ShapeDtypeStruct((B, S, MIX_WIDTH), BF16),
        grid=(B, S // t),
        in_specs=in_specs,
        out_specs=pl.BlockSpec((1, t, MIX_WIDTH), lambda b, i: (b, i, 0)),
        compiler_params=pltpu.CompilerParams(vmem_limit_bytes=VMEM_LIMIT),
        name=mode + "_attention",
    )(q, k, v, *extra)


def _merge_kernel(x_ref, h_ref, o0_ref, o1_ref, o2_ref, o3_ref, mod_ref, brw_ref, gw_ref, gb_ref, wo_ref,
                  out_ref):
    hb = h_ref[0]
    merged = None
    for m, o_ref in enumerate((o0_ref, o1_ref, o2_ref, o3_ref)):
        y = _dot(o_ref[0], brw_ref[0, m])
        cols = slice(m * D_MODEL, (m + 1) * D_MODEL)
        gate = _sigmoid(_dot(hb, gw_ref[0, :, cols]) + gb_ref[0, :, cols])
        merged = gate * y if merged is None else merged + gate * y
    out = _dot(merged.astype(BF16), wo_ref[0])
    out_ref[0] = x_ref[0] + mod_ref[0, 0, 2:3, :] * out


def _merge(l, x, hbf, o_nsa, o_fox, o_mla, o_diff, mod, brw, gw, gb, wo):
    B, S, D = x.shape
    ts = TS_DENSE
    row = lambda d: pl.BlockSpec((1, ts, d), lambda b, i: (b, i, 0))
    return pl.pallas_call(
        _merge_kernel,
        out_shape=jax.ShapeDtypeStruct((B, S, D), F32),
        grid=(B, S // ts),
        in_specs=[row(D), row(D), row(MIX_WIDTH), row(MIX_WIDTH), row(MIX_WIDTH), row(MIX_WIDTH),
                  pl.BlockSpec((1, 1, 6, D), lambda b, i: (l, b, 0, 0)),
                  pl.BlockSpec((1, 4, MIX_WIDTH, D), lambda b, i: (l, 0, 0, 0)),
                  pl.BlockSpec((1, D, 4 * D), lambda b, i: (l, 0, 0)),
                  pl.BlockSpec((1, 1, 4 * D), lambda b, i: (l, 0, 0)),
                  pl.BlockSpec((1, D, D), lambda b, i: (l, 0, 0))],
        out_specs=row(D),
        compiler_params=pltpu.CompilerParams(vmem_limit_bytes=VMEM_LIMIT),
        name="merge_out",
    )(x, hbf, o_nsa, o_fox, o_mla, o_diff, mod, brw, gw, gb, wo)


def _ffn_kernel(x_ref, mod_ref, wup_ref, cw_ref, cb_ref, wd_ref, out_ref, carry_ref):
    i = pl.program_id(1)
    x = x_ref[0]
    ts = x.shape[0]
    sh2 = mod_ref[0, 0, 3:4, :]
    sc2 = mod_ref[0, 0, 4:5, :]
    hb = (_rms_rows(x) * (1.0 + sc2) + sh2).astype(BF16)

    @pl.when(i == 0)
    def _():
        carry_ref[...] = jnp.zeros_like(carry_ref)

    row = lax.broadcasted_iota(jnp.int32, (ts, 1), 0)
    acc = None
    for c in range(D_FF // FF_CHUNK):
        cols = slice(c * FF_CHUNK, (c + 1) * FF_CHUNK)
        g = _dot(hb, wup_ref[0, :, cols])
        v = _dot(hb, wup_ref[0, :, D_FF + c * FF_CHUNK:D_FF + (c + 1) * FF_CHUNK])
        prev = carry_ref[c]
        g1 = jnp.where(row == 0, prev[7:8, :], pltpu.roll(g, 1, 0))
        g2 = jnp.where(row == 0, prev[6:7, :], jnp.where(row == 1, prev[7:8, :], pltpu.roll(g, 2, 0)))
        carry_ref[c] = g[ts - 8:ts, :]
        conv = (cw_ref[0, 0:1, cols] * g2 + cw_ref[0, 1:2, cols] * g1 + cw_ref[0, 2:3, cols] * g
                + cb_ref[0, :, cols])
        a = (_silu(conv) * v).astype(BF16)
        part = _dot(a, wd_ref[0, cols, :])
        acc = part if acc is None else acc + part
    out_ref[0] = x + mod_ref[0, 0, 5:6, :] * acc


def _ffn(l, x, mod, wup, cw, cb, wd):
    B, S, D = x.shape
    ts = TS_DENSE
    const = pl.Buffered(1)
    return pl.pallas_call(
        _ffn_kernel,
        out_shape=jax.ShapeDtypeStruct((B, S, D), F32),
        grid=(B, S // ts),
        in_specs=[pl.BlockSpec((1, ts, D), lambda b, i: (b, i, 0)),
                  pl.BlockSpec((1, 1, 6, D), lambda b, i: (l, b, 0, 0)),
                  pl.BlockSpec((1, D, 2 * D_FF), lambda b, i: (l, 0, 0), pipeline_mode=const),
                  pl.BlockSpec((1, 3, D_FF), lambda b, i: (l, 0, 0)),
                  pl.BlockSpec((1, 1, D_FF), lambda b, i: (l, 0, 0)),
                  pl.BlockSpec((1, D_FF, D), lambda b, i: (l, 0, 0), pipeline_mode=const)],
        out_specs=pl.BlockSpec((1, ts, D), lambda b, i: (b, i, 0)),
        scratch_shapes=[pltpu.VMEM((D_FF // FF_CHUNK, 8, FF_CHUNK), F32)],
        compiler_params=pltpu.CompilerParams(dimension_semantics=("arbitrary", "arbitrary"),
                                             vmem_limit_bytes=VMEM_LIMIT),
        name="conv_ffn",
    )(x, mod, wup, cw, cb, wd)


def _pad_cols(w, n):
    return jnp.pad(w, [(0, 0)] * (w.ndim - 1) + [(0, n - w.shape[-1])])


def _layout_w_in(w_in):
    o = 0
    seg = {}
    for name, n in (("nq", 256), ("nkc", 64), ("nvc", 64), ("nks", 64), ("nvs", 64), ("nkw", 64), ("nvw", 64),
                    ("ng", 12), ("fq", 256), ("fk", 256), ("fv", 256), ("ff", 4),
                    ("mcq", 256), ("mckv", 128), ("mkr", 32), ("dq", 256), ("dk", 256), ("dv", 256)):
        seg[name] = w_in[..., o:o + n]
        o += n
    parts = [seg["nq"], seg["nks"], seg["nkw"], seg["nvs"], seg["nvw"], seg["nkc"], seg["nvc"],
             _pad_cols(seg["ng"], LANES),
             seg["fq"], seg["fk"], seg["fv"], _pad_cols(seg["ff"], LANES),
             seg["mcq"], seg["mckv"], _pad_cols(seg["mkr"], LANES),
             seg["dq"], seg["dk"], seg["dv"]]
    return jnp.concatenate(parts, axis=-1).astype(BF16)


def _pack_vecs(nsa_qk_g, fox_qk_g, fox_f_b, mla_cq_g, mla_ckv_g, mla_qk_g, diff_qk_g):
    L = nsa_qk_g.shape[0]
    t4 = lambda g: jnp.tile(g, (1, 256 // g.shape[-1]))
    mla_pad = lambda g: _pad_cols(g, LANES)
    rows = [None] * N_VEC
    rows[V_NQ] = t4(nsa_qk_g[:, 0]) * (LOG2E * D_HEAD ** -0.5)
    rows[V_NK] = jnp.concatenate([nsa_qk_g[:, 2], nsa_qk_g[:, 3]], axis=-1)
    rows[V_FQ] = t4(fox_qk_g[:, 0]) * (LOG2E * D_HEAD ** -0.5)
    rows[V_FK] = t4(fox_qk_g[:, 1])
    rows[V_MQ] = t4(mla_pad(mla_qk_g[:, 0])) * (LOG2E * MLA_QK ** -0.5)
    rows[V_MK] = t4(mla_pad(mla_qk_g[:, 1]))
    rows[V_DQ] = t4(diff_qk_g[:, 0]) * (LOG2E * DIFF_QK ** -0.5)
    rows[V_DK] = t4(diff_qk_g[:, 1])
    rows[V_FB] = fox_f_b
    rows[V_CQG] = mla_cq_g
    rows[V_CKVG] = mla_ckv_g
    rows = [jnp.zeros((L, 256), F32) if r is None else _pad_cols(r.astype(F32), 256) for r in rows]
    return jnp.stack(rows, axis=1)


def _block_diag_ones():
    d = jnp.arange(256)
    mats = [(d[:, None] // g == d[None, :] // g) for g in (D_HEAD, LANES, DIFF_QK)]
    return jnp.stack(mats).astype(BF16)


def _overlap_matrix(n_cmp_rows, n_blk):
    c = jnp.arange(n_cmp_rows)[None, :] * CMP_STRIDE
    b = jnp.arange(n_blk)[:, None] * SEL_BLOCK
    ov = (c < b + SEL_BLOCK) & (c + CMP_BLOCK > b)
    return ov.astype(BF16)


def kernel(x, c, positions, ada_w, ada_b, w_in, nsa_qk_g, nsa_cmp_pe, nsa_cmp_w1, nsa_cmp_w2, fox_qk_g, fox_f_b,
           mla_cq_g, mla_ckv_g, mla_w_uq, mla_w_ukv, mla_qk_g, diff_qk_g, diff_lambda, diff_out_g, br_w, gate_w,
           gate_b, w_out, ffn_w_up, ffn_conv_w, ffn_conv_b, ffn_w_down):
    B, S, D = x.shape
    L = ada_w.shape[0]
    H = N_HEADS
    n_half = S // CMP_STRIDE

    mod = _modulation(c, ada_w, ada_b).reshape(L, B, 6, D)
    tab = _rope_tables(positions)

    w1 = _layout_w_in(w_in)
    vecs = _pack_vecs(nsa_qk_g, fox_qk_g, fox_f_b, mla_cq_g, mla_ckv_g, mla_qk_g, diff_qk_g)
    bd = _block_diag_ones()
    tri = (jnp.arange(TS_PROJ)[None, :] <= jnp.arange(TS_PROJ)[:, None]).astype(BF16)
    overlap = _overlap_matrix(n_half, S // SEL_BLOCK)
    wuq = _pad_cols(mla_w_uq.reshape(L, MLA_Q_RANK, H, MLA_QK), LANES).reshape(L, MLA_Q_RANK, H * LANES)
    ukv = mla_w_ukv.reshape(L, MLA_KV_RANK, H, MLA_NOPE + D_HEAD)
    wuk = jnp.pad(ukv[..., :MLA_NOPE], [(0, 0), (0, 0), (0, 0), (MLA_ROPE, LANES - MLA_QK)])
    wuk = wuk.reshape(L, MLA_KV_RANK, H * LANES).astype(BF16)
    wuv = ukv[..., MLA_NOPE:].reshape(L, MLA_KV_RANK, H * D_HEAD).astype(BF16)
    wuq = wuq.astype(BF16)
    pe = nsa_cmp_pe.reshape(L, 2, CMP_BLOCK * D_HEAD)
    cw1 = nsa_cmp_w1.astype(BF16)
    cw2 = nsa_cmp_w2.astype(BF16)
    kcg = _pad_cols(nsa_qk_g[:, 1], LANES).reshape(L, 1, LANES)
    brw = br_w.astype(BF16)
    gw = gate_w.astype(BF16)
    gb = gate_b.reshape(L, 1, 4 * D)
    wo = w_out.astype(BF16)
    wup = ffn_w_up.astype(BF16)
    wd = ffn_w_down.astype(BF16)
    cb = ffn_conv_b.reshape(L, 1, D_FF)
    og = diff_out_g.reshape(L, 1, D_HEAD)

    for l in range(L):
        lam_init = 0.8 - 0.6 * math.exp(-0.3 * l)
        (hbf, nq, nk, nv, nkc, nvc, ng, fq, fk, fv, fcr, mq, mk, mv, dq, dk, dv) = _projection(
            l, x, mod, tab, w1, vecs, bd, tri, wuq, wuk, wuv)
        kcmp, vcmp = _compress(l, nkc.reshape(B, n_half, CMP_STRIDE * D_HEAD),
                               nvc.reshape(B, n_half, CMP_STRIDE * D_HEAD), pe, cw1, cw2, kcg)
        o_nsa = _nsa_attention(nq, kcmp, vcmp, nk, nv, ng, overlap)
        o_fox = _dense_attention("fox", l, lam_init, fq, fk, fv, fcr)
        o_mla = _dense_attention("mla", l, lam_init, mq, mk, mv)
        o_diff = _dense_attention("diff", l, lam_init, dq, dk, dv, diff_lambda, og)
        x = _merge(l, x, hbf, o_nsa, o_fox, o_mla, o_diff, mod, brw, gw, gb, wo)
        x = _ffn(l, x, mod, wup, ffn_conv_w, cb, wd)
    return x
```

```python
import functools
import math

import jax
import jax.numpy as jnp
from jax import lax
from jax.experimental import pallas as pl
from jax.experimental.pallas import tpu as pltpu

F32 = jnp.float32
BF16 = jnp.bfloat16

D_MODEL = 1024
D_HEAD = 64
N_HEADS = 4
MIX_WIDTH = 256
NSA_ROT = 16
CMP_BLOCK = 32
CMP_STRIDE = 16
SEL_BLOCK = 64
SEL_TOPN = 16
WINDOW = 512
MLA_Q_RANK = 256
MLA_KV_RANK = 128
MLA_NOPE = 64
MLA_ROPE = 32
MLA_QK = MLA_ROPE + MLA_NOPE
DIFF_QK = 32
DIFF_ROT = 8
D_FF = 2816
ROPE_THETA = 500000.0
EPS = 1e-6
FORCE_SCORE = 1e4
NEG_BIG = -1e30

LANES = 128
VMEM_LIMIT = 56 * 1024 * 1024

C_NQ, C_NKSW, C_NKVC, C_NG = 0, 256, 384, 512
C_FQ, C_FK, C_FF = 640, 896, 1152
C_MCQ, C_MCKV, C_MKR = 1280, 1536, 1664
C_DQ, C_DK = 1792, 2048
W_IN_PAD = 2304
R_NV, R_FV, R_DV = 0, 128, 384
W_VT_ROWS = 640

V_NQ, V_NK, V_FQ, V_FK, V_MQ, V_MK, V_DQ, V_DK, V_FB, V_CQG, V_CKVG = range(11)
N_VEC = 16

TS_PROJ = 512
TS_DENSE = 512
TQ_ATT = 512
TQ_NSA = 256
TK_NSA = 256
FF_CHUNK = 1408
LOG2E = math.log2(math.e)


def _dot(a, b):
    return jnp.dot(a, b, preferred_element_type=F32)


def _dot_nt(a, b):
    return lax.dot_general(a, b, (((1,), (1,)), ((), ())), preferred_element_type=F32)


def _split2(x):
    hi = x.astype(BF16)
    lo = (x - hi.astype(F32)).astype(BF16)
    return hi, lo


def _split3(x):
    hi = x.astype(BF16)
    r = x - hi.astype(F32)
    mid = r.astype(BF16)
    lo = (r - mid.astype(F32)).astype(BF16)
    return hi, mid, lo


def _rms_rows(x):
    return x * lax.rsqrt(jnp.mean(x * x, axis=-1, keepdims=True) + EPS)


def _group_rms(x, bd, inv_n):
    ss = _dot((x * x).astype(BF16), bd)
    return x * lax.rsqrt(ss * inv_n + EPS)


def _rope(xc, tab_ref, cfg, half):
    c = tab_ref[3 * cfg, 0]
    sa = tab_ref[3 * cfg + 1, 0]
    sb = tab_ref[3 * cfg + 2, 0]
    return xc * c + pltpu.roll(xc, LANES - half, 1) * sa + pltpu.roll(xc, half, 1) * sb


def _log_sigmoid(x):
    return jnp.minimum(x, 0.0) - jnp.log1p(jnp.exp(-jnp.abs(x)))


def _sigmoid(x):
    return 1.0 / (1.0 + jnp.exp(-x))


def _silu(x):
    return x * _sigmoid(x)


def _softmax_step(st, v1t, m, acc):
    m_new = jnp.maximum(m, jnp.max(st, axis=0, keepdims=True))
    p = jnp.exp2(st - m_new).astype(BF16)
    return m_new, jnp.exp2(m - m_new) * acc + _dot(v1t, p)


def _softmax_finish(acc):
    return acc[:D_HEAD] / acc[D_HEAD:D_HEAD + 1]


def _mod_kernel(c_ref, w_ref, b_ref, o_ref):
    c = c_ref[...]
    a = _silu(c).astype(BF16)
    o_ref[0] = _dot(a, w_ref[0].astype(BF16)) + b_ref[0]


def _modulation(c, ada_w, ada_b):
    L, D, N = ada_w.shape
    B = c.shape[0]
    tn = 1536
    return pl.pallas_call(
        _mod_kernel,
        out_shape=jax.ShapeDtypeStruct((L, B, N), F32),
        grid=(L, N // tn),
        in_specs=[pl.BlockSpec((B, D), lambda l, n: (0, 0)),
                  pl.BlockSpec((1, D, tn), lambda l, n: (l, 0, n)),
                  pl.BlockSpec((1, 1, tn), lambda l, n: (l, 0, n))],
        out_specs=pl.BlockSpec((1, B, tn), lambda l, n: (l, 0, n)),
        compiler_params=pltpu.CompilerParams(vmem_limit_bytes=VMEM_LIMIT),
        name="adaln_mod",
    )(c, ada_w, ada_b.reshape(L, 1, N))


def _rope_tab_kernel(pos_ref, cst_ref, o_ref):
    pos = pos_ref[0].astype(F32)
    for cfg in range(3):
        ang = pos * cst_ref[cfg:cfg + 1, :]
        cs = jnp.cos(ang)
        sn = jnp.sin(ang)
        o_ref[3 * cfg, 0] = cs
        o_ref[3 * cfg + 1, 0] = sn * cst_ref[3 + cfg:4 + cfg, :]
        o_ref[3 * cfg + 2, 0] = sn * cst_ref[6 + cfg:7 + cfg, :]


def _rope_consts():
    rows_f, rows_a, rows_b = [], [], []
    for group, n_rot in ((D_HEAD, NSA_ROT), (LANES, MLA_ROPE), (DIFF_QK, DIFF_ROT)):
        half = n_rot // 2
        inv_freq = ROPE_THETA ** (-jnp.arange(half, dtype=F32) / half)
        d = jnp.arange(LANES) % group
        f = jnp.where(d < n_rot, inv_freq[d % half], 0.0)
        rows_f.append(f)
        rows_a.append(jnp.where(d < half, -1.0, 0.0))
        rows_b.append(jnp.where((d >= half) & (d < n_rot), 1.0, 0.0))
    return jnp.stack(rows_f + rows_a + rows_b).astype(F32)


def _rope_tables(positions):
    B, S = positions.shape
    ts = 512
    return pl.pallas_call(
        _rope_tab_kernel,
        out_shape=jax.ShapeDtypeStruct((9, B, S, LANES), F32),
        grid=(B, S // ts),
        in_specs=[pl.BlockSpec((1, ts, 1), lambda b, i: (b, i, 0)),
                  pl.BlockSpec((9, LANES), lambda b, i: (0, 0))],
        out_specs=pl.BlockSpec((9, 1, ts, LANES), lambda b, i: (0, b, i, 0)),
        name="rope_tables",
    )(positions.reshape(B, S, 1), _rope_consts())


def _proj_kernel(x_ref, mod_ref, tab_ref, w_ref, wvt_ref, vec_ref, bd_ref, tri_ref, wuq_ref, wuk_ref, wuvt_ref,
                 hbf_ref, nq_ref, nk_ref, nv_ref, nkc_ref, nvc_ref, ng_ref,
                 fq_ref, fk_ref, fv_ref, fkb_ref,
                 mq_ref, mk_ref, mv_ref, dq_ref, dk_ref, dv_ref, carry_ref):
    i = pl.program_id(1)
    x = x_ref[0]
    sh1 = mod_ref[0, 0, 0:1, :]
    sc1 = mod_ref[0, 0, 1:2, :]
    hb = (_rms_rows(x) * (1.0 + sc1) + sh1).astype(BF16)
    hbf_ref[0] = hb

    def proj(c0, n):
        return _dot(hb, w_ref[0, :, c0:c0 + n])

    def vec(r, n=LANES):
        return vec_ref[0, r:r + 1, 0:n]

    bd64, bd128, bd32 = bd_ref[0], bd_ref[1], bd_ref[2]
    row = lax.broadcasted_iota(jnp.int32, (x.shape[0], 1), 0)
    lane = lax.broadcasted_iota(jnp.int32, (1, LANES), 1)

    def head_pair(seg, gain_row, bd, inv_n, cfg, half, out_ref):
        yn = _group_rms(seg, bd, inv_n) * vec(gain_row, 256)
        for c in range(2):
            y = yn[:, c * LANES:(c + 1) * LANES]
            if cfg is not None:
                y = _rope(y, tab_ref, cfg, half)
            yb = y.astype(BF16)
            out_ref[0, 2 * c] = yb[:, :D_HEAD]
            out_ref[0, 2 * c + 1] = yb[:, D_HEAD:]

    ones_rows = jnp.where(lax.broadcasted_iota(jnp.int32, (D_HEAD, x.shape[0]), 0) == 0, 1.0, 0.0).astype(BF16)

    def value_store(vt, out_ref):
        for hd in range(vt.shape[0] // D_HEAD):
            out_ref[0, hd] = jnp.concatenate([vt[hd * D_HEAD:(hd + 1) * D_HEAD].astype(BF16), ones_rows], axis=0)

    vt_all = _dot_nt(wvt_ref[0], hb)

    head_pair(proj(C_NQ, 256), V_NQ, bd64, 1.0 / D_HEAD, 0, NSA_ROT // 2, nq_ref)
    ksw = _group_rms(proj(C_NKSW, LANES), bd64[0:LANES, 0:LANES], 1.0 / D_HEAD) * vec(V_NK)
    ksw = _rope(ksw, tab_ref, 0, NSA_ROT // 2)
    blk_hot = jnp.where(lane - D_HEAD == (i * x.shape[0] + row) // SEL_BLOCK, 1.0, 0.0)
    nk_ref[0, 0] = jnp.where(lane < D_HEAD, ksw, blk_hot).astype(BF16)
    nk_ref[0, 1] = jnp.where(lane < D_HEAD, pltpu.roll(ksw, D_HEAD, 1), 0.0).astype(BF16)
    value_store(vt_all[R_NV:R_FV], nv_ref)
    kvc = proj(C_NKVC, LANES)
    nkc_ref[0] = kvc[:, :D_HEAD]
    nvc_ref[0] = kvc[:, D_HEAD:]
    ng_ref[0] = _sigmoid(proj(C_NG, LANES))

    head_pair(proj(C_FQ, 256), V_FQ, bd64, 1.0 / D_HEAD, None, 0, fq_ref)
    head_pair(proj(C_FK, 256), V_FK, bd64, 1.0 / D_HEAD, None, 0, fk_ref)
    value_store(vt_all[R_FV:R_DV], fv_ref)

    @pl.when(i == 0)
    def _():
        carry_ref[...] = jnp.zeros_like(carry_ref)

    lf = _log_sigmoid(proj(C_FF, LANES) + vec(V_FB))
    tri = tri_ref[...]
    p0, p1, p2 = _split3(lf)
    cum = _dot(tri, p0) + _dot(tri, p1) + _dot(tri, p2) + carry_ref[0:1, :]
    ts = cum.shape[0]
    carry_ref[0:1, :] = cum[ts - 1:ts, :]
    fkb_ref[0] = cum * (-LOG2E)

    cq = proj(C_MCQ, MLA_Q_RANK)
    cqn = (_rms_rows(cq) * vec(V_CQG, MLA_Q_RANK)).astype(BF16)
    qm = _dot(cqn, wuq_ref[0])
    ckv = proj(C_MCKV, MLA_KV_RANK)
    ckvn = (_rms_rows(ckv) * vec(V_CKVG)).astype(BF16)
    kk = _dot(ckvn, wuk_ref[0])
    vvt = _dot_nt(wuvt_ref[0], ckvn)
    kr = proj(C_MKR, LANES)
    kr2 = jnp.concatenate([kr, kr], axis=1)
    for pair in range(N_HEADS // 2):
        sl = slice(pair * 256, (pair + 1) * 256)
        yq = _group_rms(qm[:, sl], bd128, 1.0 / MLA_QK) * vec(V_MQ, 256)
        yk = _group_rms(kk[:, sl] + kr2, bd128, 1.0 / MLA_QK) * vec(V_MK, 256)
        for c in range(2):
            cs = slice(c * LANES, (c + 1) * LANES)
            mq_ref[0, 2 * pair + c] = _rope(yq[:, cs], tab_ref, 1, MLA_ROPE // 2).astype(BF16)
            mk_ref[0, 2 * pair + c] = _rope(yk[:, cs], tab_ref, 1, MLA_ROPE // 2).astype(BF16)
    value_store(vvt, mv_ref)

    head_pair(proj(C_DQ, 256), V_DQ, bd32, 1.0 / DIFF_QK, 2, DIFF_ROT // 2, dq_ref)
    head_pair(proj(C_DK, 256), V_DK, bd32, 1.0 / DIFF_QK, 2, DIFF_ROT // 2, dk_ref)
    value_store(vt_all[R_DV:W_VT_ROWS], dv_ref)


def _projection(l, x, mod, tab, w1, wvt, vecs, bd, tri, wuq, wuk, wuvt):
    B, S, D = x.shape
    ts = TS_PROJ
    H = N_HEADS
    hm = lambda d, dt: jax.ShapeDtypeStruct((B, H, S, d), dt)
    hm_spec = lambda nh, d: pl.BlockSpec((1, nh, ts, d), lambda b, i: (b, 0, i, 0))
    row_spec = lambda d: pl.BlockSpec((1, ts, d), lambda b, i: (b, i, 0))
    vt = lambda nh: jax.ShapeDtypeStruct((B, nh, LANES, S), BF16)
    vt_spec = lambda nh: pl.BlockSpec((1, nh, LANES, ts), lambda b, i: (b, 0, 0, i))
    out_shape = [
        jax.ShapeDtypeStruct((B, S, D), BF16),
        hm(D_HEAD, BF16),
        jax.ShapeDtypeStruct((B, 2, S, LANES), BF16),
        vt(2),
        jax.ShapeDtypeStruct((B, S, D_HEAD), F32),
        jax.ShapeDtypeStruct((B, S, D_HEAD), F32),
        jax.ShapeDtypeStruct((B, S, LANES), F32),
        hm(D_HEAD, BF16), hm(D_HEAD, BF16), vt(H),
        jax.ShapeDtypeStruct((B, S, LANES), F32),
        hm(LANES, BF16), hm(LANES, BF16), vt(H),
        hm(D_HEAD, BF16), hm(D_HEAD, BF16), vt(H),
    ]
    out_specs = [
        row_spec(D), hm_spec(H, D_HEAD), hm_spec(2, LANES), vt_spec(2),
        row_spec(D_HEAD), row_spec(D_HEAD), row_spec(LANES),
        hm_spec(H, D_HEAD), hm_spec(H, D_HEAD), vt_spec(H),
        row_spec(LANES),
        hm_spec(H, LANES), hm_spec(H, LANES), vt_spec(H),
        hm_spec(H, D_HEAD), hm_spec(H, D_HEAD), vt_spec(H),
    ]
    in_specs = [
        row_spec(D),
        pl.BlockSpec((1, 1, 6, D), lambda b, i: (l, b, 0, 0)),
        pl.BlockSpec((9, 1, ts, LANES), lambda b, i: (0, b, i, 0)),
        pl.BlockSpec((1, D, W_IN_PAD), lambda b, i: (l, 0, 0)),
        pl.BlockSpec((1, W_VT_ROWS, D), lambda b, i: (l, 0, 0)),
        pl.BlockSpec((1, N_VEC, 256), lambda b, i: (l, 0, 0)),
        pl.BlockSpec((3, 256, 256), lambda b, i: (0, 0, 0)),
        pl.BlockSpec((ts, ts), lambda b, i: (0, 0)),
        pl.BlockSpec((1, MLA_Q_RANK, 512), lambda b, i: (l, 0, 0)),
        pl.BlockSpec((1, MLA_KV_RANK, 512), lambda b, i: (l, 0, 0)),
        pl.BlockSpec((1, 256, MLA_KV_RANK), lambda b, i: (l, 0, 0)),
    ]
    return pl.pallas_call(
        _proj_kernel,
        out_shape=out_shape,
        grid=(B, S // ts),
        in_specs=in_specs,
        out_specs=out_specs,
        scratch_shapes=[pltpu.VMEM((8, LANES), F32)],
        compiler_params=pltpu.CompilerParams(dimension_semantics=("arbitrary", "arbitrary"),
                                             vmem_limit_bytes=VMEM_LIMIT),
        name="in_proj",
    )(x, mod, tab, w1, wvt, vecs, bd, tri, wuq, wuk, wuvt)


def _cmp_kernel(kc_ref, vc_ref, pe_ref, w1_ref, w2_ref, g_ref, ko_ref, vo_ref):
    half = CMP_STRIDE * D_HEAD
    n = kc_ref.shape[1]

    def hidden(x2, j):
        xa = _dot((x2 + pe_ref[0, j:j + 1, 0:half]).astype(BF16), w1_ref[0, j, 0:half, :])
        xb = _dot((x2 + pe_ref[0, j:j + 1, half:2 * half]).astype(BF16), w1_ref[0, j, half:2 * half, :])
        return _silu(xa + pltpu.roll(xb, n - 1, 0)).astype(BF16)

    kcmp = _rms_rows(_dot(hidden(kc_ref[0], 0), w2_ref[0, 0])) * g_ref[0, 0:1, 0:D_HEAD]
    ko_ref[0] = kcmp.astype(BF16)
    vo_ref[0] = _dot_nt(w2_ref[0, 1], hidden(vc_ref[0], 1)).astype(BF16)


def _compress(l, kc2, vc2, pe, w1, w2, g):
    B, n, wide = kc2.shape
    return pl.pallas_call(
        _cmp_kernel,
        out_shape=[jax.ShapeDtypeStruct((B, n, D_HEAD), BF16), jax.ShapeDtypeStruct((B, D_HEAD, n), BF16)],
        grid=(B,),
        in_specs=[pl.BlockSpec((1, n, wide), lambda b: (b, 0, 0)),
                  pl.BlockSpec((1, n, wide), lambda b: (b, 0, 0)),
                  pl.BlockSpec((1, 2, 2 * wide), lambda b: (l, 0, 0)),
                  pl.BlockSpec((1, 2, 2 * wide, D_HEAD), lambda b: (l, 0, 0, 0)),
                  pl.BlockSpec((1, 2, D_HEAD, D_HEAD), lambda b: (l, 0, 0, 0)),
                  pl.BlockSpec((1, 1, LANES), lambda b: (l, 0, 0))],
        out_specs=[pl.BlockSpec((1, n, D_HEAD), lambda b: (b, 0, 0)),
                   pl.BlockSpec((1, D_HEAD, n), lambda b: (b, 0, 0))],
        name="nsa_compress",
    )(kc2, vc2, pe, w1, w2, g)


def _nsa_kernel(q_ref, kc_ref, vc_ref, k_ref, v_ref, g_ref, ov_ref, o_ref):
    i = pl.program_id(1)
    tq, tk = TQ_NSA, TK_NSA
    H = N_HEADS
    rows = H * tq
    n_blk = ov_ref.shape[0]
    q = q_ref[0].reshape(rows, D_HEAD)
    t4 = i * tq + lax.broadcasted_iota(jnp.int32, (1, rows), 1) % tq

    nc = kc_ref.shape[1]
    sc = _dot_nt(kc_ref[0], q)
    cend = lax.broadcasted_iota(jnp.int32, (nc, 1), 0) * CMP_STRIDE + (CMP_BLOCK - 1)
    sc = jnp.where(cend <= t4, sc, NEG_BIG)
    e = jnp.exp2(sc - jnp.max(sc, axis=0, keepdims=True))
    p = e / jnp.sum(e, axis=0, keepdims=True)
    p = jnp.where(t4 >= CMP_BLOCK - 1, p, 0.0)
    o_cmp = _dot(vc_ref[0], p.astype(BF16))
    psum = p[:, 0:tq] + p[:, tq:2 * tq] + p[:, 2 * tq:3 * tq] + p[:, 3 * tq:4 * tq]
    p0, p1, p2 = _split3(psum)
    ov = ov_ref[...]
    imp = _dot(ov, p0) + _dot(ov, p1) + _dot(ov, p2)

    blk = lax.broadcasted_iota(jnp.int32, (n_blk, 1), 0)
    tl = i * tq + lax.broadcasted_iota(jnp.int32, (1, tq), 1)
    cur = tl // SEL_BLOCK
    forced = (blk == 0) | (blk == cur) | (blk == cur - 1)
    score = jnp.where(blk * SEL_BLOCK > tl, -1.0, jnp.where(forced, FORCE_SCORE, imp))
    rank = jnp.zeros((n_blk, tq), F32)
    for r in range(n_blk):
        other = score[r:r + 1, :]
        ahead = (other > score) | ((other == score) & (blk > r))
        rank = rank + jnp.where(ahead, 1.0, 0.0)
    drop_t = jnp.where(rank < float(min(SEL_TOPN, n_blk)), 0.0, NEG_BIG)
    drop = jnp.concatenate([drop_t, jnp.zeros((LANES - n_blk, tq), F32)], axis=0).T.astype(BF16)
    qx = jnp.concatenate([q, jnp.concatenate([drop[:, 0:D_HEAD]] * H, axis=0)], axis=1)

    def slc_scores(j):
        return _dot_nt(k_ref[0, 0, pl.ds(j * tk, tk), :], qx)

    init = (jnp.full((1, rows), NEG_BIG, F32), jnp.zeros((LANES, rows), F32))

    def far_step(j, carry):
        return _softmax_step(slc_scores(j), v_ref[0, 0, :, pl.ds(j * tk, tk)], *carry)

    def near_step(j, carry):
        dist = t4 - (j * tk + lax.broadcasted_iota(jnp.int32, (tk, 1), 0))
        s_s = jnp.where(dist >= 0, slc_scores(j), NEG_BIG)
        s_w = _dot_nt(k_ref[0, 1, pl.ds(j * tk, tk), :], qx)
        s_w = jnp.where((dist >= 0) & (dist < WINDOW), s_w, NEG_BIG)
        c_s = _softmax_step(s_s, v_ref[0, 0, :, pl.ds(j * tk, tk)], *carry[0])
        c_w = _softmax_step(s_w, v_ref[0, 1, :, pl.ds(j * tk, tk)], *carry[1])
        return c_s, c_w

    lo = jnp.maximum(i - WINDOW // tk, 0)
    c_s = lax.fori_loop(0, lo, far_step, init)
    (_, a_s), (_, a_w) = lax.fori_loop(lo, i + 1, near_step, (c_s, init))
    o_slc = _softmax_finish(a_s)
    o_win = _softmax_finish(a_w)

    gt = g_ref[0].T
    outs = []
    for hd in range(H):
        r = slice(hd * tq, (hd + 1) * tq)
        outs.append(gt[hd:hd + 1] * o_cmp[:, r] + gt[H + hd:H + hd + 1] * o_slc[:, r]
                    + gt[2 * H + hd:2 * H + hd + 1] * o_win[:, r])
    o_ref[0] = jnp.concatenate(outs, axis=0).T.astype(BF16)


def _nsa_attention(q, kcmp, vcmp, ksw, vsw, gates, overlap):
    B, H, S, _ = q.shape
    tq = TQ_NSA
    nc = kcmp.shape[1]
    return pl.pallas_call(
        _nsa_kernel,
        out_shape=jax.ShapeDtypeStruct((B, S, MIX_WIDTH), BF16),
        grid=(B, S // tq),
        in_specs=[pl.BlockSpec((1, H, tq, D_HEAD), lambda b, i: (b, 0, i, 0)),
                  pl.BlockSpec((1, nc, D_HEAD), lambda b, i: (b, 0, 0)),
                  pl.BlockSpec((1, D_HEAD, nc), lambda b, i: (b, 0, 0)),
                  pl.BlockSpec((1, 2, S, LANES), lambda b, i: (b, 0, 0, 0)),
                  pl.BlockSpec((1, 2, LANES, S), lambda b, i: (b, 0, 0, 0)),
                  pl.BlockSpec((1, tq, LANES), lambda b, i: (b, i, 0)),
                  pl.BlockSpec(overlap.shape, lambda b, i: (0, 0))],
        out_specs=pl.BlockSpec((1, tq, MIX_WIDTH), lambda b, i: (b, i, 0)),
        compiler_params=pltpu.CompilerParams(vmem_limit_bytes=VMEM_LIMIT),
        name="nsa_attention",
    )(q, kcmp, vcmp, ksw, vsw, gates, overlap)


def _attn_kernel(mode, lam_init, *refs):
    if mode == "fox":
        q_ref, k_ref, v_ref, kb_ref, o_ref = refs
    elif mode == "diff":
        q_ref, k_ref, v_ref, lam_ref, og_ref, o_ref = refs
    else:
        q_ref, k_ref, v_ref, o_ref = refs
    i = pl.program_id(1)
    t = TQ_ATT
    H = N_HEADS
    qs = []
    for hd in range(H):
        q = q_ref[0, hd]
        if mode == "diff":
            lane = lax.broadcasted_iota(jnp.int32, (1, D_HEAD), 1)
            zero = jnp.zeros_like(q)
            q = jnp.concatenate([jnp.where(lane < DIFF_QK, q, zero), jnp.where(lane >= DIFF_QK, q, zero)], axis=0)
        qs.append(q)
    rows = qs[0].shape[0]

    def step(j, carry, mask=None):
        out = []
        for hd in range(H):
            s = _dot_nt(k_ref[0, hd, pl.ds(j * t, t), :], qs[hd])
            if mode == "fox":
                s = s + kb_ref[0, pl.ds(j * t, t), :][:, hd:hd + 1]
            if mask is not None:
                s = jnp.where(mask, s, NEG_BIG)
            out.append(_softmax_step(s, v_ref[0, hd, :, pl.ds(j * t, t)], *carry[hd]))
        return tuple(out)

    init = tuple((jnp.full((1, rows), NEG_BIG, F32), jnp.zeros((LANES, rows), F32)) for _ in range(H))
    carry = lax.fori_loop(0, i, step, init)
    kk = lax.broadcasted_iota(jnp.int32, (t, 1), 0)
    rr = lax.broadcasted_iota(jnp.int32, (1, rows), 1) % t
    carry = step(i, carry, mask=kk <= rr)

    outs = []
    for hd in range(H):
        o = _softmax_finish(carry[hd][1])
        if mode == "diff":
            lv = lam_ref[0]
            lam = (jnp.exp(jnp.sum(lv[0:1] * lv[1:2], axis=-1, keepdims=True))
                   - jnp.exp(jnp.sum(lv[2:3] * lv[3:4], axis=-1, keepdims=True)) + lam_init)
            o = o[:, 0:t] - lam * o[:, t:2 * t]
            o = o * lax.rsqrt(jnp.mean(o * o, axis=0, keepdims=True) + EPS) * og_ref[0] * (1.0 - lam_init)
        outs.append(o)
    o_ref[0] = jnp.concatenate(outs, axis=0).T.astype(BF16)


def _dense_attention(mode, l, lam_init, q, k, v, *extra):
    B, H, S, dk = q.shape
    t = TQ_ATT
    in_specs = [pl.BlockSpec((1, H, t, dk), lambda b, i: (b, 0, i, 0)),
                pl.BlockSpec((1, H, S, dk), lambda b, i: (b, 0, 0, 0)),
                pl.BlockSpec((1, H, LANES, S), lambda b, i: (b, 0, 0, 0))]
    if mode == "fox":
        in_specs += [pl.BlockSpec((1, S, LANES), lambda b, i: (b, 0, 0))]
    elif mode == "diff":
        in_specs += [pl.BlockSpec((1, 4, DIFF_QK), lambda b, i: (l, 0, 0)),
                     pl.BlockSpec((1, D_HEAD, 1), lambda b, i: (l, 0, 0))]
    return pl.pallas_call(
        functools.partial(_attn_kernel, mode, lam_init),
        out_shape=jax.ShapeDtypeStruct((B, S, MIX_WIDTH), BF16),
        grid=(B, S // t),
        in_specs=in_specs,
        out_specs=pl.BlockSpec((1, t, MIX_WIDTH), lambda b, i: (b, i, 0)),
        compiler_params=pltpu.CompilerParams(vmem_limit_bytes=VMEM_LIMIT),
        name=mode + "_attention",
    )(q, k, v, *extra)


def _merge_kernel(x_ref, h_ref, o0_ref, o1_ref, o2_ref, o3_ref, mod_ref, brw_ref, gw_ref, gb_ref, wo_ref,
                  out_ref):
    hb = h_ref[0]
    merged = None
    for m, o_ref in enumerate((o0_ref, o1_ref, o2_ref, o3_ref)):
        y = _dot(o_ref[0], brw_ref[0, m])
        cols = slice(m * D_MODEL, (m + 1) * D_MODEL)
        gate = _sigmoid(_dot(hb, gw_ref[0, :, cols]) + gb_ref[0, :, cols])
        merged = gate * y if merged is None else merged + gate * y
    out = _dot(merged.astype(BF16), wo_ref[0])
    out_ref[0] = x_ref[0] + mod_ref[0, 0, 2:3, :] * out


def _merge(l, x, hbf, o_nsa, o_fox, o_mla, o_diff, mod, brw, gw, gb, wo):
    B, S, D = x.shape
    ts = TS_DENSE
    row = lambda d: pl.BlockSpec((1, ts, d), lambda b, i: (b, i, 0))
    return pl.pallas_call(
        _merge_kernel,
        out_shape=jax.ShapeDtypeStruct((B, S, D), F32),
        grid=(B, S // ts),
        in_specs=[row(D), row(D), row(MIX_WIDTH), row(MIX_WIDTH), row(MIX_WIDTH), row(MIX_WIDTH),
                  pl.BlockSpec((1, 1, 6, D), lambda b, i: (l, b, 0, 0)),
                  pl.BlockSpec((1, 4, MIX_WIDTH, D), lambda b, i: (l, 0, 0, 0)),
                  pl.BlockSpec((1, D, 4 * D), lambda b, i: (l, 0, 0)),
                  pl.BlockSpec((1, 1, 4 * D), lambda b, i: (l, 0, 0)),
                  pl.BlockSpec((1, D, D), lambda b, i: (l, 0, 0))],
        out_specs=row(D),
        compiler_params=pltpu.CompilerParams(vmem_limit_bytes=VMEM_LIMIT),
        name="merge_out",
    )(x, hbf, o_nsa, o_fox, o_mla, o_diff, mod, brw, gw, gb, wo)


def _ffn_kernel(x_ref, mod_ref, wup_ref, cw_ref, cb_ref, wd_ref, out_ref, carry_ref):
    i = pl.program_id(1)
    x = x_ref[0]
    ts = x.shape[0]
    sh2 = mod_ref[0, 0, 3:4, :]
    sc2 = mod_ref[0, 0, 4:5, :]
    hb = (_rms_rows(x) * (1.0 + sc2) + sh2).astype(BF16)

    @pl.when(i == 0)
    def _():
        carry_ref[...] = jnp.zeros_like(carry_ref)

    row = lax.broadcasted_iota(jnp.int32, (ts, 1), 0)
    acc = None
    for c in range(D_FF // FF_CHUNK):
        cols = slice(c * FF_CHUNK, (c + 1) * FF_CHUNK)
        g = _dot(hb, wup_ref[0, :, cols])
        v = _dot(hb, wup_ref[0, :, D_FF + c * FF_CHUNK:D_FF + (c + 1) * FF_CHUNK])
        prev = carry_ref[c]
        g1 = jnp.where(row == 0, prev[7:8, :], pltpu.roll(g, 1, 0))
        g2 = jnp.where(row == 0, prev[6:7, :], jnp.where(row == 1, prev[7:8, :], pltpu.roll(g, 2, 0)))
        carry_ref[c] = g[ts - 8:ts, :]
        conv = (cw_ref[0, 0:1, cols] * g2 + cw_ref[0, 1:2, cols] * g1 + cw_ref[0, 2:3, cols] * g
                + cb_ref[0, :, cols])
        a = (_silu(conv) * v).astype(BF16)
        part = _dot(a, wd_ref[0, cols, :])
        acc = part if acc is None else acc + part
    out_ref[0] = x + mod_ref[0, 0, 5:6, :] * acc


def _ffn(l, x, mod, wup, cw, cb, wd):
    B, S, D = x.shape
    ts = TS_DENSE
    const = pl.Buffered(1)
    return pl.pallas_call(
        _ffn_kernel,
        out_shape=jax.ShapeDtypeStruct((B, S, D), F32),
        grid=(B, S // ts),
        in_specs=[pl.BlockSpec((1, ts, D), lambda b, i: (b, i, 0)),
                  pl.BlockSpec((1, 1, 6, D), lambda b, i: (l, b, 0, 0)),
                  pl.BlockSpec((1, D, 2 * D_FF), lambda b, i: (l, 0, 0), pipeline_mode=const),
                  pl.BlockSpec((1, 3, D_FF), lambda b, i: (l, 0, 0)),
                  pl.BlockSpec((1, 1, D_FF), lambda b, i: (l, 0, 0)),
                  pl.BlockSpec((1, D_FF, D), lambda b, i: (l, 0, 0), pipeline_mode=const)],
        out_specs=pl.BlockSpec((1, ts, D), lambda b, i: (b, i, 0)),
        scratch_shapes=[pltpu.VMEM((D_FF // FF_CHUNK, 8, FF_CHUNK), F32)],
        compiler_params=pltpu.CompilerParams(dimension_semantics=("arbitrary", "arbitrary"),
                                             vmem_limit_bytes=VMEM_LIMIT),
        name="conv_ffn",
    )(x, mod, wup, cw, cb, wd)


def _pad_cols(w, n):
    return jnp.pad(w, [(0, 0)] * (w.ndim - 1) + [(0, n - w.shape[-1])])


def _layout_w_in(w_in):
    o = 0
    seg = {}
    for name, n in (("nq", 256), ("nkc", 64), ("nvc", 64), ("nks", 64), ("nvs", 64), ("nkw", 64), ("nvw", 64),
                    ("ng", 12), ("fq", 256), ("fk", 256), ("fv", 256), ("ff", 4),
                    ("mcq", 256), ("mckv", 128), ("mkr", 32), ("dq", 256), ("dk", 256), ("dv", 256)):
        seg[name] = w_in[..., o:o + n]
        o += n
    parts = [seg["nq"], seg["nks"], seg["nkw"], seg["nkc"], seg["nvc"], _pad_cols(seg["ng"], LANES),
             seg["fq"], seg["fk"], _pad_cols(seg["ff"], LANES),
             seg["mcq"], seg["mckv"], _pad_cols(seg["mkr"], LANES),
             seg["dq"], seg["dk"]]
    w1 = jnp.concatenate(parts, axis=-1).astype(BF16)
    wvt = jnp.concatenate([seg["nvs"], seg["nvw"], seg["fv"], seg["dv"]], axis=-1)
    return w1, jnp.swapaxes(wvt, -1, -2).astype(BF16)


def _pack_vecs(nsa_qk_g, fox_qk_g, fox_f_b, mla_cq_g, mla_ckv_g, mla_qk_g, diff_qk_g):
    L = nsa_qk_g.shape[0]
    t4 = lambda g: jnp.tile(g, (1, 256 // g.shape[-1]))
    mla_pad = lambda g: _pad_cols(g, LANES)
    rows = [None] * N_VEC
    rows[V_NQ] = t4(nsa_qk_g[:, 0]) * (LOG2E * D_HEAD ** -0.5)
    rows[V_NK] = jnp.concatenate([nsa_qk_g[:, 2], nsa_qk_g[:, 3]], axis=-1)
    rows[V_FQ] = t4(fox_qk_g[:, 0]) * (LOG2E * D_HEAD ** -0.5)
    rows[V_FK] = t4(fox_qk_g[:, 1])
    rows[V_MQ] = t4(mla_pad(mla_qk_g[:, 0])) * (LOG2E * MLA_QK ** -0.5)
    rows[V_MK] = t4(mla_pad(mla_qk_g[:, 1]))
    rows[V_DQ] = t4(diff_qk_g[:, 0]) * (LOG2E * DIFF_QK ** -0.5)
    rows[V_DK] = t4(diff_qk_g[:, 1])
    rows[V_FB] = fox_f_b
    rows[V_CQG] = mla_cq_g
    rows[V_CKVG] = mla_ckv_g
    rows = [jnp.zeros((L, 256), F32) if r is None else _pad_cols(r.astype(F32), 256) for r in rows]
    return jnp.stack(rows, axis=1)


def _block_diag_ones():
    d = jnp.arange(256)
    mats = [(d[:, None] // g == d[None, :] // g) for g in (D_HEAD, LANES, DIFF_QK)]
    return jnp.stack(mats).astype(BF16)


def _overlap_matrix(n_cmp_rows, n_blk):
    c = jnp.arange(n_cmp_rows)[None, :] * CMP_STRIDE
    b = jnp.arange(n_blk)[:, None] * SEL_BLOCK
    ov = (c < b + SEL_BLOCK) & (c + CMP_BLOCK > b)
    return ov.astype(BF16)


def kernel(x, c, positions, ada_w, ada_b, w_in, nsa_qk_g, nsa_cmp_pe, nsa_cmp_w1, nsa_cmp_w2, fox_qk_g, fox_f_b,
           mla_cq_g, mla_ckv_g, mla_w_uq, mla_w_ukv, mla_qk_g, diff_qk_g, diff_lambda, diff_out_g, br_w, gate_w,
           gate_b, w_out, ffn_w_up, ffn_conv_w, ffn_conv_b, ffn_w_down):
    B, S, D = x.shape
    L = ada_w.shape[0]
    H = N_HEADS
    n_half = S // CMP_STRIDE

    mod = _modulation(c, ada_w, ada_b).reshape(L, B, 6, D)
    tab = _rope_tables(positions)

    w1, wvt = _layout_w_in(w_in)
    vecs = _pack_vecs(nsa_qk_g, fox_qk_g, fox_f_b, mla_cq_g, mla_ckv_g, mla_qk_g, diff_qk_g)
    bd = _block_diag_ones()
    tri = (jnp.arange(TS_PROJ)[None, :] <= jnp.arange(TS_PROJ)[:, None]).astype(BF16)
    overlap = _overlap_matrix(n_half, S // SEL_BLOCK)
    wuq = _pad_cols(mla_w_uq.reshape(L, MLA_Q_RANK, H, MLA_QK), LANES).reshape(L, MLA_Q_RANK, H * LANES)
    ukv = mla_w_ukv.reshape(L, MLA_KV_RANK, H, MLA_NOPE + D_HEAD)
    wuk = jnp.pad(ukv[..., :MLA_NOPE], [(0, 0), (0, 0), (0, 0), (MLA_ROPE, LANES - MLA_QK)])
    wuk = wuk.reshape(L, MLA_KV_RANK, H * LANES).astype(BF16)
    wuvt = jnp.swapaxes(ukv[..., MLA_NOPE:].reshape(L, MLA_KV_RANK, H * D_HEAD), 1, 2).astype(BF16)
    wuq = wuq.astype(BF16)
    pe = nsa_cmp_pe.reshape(L, 2, CMP_BLOCK * D_HEAD)
    cw1 = nsa_cmp_w1.astype(BF16)
    cw2 = jnp.stack([nsa_cmp_w2[:, 0], jnp.swapaxes(nsa_cmp_w2[:, 1], 1, 2)], axis=1).astype(BF16)
    kcg = _pad_cols(nsa_qk_g[:, 1], LANES).reshape(L, 1, LANES)
    brw = br_w.astype(BF16)
    gw = gate_w.astype(BF16)
    gb = gate_b.reshape(L, 1, 4 * D)
    wo = w_out.astype(BF16)
    wup = ffn_w_up.astype(BF16)
    wd = ffn_w_down.astype(BF16)
    cb = ffn_conv_b.reshape(L, 1, D_FF)
    og = diff_out_g.reshape(L, D_HEAD, 1)

    for l in range(L):
        lam_init = 0.8 - 0.6 * math.exp(-0.3 * l)
        (hbf, nq, nk, nv, nkc, nvc, ng, fq, fk, fv, fkb, mq, mk, mv, dq, dk, dv) = _projection(
            l, x, mod, tab, w1, wvt, vecs, bd, tri, wuq, wuk, wuvt)
        kcmp, vcmp = _compress(l, nkc.reshape(B, n_half, CMP_STRIDE * D_HEAD),
                               nvc.reshape(B, n_half, CMP_STRIDE * D_HEAD), pe, cw1, cw2, kcg)
        o_nsa = _nsa_attention(nq, kcmp, vcmp, nk, nv, ng, overlap)
        o_fox = _dense_attention("fox", l, lam_init, fq, fk, fv, fkb)
        o_mla = _dense_attention("mla", l, lam_init, mq, mk, mv)
        o_diff = _dense_attention("diff", l, lam_init, dq, dk, dv, diff_lambda, og)
        x = _merge(l, x, hbf, o_nsa, o_fox, o_mla, o_diff, mod, brw, gw, gb, wo)
        x = _ffn(l, x, mod, wup, ffn_conv_w, cb, wd)
    return x
```

```python
import functools
import math

import jax
import jax.numpy as jnp
from jax import lax
from jax.experimental import pallas as pl
from jax.experimental.pallas import tpu as pltpu

F32 = jnp.float32
BF16 = jnp.bfloat16

D_MODEL = 1024
D_HEAD = 64
N_HEADS = 4
MIX_WIDTH = 256
NSA_ROT = 16
CMP_BLOCK = 32
CMP_STRIDE = 16
SEL_BLOCK = 64
SEL_TOPN = 16
WINDOW = 512
MLA_Q_RANK = 256
MLA_KV_RANK = 128
MLA_NOPE = 64
MLA_ROPE = 32
MLA_QK = MLA_ROPE + MLA_NOPE
DIFF_QK = 32
DIFF_ROT = 8
D_FF = 2816
ROPE_THETA = 500000.0
EPS = 1e-6
FORCE_SCORE = 1e4
NEG_BIG = -1e30

LANES = 128
VMEM_LIMIT = 56 * 1024 * 1024

C_NQ, C_NKSW, C_NKVC, C_NG = 0, 256, 384, 512
C_FQ, C_FK, C_FF = 640, 896, 1152
C_MCQ, C_MCKV, C_MKR = 1280, 1536, 1664
C_DQ, C_DK = 1792, 2048
W_IN_PAD = 2304
R_NV, R_FV, R_DV = 0, 128, 384
W_VT_ROWS = 640

V_NQ, V_NK, V_FQ, V_FK, V_MQ, V_MK, V_DQ, V_DK, V_FB, V_CQG, V_CKVG = range(11)
N_VEC = 16

TS_PROJ = 512
TS_DENSE = 512
TQ_ATT = 512
TQ_NSA = 256
TK_NSA = 256
FF_CHUNK = 1408
LOG2E = math.log2(math.e)


def _dot(a, b):
    return jnp.dot(a, b, preferred_element_type=F32)


def _dot_nt(a, b):
    return lax.dot_general(a, b, (((1,), (1,)), ((), ())), preferred_element_type=F32)


def _split3(x):
    hi = x.astype(BF16)
    r = x - hi.astype(F32)
    mid = r.astype(BF16)
    lo = (r - mid.astype(F32)).astype(BF16)
    return hi, mid, lo


def _rms_rows(x):
    return x * lax.rsqrt(jnp.mean(x * x, axis=-1, keepdims=True) + EPS)


def _modulated_norm(x, mod_ref, r):
    return _rms_rows(x) * (1.0 + mod_ref[0, 0, r + 1:r + 2, :]) + mod_ref[0, 0, r:r + 1, :]


def _group_rms(x, bd, inv_n):
    ss = _dot((x * x).astype(BF16), bd)
    return x * lax.rsqrt(ss * inv_n + EPS)


def _rope(xc, tab_ref, cfg, half):
    c = tab_ref[3 * cfg, 0]
    sa = tab_ref[3 * cfg + 1, 0]
    sb = tab_ref[3 * cfg + 2, 0]
    return xc * c + pltpu.roll(xc, LANES - half, 1) * sa + pltpu.roll(xc, half, 1) * sb


def _log_sigmoid(x):
    return jnp.minimum(x, 0.0) - jnp.log1p(jnp.exp(-jnp.abs(x)))


def _sigmoid(x):
    return 1.0 / (1.0 + jnp.exp(-x))


def _silu(x):
    return x * _sigmoid(x)


def _softmax_step(st, v1t, m, acc):
    m_new = jnp.maximum(m, jnp.max(st, axis=0, keepdims=True))
    p = jnp.exp2(st - m_new).astype(BF16)
    return m_new, jnp.exp2(m - m_new) * acc + _dot(v1t, p)


def _softmax_finish(acc):
    return acc[:D_HEAD] / acc[D_HEAD:D_HEAD + 1]


def _staggered(n, n_tiles, norm_next, process_prev, init):
    @pl.when(n == 0)
    def _():
        init()
        norm_next()

    @pl.when((n > 0) & (n < n_tiles))
    def _():
        norm_next()
        process_prev()

    @pl.when(n == n_tiles)
    def _():
        process_prev()


def _mod_kernel(c_ref, w_ref, b_ref, o_ref):
    c = c_ref[...]
    a = _silu(c).astype(BF16)
    o_ref[0] = _dot(a, w_ref[0].astype(BF16)) + b_ref[0]


def _modulation(c, ada_w, ada_b):
    L, D, N = ada_w.shape
    B = c.shape[0]
    tn = 1536
    return pl.pallas_call(
        _mod_kernel,
        out_shape=jax.ShapeDtypeStruct((L, B, N), F32),
        grid=(L, N // tn),
        in_specs=[pl.BlockSpec((B, D), lambda l, n: (0, 0)),
                  pl.BlockSpec((1, D, tn), lambda l, n: (l, 0, n)),
                  pl.BlockSpec((1, 1, tn), lambda l, n: (l, 0, n))],
        out_specs=pl.BlockSpec((1, B, tn), lambda l, n: (l, 0, n)),
        compiler_params=pltpu.CompilerParams(vmem_limit_bytes=VMEM_LIMIT),
        name="adaln_mod",
    )(c, ada_w, ada_b.reshape(L, 1, N))


def _rope_tab_kernel(pos_ref, cst_ref, o_ref):
    pos = pos_ref[0].astype(F32)
    for cfg in range(3):
        ang = pos * cst_ref[cfg:cfg + 1, :]
        cs = jnp.cos(ang)
        sn = jnp.sin(ang)
        o_ref[3 * cfg, 0] = cs
        o_ref[3 * cfg + 1, 0] = sn * cst_ref[3 + cfg:4 + cfg, :]
        o_ref[3 * cfg + 2, 0] = sn * cst_ref[6 + cfg:7 + cfg, :]


def _rope_consts():
    rows_f, rows_a, rows_b = [], [], []
    for group, n_rot in ((D_HEAD, NSA_ROT), (LANES, MLA_ROPE), (DIFF_QK, DIFF_ROT)):
        half = n_rot // 2
        inv_freq = ROPE_THETA ** (-jnp.arange(half, dtype=F32) / half)
        d = jnp.arange(LANES) % group
        f = jnp.where(d < n_rot, inv_freq[d % half], 0.0)
        rows_f.append(f)
        rows_a.append(jnp.where(d < half, -1.0, 0.0))
        rows_b.append(jnp.where((d >= half) & (d < n_rot), 1.0, 0.0))
    return jnp.stack(rows_f + rows_a + rows_b).astype(F32)


def _rope_tables(positions):
    B, S = positions.shape
    ts = 512
    return pl.pallas_call(
        _rope_tab_kernel,
        out_shape=jax.ShapeDtypeStruct((9, B, S, LANES), F32),
        grid=(B, S // ts),
        in_specs=[pl.BlockSpec((1, ts, 1), lambda b, i: (b, i, 0)),
                  pl.BlockSpec((9, LANES), lambda b, i: (0, 0))],
        out_specs=pl.BlockSpec((9, 1, ts, LANES), lambda b, i: (0, b, i, 0)),
        name="rope_tables",
    )(positions.reshape(B, S, 1), _rope_consts())


def _proj_kernel(n_s, xn_ref, modn_ref, tab_ref, w_ref, wvt_ref, vec_ref, bd_ref, tri_ref, wuq_ref, wuk_ref,
                 wuvt_ref, hbf_ref, nq_ref, nk_ref, nv_ref, nkc_ref, nvc_ref, ng_ref,
                 fq_ref, fk_ref, fv_ref, fkb_ref,
                 mq_ref, mk_ref, mv_ref, dq_ref, dk_ref, dv_ref, hb_scr, carry_ref):
    n = pl.program_id(0)
    ts = xn_ref.shape[1]

    def norm_next():
        hb_scr[n % 2] = _modulated_norm(xn_ref[0], modn_ref, 0).astype(BF16)

    def init():
        carry_ref[...] = jnp.zeros_like(carry_ref)

    def process_prev():
        i = (n - 1) % n_s
        keep = jnp.where(i == 0, 0.0, 1.0)
        hb = hb_scr[(n + 1) % 2]
        hbf_ref[0] = hb

        def proj(c0, width):
            return _dot(hb, w_ref[0, :, c0:c0 + width])

        def vec(r, width=LANES):
            return vec_ref[0, r:r + 1, 0:width]

        bd64, bd128, bd32 = bd_ref[0], bd_ref[1], bd_ref[2]
        row = lax.broadcasted_iota(jnp.int32, (ts, 1), 0)
        lane = lax.broadcasted_iota(jnp.int32, (1, LANES), 1)

        def head_pair(seg, gain_row, bd, inv_n, cfg, half, out_ref):
            yn = _group_rms(seg, bd, inv_n) * vec(gain_row, 256)
            for c in range(2):
                y = yn[:, c * LANES:(c + 1) * LANES]
                if cfg is not None:
                    y = _rope(y, tab_ref, cfg, half)
                yb = y.astype(BF16)
                out_ref[0, 2 * c] = yb[:, :D_HEAD]
                out_ref[0, 2 * c + 1] = yb[:, D_HEAD:]

        ones_rows = jnp.where(lax.broadcasted_iota(jnp.int32, (D_HEAD, ts), 0) == 0, 1.0, 0.0).astype(BF16)

        def value_store(vt, out_ref):
            for hd in range(vt.shape[0] // D_HEAD):
                out_ref[0, hd] = jnp.concatenate([vt[hd * D_HEAD:(hd + 1) * D_HEAD].astype(BF16), ones_rows], axis=0)

        head_pair(proj(C_NQ, 256), V_NQ, bd64, 1.0 / D_HEAD, 0, NSA_ROT // 2, nq_ref)
        ksw = _group_rms(proj(C_NKSW, LANES), bd64[0:LANES, 0:LANES], 1.0 / D_HEAD) * vec(V_NK)
        ksw = _rope(ksw, tab_ref, 0, NSA_ROT // 2)
        blk_hot = jnp.where(lane - D_HEAD == (i * ts + row) // SEL_BLOCK, 1.0, 0.0)
        nk_ref[0, 0] = jnp.where(lane < D_HEAD, ksw, blk_hot).astype(BF16)
        nk_ref[0, 1] = jnp.where(lane < D_HEAD, pltpu.roll(ksw, D_HEAD, 1), 0.0).astype(BF16)
        kvc = proj(C_NKVC, LANES)
        nkc_ref[0] = kvc[:, :D_HEAD]
        nvc_ref[0] = kvc[:, D_HEAD:]
        ng_ref[0] = _sigmoid(proj(C_NG, LANES))

        head_pair(proj(C_FQ, 256), V_FQ, bd64, 1.0 / D_HEAD, None, 0, fq_ref)
        head_pair(proj(C_FK, 256), V_FK, bd64, 1.0 / D_HEAD, None, 0, fk_ref)

        lf = _log_sigmoid(proj(C_FF, LANES) + vec(V_FB))
        tri = tri_ref[...]
        p0, p1, p2 = _split3(lf)
        cum = _dot(tri, p0) + _dot(tri, p1) + _dot(tri, p2) + carry_ref[0:1, :] * keep
        carry_ref[0:1, :] = cum[ts - 1:ts, :]
        fkb_ref[0] = cum * (-LOG2E)

        cq = proj(C_MCQ, MLA_Q_RANK)
        cqn = (_rms_rows(cq) * vec(V_CQG, MLA_Q_RANK)).astype(BF16)
        qm = _dot(cqn, wuq_ref[0])
        ckv = proj(C_MCKV, MLA_KV_RANK)
        ckvn = (_rms_rows(ckv) * vec(V_CKVG)).astype(BF16)
        kk = _dot(ckvn, wuk_ref[0])
        vvt = _dot_nt(wuvt_ref[0], ckvn)
        kr = proj(C_MKR, LANES)
        kr2 = jnp.concatenate([kr, kr], axis=1)
        for pair in range(N_HEADS // 2):
            sl = slice(pair * 256, (pair + 1) * 256)
            yq = _group_rms(qm[:, sl], bd128, 1.0 / MLA_QK) * vec(V_MQ, 256)
            yk = _group_rms(kk[:, sl] + kr2, bd128, 1.0 / MLA_QK) * vec(V_MK, 256)
            for c in range(2):
                cs = slice(c * LANES, (c + 1) * LANES)
                mq_ref[0, 2 * pair + c] = _rope(yq[:, cs], tab_ref, 1, MLA_ROPE // 2).astype(BF16)
                mk_ref[0, 2 * pair + c] = _rope(yk[:, cs], tab_ref, 1, MLA_ROPE // 2).astype(BF16)

        head_pair(proj(C_DQ, 256), V_DQ, bd32, 1.0 / DIFF_QK, 2, DIFF_ROT // 2, dq_ref)
        head_pair(proj(C_DK, 256), V_DK, bd32, 1.0 / DIFF_QK, 2, DIFF_ROT // 2, dk_ref)

        value_store(vvt, mv_ref)
        vt_all = _dot_nt(wvt_ref[0], hb)
        value_store(vt_all[R_NV:R_FV], nv_ref)
        value_store(vt_all[R_FV:R_DV], fv_ref)
        value_store(vt_all[R_DV:W_VT_ROWS], dv_ref)

    _staggered(n, pl.num_programs(0) - 1, norm_next, process_prev, init)


def _projection(l, x, mod, tab, w1, wvt, vecs, bd, tri, wuq, wuk, wuvt):
    B, S, D = x.shape
    ts = TS_PROJ
    H = N_HEADS
    n_s = S // ts
    n_tiles = B * n_s
    nxt = lambda n: jnp.minimum(n, n_tiles - 1)
    prv = lambda n: jnp.maximum(n - 1, 0)
    hm = lambda d, dt: jax.ShapeDtypeStruct((B, H, S, d), dt)
    hm_spec = lambda nh, d: pl.BlockSpec((1, nh, ts, d), lambda n: (prv(n) // n_s, 0, prv(n) % n_s, 0))
    row_spec = lambda d: pl.BlockSpec((1, ts, d), lambda n: (prv(n) // n_s, prv(n) % n_s, 0))
    vt = lambda nh: jax.ShapeDtypeStruct((B, nh, LANES, S), BF16)
    vt_spec = lambda nh: pl.BlockSpec((1, nh, LANES, ts), lambda n: (prv(n) // n_s, 0, 0, prv(n) % n_s))
    out_shape = [
        jax.ShapeDtypeStruct((B, S, D), BF16),
        hm(D_HEAD, BF16),
        jax.ShapeDtypeStruct((B, 2, S, LANES), BF16),
        vt(2),
        jax.ShapeDtypeStruct((B, S, D_HEAD), F32),
        jax.ShapeDtypeStruct((B, S, D_HEAD), F32),
        jax.ShapeDtypeStruct((B, S, LANES), F32),
        hm(D_HEAD, BF16), hm(D_HEAD, BF16), vt(H),
        jax.ShapeDtypeStruct((B, S, LANES), F32),
        hm(LANES, BF16), hm(LANES, BF16), vt(H),
        hm(D_HEAD, BF16), hm(D_HEAD, BF16), vt(H),
    ]
    out_specs = [
        row_spec(D), hm_spec(H, D_HEAD), hm_spec(2, LANES), vt_spec(2),
        row_spec(D_HEAD), row_spec(D_HEAD), row_spec(LANES),
        hm_spec(H, D_HEAD), hm_spec(H, D_HEAD), vt_spec(H),
        row_spec(LANES),
        hm_spec(H, LANES), hm_spec(H, LANES), vt_spec(H),
        hm_spec(H, D_HEAD), hm_spec(H, D_HEAD), vt_spec(H),
    ]
    in_specs = [
        pl.BlockSpec((1, ts, D), lambda n: (nxt(n) // n_s, nxt(n) % n_s, 0)),
        pl.BlockSpec((1, 1, 6, D), lambda n: (l, nxt(n) // n_s, 0, 0)),
        pl.BlockSpec((9, 1, ts, LANES), lambda n: (0, prv(n) // n_s, prv(n) % n_s, 0)),
        pl.BlockSpec((1, D, W_IN_PAD), lambda n: (l, 0, 0)),
        pl.BlockSpec((1, W_VT_ROWS, D), lambda n: (l, 0, 0)),
        pl.BlockSpec((1, N_VEC, 256), lambda n: (l, 0, 0)),
        pl.BlockSpec((3, 256, 256), lambda n: (0, 0, 0)),
        pl.BlockSpec((ts, ts), lambda n: (0, 0)),
        pl.BlockSpec((1, MLA_Q_RANK, 512), lambda n: (l, 0, 0)),
        pl.BlockSpec((1, MLA_KV_RANK, 512), lambda n: (l, 0, 0)),
        pl.BlockSpec((1, 256, MLA_KV_RANK), lambda n: (l, 0, 0)),
    ]
    return pl.pallas_call(
        functools.partial(_proj_kernel, n_s),
        out_shape=out_shape,
        grid=(n_tiles + 1,),
        in_specs=in_specs,
        out_specs=out_specs,
        scratch_shapes=[pltpu.VMEM((2, ts, D), BF16), pltpu.VMEM((8, LANES), F32)],
        compiler_params=pltpu.CompilerParams(dimension_semantics=("arbitrary",),
                                             vmem_limit_bytes=VMEM_LIMIT),
        name="in_proj",
    )(x, mod, tab, w1, wvt, vecs, bd, tri, wuq, wuk, wuvt)


def _cmp_kernel(kc_ref, vc_ref, pe_ref, w1_ref, w2_ref, g_ref, ko_ref, vo_ref):
    half = CMP_STRIDE * D_HEAD
    n = kc_ref.shape[1]

    def hidden(x2, j):
        xa = _dot((x2 + pe_ref[0, j:j + 1, 0:half]).astype(BF16), w1_ref[0, j, 0:half, :])
        xb = _dot((x2 + pe_ref[0, j:j + 1, half:2 * half]).astype(BF16), w1_ref[0, j, half:2 * half, :])
        return _silu(xa + pltpu.roll(xb, n - 1, 0)).astype(BF16)

    kcmp = _rms_rows(_dot(hidden(kc_ref[0], 0), w2_ref[0, 0])) * g_ref[0, 0:1, 0:D_HEAD]
    ko_ref[0] = kcmp.astype(BF16)
    vo_ref[0] = _dot_nt(w2_ref[0, 1], hidden(vc_ref[0], 1)).astype(BF16)


def _compress(l, kc2, vc2, pe, w1, w2, g):
    B, n, wide = kc2.shape
    return pl.pallas_call(
        _cmp_kernel,
        out_shape=[jax.ShapeDtypeStruct((B, n, D_HEAD), BF16), jax.ShapeDtypeStruct((B, D_HEAD, n), BF16)],
        grid=(B,),
        in_specs=[pl.BlockSpec((1, n, wide), lambda b: (b, 0, 0)),
                  pl.BlockSpec((1, n, wide), lambda b: (b, 0, 0)),
                  pl.BlockSpec((1, 2, 2 * wide), lambda b: (l, 0, 0)),
                  pl.BlockSpec((1, 2, 2 * wide, D_HEAD), lambda b: (l, 0, 0, 0)),
                  pl.BlockSpec((1, 2, D_HEAD, D_HEAD), lambda b: (l, 0, 0, 0)),
                  pl.BlockSpec((1, 1, LANES), lambda b: (l, 0, 0))],
        out_specs=[pl.BlockSpec((1, n, D_HEAD), lambda b: (b, 0, 0)),
                   pl.BlockSpec((1, D_HEAD, n), lambda b: (b, 0, 0))],
        name="nsa_compress",
    )(kc2, vc2, pe, w1, w2, g)


def _nsa_kernel(q_ref, kc_ref, vc_ref, k_ref, v_ref, g_ref, ov_ref, o_ref):
    i = pl.program_id(1)
    tq, tk = TQ_NSA, TK_NSA
    H = N_HEADS
    rows = H * tq
    n_blk = ov_ref.shape[0]
    q = q_ref[0].reshape(rows, D_HEAD)
    t4 = i * tq + lax.broadcasted_iota(jnp.int32, (1, rows), 1) % tq

    nc = kc_ref.shape[1]
    sc = _dot_nt(kc_ref[0], q)
    cend = lax.broadcasted_iota(jnp.int32, (nc, 1), 0) * CMP_STRIDE + (CMP_BLOCK - 1)
    sc = jnp.where(cend <= t4, sc, NEG_BIG)
    e = jnp.exp2(sc - jnp.max(sc, axis=0, keepdims=True))
    p = e / jnp.sum(e, axis=0, keepdims=True)
    p = jnp.where(t4 >= CMP_BLOCK - 1, p, 0.0)
    o_cmp = _dot(vc_ref[0], p.astype(BF16))
    psum = p[:, 0:tq] + p[:, tq:2 * tq] + p[:, 2 * tq:3 * tq] + p[:, 3 * tq:4 * tq]
    p0, p1, p2 = _split3(psum)
    ov = ov_ref[...]
    imp = _dot(ov, p0) + _dot(ov, p1) + _dot(ov, p2)

    blk = lax.broadcasted_iota(jnp.int32, (n_blk, 1), 0)
    tl = i * tq + lax.broadcasted_iota(jnp.int32, (1, tq), 1)
    cur = tl // SEL_BLOCK
    forced = (blk == 0) | (blk == cur) | (blk == cur - 1)
    score = jnp.where(blk * SEL_BLOCK > tl, -1.0, jnp.where(forced, FORCE_SCORE, imp))
    rank = jnp.zeros((n_blk, tq), F32)
    for r in range(n_blk):
        other = score[r:r + 1, :]
        ahead = (other > score) | ((other == score) & (blk > r))
        rank = rank + jnp.where(ahead, 1.0, 0.0)
    drop_t = jnp.where(rank < float(min(SEL_TOPN, n_blk)), 0.0, NEG_BIG)
    drop = jnp.concatenate([drop_t, jnp.zeros((LANES - n_blk, tq), F32)], axis=0).T.astype(BF16)
    qx = jnp.concatenate([q, jnp.concatenate([drop[:, 0:D_HEAD]] * H, axis=0)], axis=1)

    def slc_scores(j):
        return _dot_nt(k_ref[0, 0, pl.ds(j * tk, tk), :], qx)

    init = (jnp.full((1, rows), NEG_BIG, F32), jnp.zeros((LANES, rows), F32))

    def far_step(j, carry):
        return _softmax_step(slc_scores(j), v_ref[0, 0, :, pl.ds(j * tk, tk)], *carry)

    n_near = WINDOW // tk
    kl = lax.broadcasted_iota(jnp.int32, (tk, 1), 0)
    rl = lax.broadcasted_iota(jnp.int32, (1, rows), 1) % tq

    def near_step(rel, carry):
        j = i - rel
        s_s = slc_scores(j)
        s_w = _dot_nt(k_ref[0, 1, pl.ds(j * tk, tk), :], qx)
        if rel == 0:
            s_s = jnp.where(kl <= rl, s_s, NEG_BIG)
            s_w = jnp.where(kl <= rl, s_w, NEG_BIG)
        elif rel == n_near:
            s_w = jnp.where(kl > rl, s_w, NEG_BIG)
        c_s = _softmax_step(s_s, v_ref[0, 0, :, pl.ds(j * tk, tk)], *carry[0])
        c_w = _softmax_step(s_w, v_ref[0, 1, :, pl.ds(j * tk, tk)], *carry[1])
        return c_s, c_w

    lo = jnp.maximum(i - n_near, 0)
    carry = (lax.fori_loop(0, lo, far_step, init), init)
    for rel in range(n_near, 0, -1):
        carry = lax.cond(i >= rel, functools.partial(near_step, rel), lambda c: c, carry)
    (_, a_s), (_, a_w) = near_step(0, carry)
    o_slc = _softmax_finish(a_s)
    o_win = _softmax_finish(a_w)

    gt = g_ref[0].T
    outs = []
    for hd in range(H):
        r = slice(hd * tq, (hd + 1) * tq)
        outs.append(gt[hd:hd + 1] * o_cmp[:, r] + gt[H + hd:H + hd + 1] * o_slc[:, r]
                    + gt[2 * H + hd:2 * H + hd + 1] * o_win[:, r])
    o_ref[0] = jnp.concatenate(outs, axis=0).T.astype(BF16)


def _nsa_attention(q, kcmp, vcmp, ksw, vsw, gates, overlap):
    B, H, S, _ = q.shape
    tq = TQ_NSA
    nc = kcmp.shape[1]
    return pl.pallas_call(
        _nsa_kernel,
        out_shape=jax.ShapeDtypeStruct((B, S, MIX_WIDTH), BF16),
        grid=(B, S // tq),
        in_specs=[pl.BlockSpec((1, H, tq, D_HEAD), lambda b, i: (b, 0, i, 0)),
                  pl.BlockSpec((1, nc, D_HEAD), lambda b, i: (b, 0, 0)),
                  pl.BlockSpec((1, D_HEAD, nc), lambda b, i: (b, 0, 0)),
                  pl.BlockSpec((1, 2, S, LANES), lambda b, i: (b, 0, 0, 0)),
                  pl.BlockSpec((1, 2, LANES, S), lambda b, i: (b, 0, 0, 0)),
                  pl.BlockSpec((1, tq, LANES), lambda b, i: (b, i, 0)),
                  pl.BlockSpec(overlap.shape, lambda b, i: (0, 0))],
        out_specs=pl.BlockSpec((1, tq, MIX_WIDTH), lambda b, i: (b, i, 0)),
        compiler_params=pltpu.CompilerParams(vmem_limit_bytes=VMEM_LIMIT),
        name="nsa_attention",
    )(q, kcmp, vcmp, ksw, vsw, gates, overlap)


def _attn_kernel(mode, lam_init, *refs):
    if mode == "fox":
        q_ref, k_ref, v_ref, kb_ref, o_ref = refs
    elif mode == "diff":
        q_ref, k_ref, v_ref, lam_ref, og_ref, o_ref = refs
    else:
        q_ref, k_ref, v_ref, o_ref = refs
    i = pl.program_id(1)
    t = TQ_ATT
    H = N_HEADS
    qs = []
    for hd in range(H):
        q = q_ref[0, hd]
        if mode == "diff":
            lane = lax.broadcasted_iota(jnp.int32, (1, D_HEAD), 1)
            zero = jnp.zeros_like(q)
            q = jnp.concatenate([jnp.where(lane < DIFF_QK, q, zero), jnp.where(lane >= DIFF_QK, q, zero)], axis=0)
        qs.append(q)
    rows = qs[0].shape[0]

    def step(j, carry, mask=None):
        out = []
        for hd in range(H):
            s = _dot_nt(k_ref[0, hd, pl.ds(j * t, t), :], qs[hd])
            if mode == "fox":
                s = s + kb_ref[0, pl.ds(j * t, t), :][:, hd:hd + 1]
            if mask is not None:
                s = jnp.where(mask, s, NEG_BIG)
            out.append(_softmax_step(s, v_ref[0, hd, :, pl.ds(j * t, t)], *carry[hd]))
        return tuple(out)

    init = tuple((jnp.full((1, rows), NEG_BIG, F32), jnp.zeros((LANES, rows), F32)) for _ in range(H))
    carry = lax.fori_loop(0, i, step, init)
    kk = lax.broadcasted_iota(jnp.int32, (t, 1), 0)
    rr = lax.broadcasted_iota(jnp.int32, (1, rows), 1) % t
    carry = step(i, carry, mask=kk <= rr)

    outs = []
    for hd in range(H):
        o = _softmax_finish(carry[hd][1])
        if mode == "diff":
            lv = lam_ref[0]
            lam = (jnp.exp(jnp.sum(lv[0:1] * lv[1:2], axis=-1, keepdims=True))
                   - jnp.exp(jnp.sum(lv[2:3] * lv[3:4], axis=-1, keepdims=True)) + lam_init)
            o = o[:, 0:t] - lam * o[:, t:2 * t]
            o = o * lax.rsqrt(jnp.mean(o * o, axis=0, keepdims=True) + EPS) * og_ref[0] * (1.0 - lam_init)
        outs.append(o)
    o_ref[0] = jnp.concatenate(outs, axis=0).T.astype(BF16)


def _dense_attention(mode, l, lam_init, q, k, v, *extra):
    B, H, S, dk = q.shape
    t = TQ_ATT
    in_specs = [pl.BlockSpec((1, H, t, dk), lambda b, i: (b, 0, i, 0)),
                pl.BlockSpec((1, H, S, dk), lambda b, i: (b, 0, 0, 0)),
                pl.BlockSpec((1, H, LANES, S), lambda b, i: (b, 0, 0, 0))]
    if mode == "fox":
        in_specs += [pl.BlockSpec((1, S, LANES), lambda b, i: (b, 0, 0))]
    elif mode == "diff":
        in_specs += [pl.BlockSpec((1, 4, DIFF_QK), lambda b, i: (l, 0, 0)),
                     pl.BlockSpec((1, D_HEAD, 1), lambda b, i: (l, 0, 0))]
    return pl.pallas_call(
        functools.partial(_attn_kernel, mode, lam_init),
        out_shape=jax.ShapeDtypeStruct((B, S, MIX_WIDTH), BF16),
        grid=(B, S // t),
        in_specs=in_specs,
        out_specs=pl.BlockSpec((1, t, MIX_WIDTH), lambda b, i: (b, i, 0)),
        compiler_params=pltpu.CompilerParams(vmem_limit_bytes=VMEM_LIMIT),
        name=mode + "_attention",
    )(q, k, v, *extra)


def _merge_kernel(x_ref, h_ref, o0_ref, o1_ref, o2_ref, o3_ref, mod_ref, brw_ref, gw_ref, gb_ref, wo_ref,
                  out_ref):
    hb = h_ref[0]
    merged = None
    for m, o_ref in enumerate((o0_ref, o1_ref, o2_ref, o3_ref)):
        y = _dot(o_ref[0], brw_ref[0, m])
        cols = slice(m * D_MODEL, (m + 1) * D_MODEL)
        gate = _sigmoid(_dot(hb, gw_ref[0, :, cols]) + gb_ref[0, :, cols])
        merged = gate * y if merged is None else merged + gate * y
    out = _dot(merged.astype(BF16), wo_ref[0])
    out_ref[0] = x_ref[0] + mod_ref[0, 0, 2:3, :] * out


def _merge(l, x, hbf, o_nsa, o_fox, o_mla, o_diff, mod, brw, gw, gb, wo):
    B, S, D = x.shape
    ts = TS_DENSE
    row = lambda d: pl.BlockSpec((1, ts, d), lambda b, i: (b, i, 0))
    return pl.pallas_call(
        _merge_kernel,
        out_shape=jax.ShapeDtypeStruct((B, S, D), F32),
        grid=(B, S // ts),
        in_specs=[row(D), row(D), row(MIX_WIDTH), row(MIX_WIDTH), row(MIX_WIDTH), row(MIX_WIDTH),
                  pl.BlockSpec((1, 1, 6, D), lambda b, i: (l, b, 0, 0)),
                  pl.BlockSpec((1, 4, MIX_WIDTH, D), lambda b, i: (l, 0, 0, 0)),
                  pl.BlockSpec((1, D, 4 * D), lambda b, i: (l, 0, 0)),
                  pl.BlockSpec((1, 1, 4 * D), lambda b, i: (l, 0, 0)),
                  pl.BlockSpec((1, D, D), lambda b, i: (l, 0, 0))],
        out_specs=row(D),
        compiler_params=pltpu.CompilerParams(vmem_limit_bytes=VMEM_LIMIT),
        name="merge_out",
    )(x, hbf, o_nsa, o_fox, o_mla, o_diff, mod, brw, gw, gb, wo)


def _ffn_kernel(n_s, xn_ref, modn_ref, xp_ref, modp_ref, wup_ref, cw_ref, cb_ref, wd_ref, out_ref,
                hb_scr, carry_ref):
    n = pl.program_id(0)
    ts = xn_ref.shape[1]

    def norm_next():
        hb_scr[n % 2] = _modulated_norm(xn_ref[0], modn_ref, 3).astype(BF16)

    def init():
        carry_ref[...] = jnp.zeros_like(carry_ref)

    def process_prev():
        keep = jnp.where((n - 1) % n_s == 0, 0.0, 1.0)
        hb = hb_scr[(n + 1) % 2]
        row = lax.broadcasted_iota(jnp.int32, (ts, 1), 0)
        acc = None
        for c in range(D_FF // FF_CHUNK):
            cols = slice(c * FF_CHUNK, (c + 1) * FF_CHUNK)
            g = _dot(hb, wup_ref[0, :, cols])
            v = _dot(hb, wup_ref[0, :, D_FF + c * FF_CHUNK:D_FF + (c + 1) * FF_CHUNK])
            prev = carry_ref[c] * keep
            g1 = jnp.where(row == 0, prev[7:8, :], pltpu.roll(g, 1, 0))
            g2 = jnp.where(row == 0, prev[6:7, :], jnp.where(row == 1, prev[7:8, :], pltpu.roll(g, 2, 0)))
            carry_ref[c] = g[ts - 8:ts, :]
            conv = (cw_ref[0, 0:1, cols] * g2 + cw_ref[0, 1:2, cols] * g1 + cw_ref[0, 2:3, cols] * g
                    + cb_ref[0, :, cols])
            a = (_silu(conv) * v).astype(BF16)
            part = _dot(a, wd_ref[0, cols, :])
            acc = part if acc is None else acc + part
        out_ref[0] = xp_ref[0] + modp_ref[0, 0, 5:6, :] * acc

    _staggered(n, pl.num_programs(0) - 1, norm_next, process_prev, init)


def _ffn(l, x, mod, wup, cw, cb, wd):
    B, S, D = x.shape
    ts = TS_DENSE
    const = pl.Buffered(1)
    n_s = S // ts
    n_tiles = B * n_s
    nxt = lambda n: jnp.minimum(n, n_tiles - 1)
    prv = lambda n: jnp.maximum(n - 1, 0)
    tile = lambda f: pl.BlockSpec((1, ts, D), lambda n: (f(n) // n_s, f(n) % n_s, 0))
    mods = lambda f: pl.BlockSpec((1, 1, 6, D), lambda n: (l, f(n) // n_s, 0, 0))
    return pl.pallas_call(
        functools.partial(_ffn_kernel, n_s),
        out_shape=jax.ShapeDtypeStruct((B, S, D), F32),
        grid=(n_tiles + 1,),
        in_specs=[tile(nxt), mods(nxt), tile(prv), mods(prv),
                  pl.BlockSpec((1, D, 2 * D_FF), lambda n: (l, 0, 0), pipeline_mode=const),
                  pl.BlockSpec((1, 3, D_FF), lambda n: (l, 0, 0)),
                  pl.BlockSpec((1, 1, D_FF), lambda n: (l, 0, 0)),
                  pl.BlockSpec((1, D_FF, D), lambda n: (l, 0, 0), pipeline_mode=const)],
        out_specs=tile(prv),
        scratch_shapes=[pltpu.VMEM((2, ts, D), BF16), pltpu.VMEM((D_FF // FF_CHUNK, 8, FF_CHUNK), F32)],
        compiler_params=pltpu.CompilerParams(dimension_semantics=("arbitrary",),
                                             vmem_limit_bytes=VMEM_LIMIT),
        name="conv_ffn",
    )(x, mod, x, mod, wup, cw, cb, wd)


def _pad_cols(w, n):
    return jnp.pad(w, [(0, 0)] * (w.ndim - 1) + [(0, n - w.shape[-1])])


def _layout_w_in(w_in):
    o = 0
    seg = {}
    for name, n in (("nq", 256), ("nkc", 64), ("nvc", 64), ("nks", 64), ("nvs", 64), ("nkw", 64), ("nvw", 64),
                    ("ng", 12), ("fq", 256), ("fk", 256), ("fv", 256), ("ff", 4),
                    ("mcq", 256), ("mckv", 128), ("mkr", 32), ("dq", 256), ("dk", 256), ("dv", 256)):
        seg[name] = w_in[..., o:o + n]
        o += n
    parts = [seg["nq"], seg["nks"], seg["nkw"], seg["nkc"], seg["nvc"], _pad_cols(seg["ng"], LANES),
             seg["fq"], seg["fk"], _pad_cols(seg["ff"], LANES),
             seg["mcq"], seg["mckv"], _pad_cols(seg["mkr"], LANES),
             seg["dq"], seg["dk"]]
    w1 = jnp.concatenate(parts, axis=-1).astype(BF16)
    wvt = jnp.concatenate([seg["nvs"], seg["nvw"], seg["fv"], seg["dv"]], axis=-1)
    return w1, jnp.swapaxes(wvt, -1, -2).astype(BF16)


def _pack_vecs(nsa_qk_g, fox_qk_g, fox_f_b, mla_cq_g, mla_ckv_g, mla_qk_g, diff_qk_g):
    L = nsa_qk_g.shape[0]
    t4 = lambda g: jnp.tile(g, (1, 256 // g.shape[-1]))
    mla_pad = lambda g: _pad_cols(g, LANES)
    rows = [None] * N_VEC
    rows[V_NQ] = t4(nsa_qk_g[:, 0]) * (LOG2E * D_HEAD ** -0.5)
    rows[V_NK] = jnp.concatenate([nsa_qk_g[:, 2], nsa_qk_g[:, 3]], axis=-1)
    rows[V_FQ] = t4(fox_qk_g[:, 0]) * (LOG2E * D_HEAD ** -0.5)
    rows[V_FK] = t4(fox_qk_g[:, 1])
    rows[V_MQ] = t4(mla_pad(mla_qk_g[:, 0])) * (LOG2E * MLA_QK ** -0.5)
    rows[V_MK] = t4(mla_pad(mla_qk_g[:, 1]))
    rows[V_DQ] = t4(diff_qk_g[:, 0]) * (LOG2E * DIFF_QK ** -0.5)
    rows[V_DK] = t4(diff_qk_g[:, 1])
    rows[V_FB] = fox_f_b
    rows[V_CQG] = mla_cq_g
    rows[V_CKVG] = mla_ckv_g
    rows = [jnp.zeros((L, 256), F32) if r is None else _pad_cols(r.astype(F32), 256) for r in rows]
    return jnp.stack(rows, axis=1)


def _block_diag_ones():
    d = jnp.arange(256)
    mats = [(d[:, None] // g == d[None, :] // g) for g in (D_HEAD, LANES, DIFF_QK)]
    return jnp.stack(mats).astype(BF16)


def _overlap_matrix(n_cmp_rows, n_blk):
    c = jnp.arange(n_cmp_rows)[None, :] * CMP_STRIDE
    b = jnp.arange(n_blk)[:, None] * SEL_BLOCK
    ov = (c < b + SEL_BLOCK) & (c + CMP_BLOCK > b)
    return ov.astype(BF16)


def kernel(x, c, positions, ada_w, ada_b, w_in, nsa_qk_g, nsa_cmp_pe, nsa_cmp_w1, nsa_cmp_w2, fox_qk_g, fox_f_b,
           mla_cq_g, mla_ckv_g, mla_w_uq, mla_w_ukv, mla_qk_g, diff_qk_g, diff_lambda, diff_out_g, br_w, gate_w,
           gate_b, w_out, ffn_w_up, ffn_conv_w, ffn_conv_b, ffn_w_down):
    B, S, D = x.shape
    L = ada_w.shape[0]
    H = N_HEADS
    n_half = S // CMP_STRIDE

    mod = _modulation(c, ada_w, ada_b).reshape(L, B, 6, D)
    tab = _rope_tables(positions)

    w1, wvt = _layout_w_in(w_in)
    vecs = _pack_vecs(nsa_qk_g, fox_qk_g, fox_f_b, mla_cq_g, mla_ckv_g, mla_qk_g, diff_qk_g)
    bd = _block_diag_ones()
    tri = (jnp.arange(TS_PROJ)[None, :] <= jnp.arange(TS_PROJ)[:, None]).astype(BF16)
    overlap = _overlap_matrix(n_half, S // SEL_BLOCK)
    wuq = _pad_cols(mla_w_uq.reshape(L, MLA_Q_RANK, H, MLA_QK), LANES).reshape(L, MLA_Q_RANK, H * LANES)
    ukv = mla_w_ukv.reshape(L, MLA_KV_RANK, H, MLA_NOPE + D_HEAD)
    wuk = jnp.pad(ukv[..., :MLA_NOPE], [(0, 0), (0, 0), (0, 0), (MLA_ROPE, LANES - MLA_QK)])
    wuk = wuk.reshape(L, MLA_KV_RANK, H * LANES).astype(BF16)
    wuvt = jnp.swapaxes(ukv[..., MLA_NOPE:].reshape(L, MLA_KV_RANK, H * D_HEAD), 1, 2).astype(BF16)
    wuq = wuq.astype(BF16)
    pe = nsa_cmp_pe.reshape(L, 2, CMP_BLOCK * D_HEAD)
    cw1 = nsa_cmp_w1.astype(BF16)
    cw2 = jnp.stack([nsa_cmp_w2[:, 0], jnp.swapaxes(nsa_cmp_w2[:, 1], 1, 2)], axis=1).astype(BF16)
    kcg = _pad_cols(nsa_qk_g[:, 1], LANES).reshape(L, 1, LANES)
    brw = br_w.astype(BF16)
    gw = gate_w.astype(BF16)
    gb = gate_b.reshape(L, 1, 4 * D)
    wo = w_out.astype(BF16)
    wup = ffn_w_up.astype(BF16)
    wd = ffn_w_down.astype(BF16)
    cb = ffn_conv_b.reshape(L, 1, D_FF)
    og = diff_out_g.reshape(L, D_HEAD, 1)

    for l in range(L):
        lam_init = 0.8 - 0.6 * math.exp(-0.3 * l)
        (hbf, nq, nk, nv, nkc, nvc, ng, fq, fk, fv, fkb, mq, mk, mv, dq, dk, dv) = _projection(
            l, x, mod, tab, w1, wvt, vecs, bd, tri, wuq, wuk, wuvt)
        kcmp, vcmp = _compress(l, nkc.reshape(B, n_half, CMP_STRIDE * D_HEAD),
                               nvc.reshape(B, n_half, CMP_STRIDE * D_HEAD), pe, cw1, cw2, kcg)
        o_nsa = _nsa_attention(nq, kcmp, vcmp, nk, nv, ng, overlap)
        o_fox = _dense_attention("fox", l, lam_init, fq, fk, fv, fkb)
        o_mla = _dense_attention("mla", l, lam_init, mq, mk, mv)
        o_diff = _dense_attention("diff", l, lam_init, dq, dk, dv, diff_lambda, og)
        x = _merge(l, x, hbf, o_nsa, o_fox, o_mla, o_diff, mod, brw, gw, gb, wo)
        x = _ffn(l, x, mod, wup, ffn_conv_w, cb, wd)
    return x
```

```python
import functools
import math

import jax
import jax.numpy as jnp
import numpy as np
from jax import lax
from jax.experimental import pallas as pl
from jax.experimental.pallas import tpu as pltpu

F32 = jnp.float32
BF16 = jnp.bfloat16

D_MODEL = 1024
D_HEAD = 64
N_HEADS = 4
MIX_WIDTH = 256
NSA_ROT = 16
CMP_BLOCK = 32
CMP_STRIDE = 16
SEL_BLOCK = 64
SEL_TOPN = 16
WINDOW = 512
MLA_Q_RANK = 256
MLA_KV_RANK = 128
MLA_NOPE = 64
MLA_ROPE = 32
MLA_QK = MLA_ROPE + MLA_NOPE
DIFF_QK = 32
DIFF_ROT = 8
D_FF = 2816
ROPE_THETA = 500000.0
EPS = 1e-6
FORCE_SCORE = 1e4
NEG_BIG = -1e30

LANES = 128
VMEM_LIMIT = 56 * 1024 * 1024

C_NQ, C_NKSW, C_NKVC, C_NG = 0, 256, 384, 512
C_FQ, C_FK, C_FF = 640, 896, 1152
C_MCQ, C_MCKV, C_MKR = 1280, 1536, 1664
C_DQ, C_DK = 1792, 2048
W_IN_PAD = 2304
R_NV, R_FV, R_DV = 0, 128, 384
W_VT_ROWS = 640

V_NQ, V_NK, V_FQ, V_FK, V_MQ, V_MK, V_DQ, V_DK, V_FB, V_CQG, V_CKVG = range(11)
N_VEC = 16

TS_PROJ = 512
TS_DENSE = 512
TQ_ATT = 512
TQ_NSA = 256
TK_NSA = 256
FF_CHUNK = 1408
LOG2E = math.log2(math.e)


def _dot(a, b):
    return jnp.dot(a, b, preferred_element_type=F32)


def _dot_nt(a, b):
    return lax.dot_general(a, b, (((1,), (1,)), ((), ())), preferred_element_type=F32)


def _split3(x):
    hi = x.astype(BF16)
    r = x - hi.astype(F32)
    mid = r.astype(BF16)
    lo = (r - mid.astype(F32)).astype(BF16)
    return hi, mid, lo


def _rms_rows(x):
    return x * lax.rsqrt(jnp.mean(x * x, axis=-1, keepdims=True) + EPS)


def _modulated_norm(x, mod_ref, r):
    return _rms_rows(x) * (1.0 + mod_ref[0, 0, r + 1:r + 2, :]) + mod_ref[0, 0, r:r + 1, :]


def _group_rms(x, bd, inv_n):
    ss = _dot((x * x).astype(BF16), bd)
    return x * lax.rsqrt(ss * inv_n + EPS)


def _rope(xc, tab_ref, cfg, half):
    c = tab_ref[3 * cfg, 0]
    sa = tab_ref[3 * cfg + 1, 0]
    sb = tab_ref[3 * cfg + 2, 0]
    return xc * c + pltpu.roll(xc, LANES - half, 1) * sa + pltpu.roll(xc, half, 1) * sb


def _log_sigmoid(x):
    return jnp.minimum(x, 0.0) - jnp.log1p(jnp.exp(-jnp.abs(x)))


def _sigmoid(x):
    return 1.0 / (1.0 + jnp.exp(-x))


def _silu(x):
    return x * _sigmoid(x)


def _softmax_step(st, v1t, m, acc):
    m_new = jnp.maximum(m, jnp.max(st, axis=0, keepdims=True))
    p = jnp.exp2(st - m_new).astype(BF16)
    return m_new, jnp.exp2(m - m_new) * acc + _dot(v1t, p)


def _softmax_finish(acc):
    return acc[:D_HEAD] / acc[D_HEAD:D_HEAD + 1]


def _staggered(n, n_tiles, norm_next, process_prev, init):
    @pl.when(n == 0)
    def _():
        init()
        norm_next()

    @pl.when((n > 0) & (n < n_tiles))
    def _():
        norm_next()
        process_prev()

    @pl.when(n == n_tiles)
    def _():
        process_prev()


def _mod_kernel(c_ref, w_ref, b_ref, o_ref):
    c = c_ref[...]
    a = _silu(c).astype(BF16)
    o_ref[0] = _dot(a, w_ref[0].astype(BF16)) + b_ref[0]


def _modulation(c, ada_w, ada_b):
    L, D, N = ada_w.shape
    B = c.shape[0]
    tn = 1536
    return pl.pallas_call(
        _mod_kernel,
        out_shape=jax.ShapeDtypeStruct((L, B, N), F32),
        grid=(L, N // tn),
        in_specs=[pl.BlockSpec((B, D), lambda l, n: (0, 0)),
                  pl.BlockSpec((1, D, tn), lambda l, n: (l, 0, n)),
                  pl.BlockSpec((1, 1, tn), lambda l, n: (l, 0, n))],
        out_specs=pl.BlockSpec((1, B, tn), lambda l, n: (l, 0, n)),
        compiler_params=pltpu.CompilerParams(vmem_limit_bytes=VMEM_LIMIT),
        name="adaln_mod",
    )(c, ada_w, ada_b.reshape(L, 1, N))


_ROPE_CFGS = ((D_HEAD, NSA_ROT, 32), (LANES, MLA_ROPE, 0), (DIFF_QK, DIFF_ROT, 48))


def _rope_tab_kernel(pos_ref, frq_ref, o_ref):
    ang = pos_ref[0].astype(F32) * frq_ref[...]
    cs = jnp.cos(ang)
    sn = jnp.sin(ang)
    lane = lax.broadcasted_iota(jnp.int32, (1, LANES), 1)
    for cfg, (group, n_rot, src) in enumerate(_ROPE_CFGS):
        def spread(x):
            out = None
            for gi in range(LANES // group):
                shift = (gi * group - src) % LANES
                xs = x if shift == 0 else pltpu.roll(x, shift, 1)
                out = xs if out is None else jnp.where(lane >= gi * group, xs, out)
            return out

        d = lane % group
        c = spread(cs)
        s = spread(sn)
        o_ref[3 * cfg, 0] = jnp.where(d < n_rot, c, 1.0)
        o_ref[3 * cfg + 1, 0] = jnp.where(d < n_rot // 2, -s, 0.0)
        o_ref[3 * cfg + 2, 0] = jnp.where((d >= n_rot // 2) & (d < n_rot), s, 0.0)


def _rope_consts():
    row = jnp.zeros((LANES,), F32)
    for group, n_rot, src in _ROPE_CFGS:
        half = n_rot // 2
        inv_freq = ROPE_THETA ** (-jnp.arange(half, dtype=F32) / half)
        row = row.at[src:src + n_rot].set(jnp.tile(inv_freq, 2))
    return row.reshape(1, LANES)


def _rope_tables(positions):
    B, S = positions.shape
    ts = 512
    return pl.pallas_call(
        _rope_tab_kernel,
        out_shape=jax.ShapeDtypeStruct((9, B, S, LANES), F32),
        grid=(B, S // ts),
        in_specs=[pl.BlockSpec((1, ts, 1), lambda b, i: (b, i, 0)),
                  pl.BlockSpec((1, LANES), lambda b, i: (0, 0))],
        out_specs=pl.BlockSpec((9, 1, ts, LANES), lambda b, i: (0, b, i, 0)),
        name="rope_tables",
    )(positions.reshape(B, S, 1), _rope_consts())


def _proj_kernel(n_s, xn_ref, modn_ref, tab_ref, w_ref, wvt_ref, vec_ref, bd_ref, tri_ref, wuq_ref, wuk_ref,
                 wuvt_ref, hbf_ref, nq_ref, nk_ref, nv_ref, nkc_ref, nvc_ref, ng_ref,
                 fq_ref, fk_ref, fv_ref, fkb_ref,
                 mq_ref, mk_ref, mv_ref, dq_ref, dk_ref, dv_ref, hb_scr, carry_ref):
    n = pl.program_id(0)
    ts = xn_ref.shape[1]

    def norm_next():
        hb_scr[n % 2] = _modulated_norm(xn_ref[0], modn_ref, 0).astype(BF16)

    def init():
        carry_ref[...] = jnp.zeros_like(carry_ref)

    def process_prev():
        i = (n - 1) % n_s
        keep = jnp.where(i == 0, 0.0, 1.0)
        hb = hb_scr[(n + 1) % 2]
        hbf_ref[0] = hb

        def proj(c0, width):
            return _dot(hb, w_ref[0, :, c0:c0 + width])

        def vec(r, width=LANES):
            return vec_ref[0, r:r + 1, 0:width]

        bd64, bd128, bd32 = bd_ref[0], bd_ref[1], bd_ref[2]
        row = lax.broadcasted_iota(jnp.int32, (ts, 1), 0)
        lane = lax.broadcasted_iota(jnp.int32, (1, LANES), 1)

        def head_pair(seg, gain_row, bd, inv_n, cfg, half, out_ref):
            yn = _group_rms(seg, bd, inv_n) * vec(gain_row, 256)
            for c in range(2):
                y = yn[:, c * LANES:(c + 1) * LANES]
                if cfg is not None:
                    y = _rope(y, tab_ref, cfg, half)
                yb = y.astype(BF16)
                out_ref[0, 2 * c] = yb[:, :D_HEAD]
                out_ref[0, 2 * c + 1] = yb[:, D_HEAD:]

        ones_rows = jnp.where(lax.broadcasted_iota(jnp.int32, (D_HEAD, ts), 0) == 0, 1.0, 0.0).astype(BF16)

        def value_store(vt, out_ref):
            for hd in range(vt.shape[0] // D_HEAD):
                out_ref[0, hd] = jnp.concatenate([vt[hd * D_HEAD:(hd + 1) * D_HEAD].astype(BF16), ones_rows], axis=0)

        head_pair(proj(C_NQ, 256), V_NQ, bd64, 1.0 / D_HEAD, 0, NSA_ROT // 2, nq_ref)
        ksw = _group_rms(proj(C_NKSW, LANES), bd64[0:LANES, 0:LANES], 1.0 / D_HEAD) * vec(V_NK)
        ksw = _rope(ksw, tab_ref, 0, NSA_ROT // 2)
        blk_hot = jnp.where(lane - D_HEAD == (i * ts + row) // SEL_BLOCK, 1.0, 0.0)
        nk_ref[0, 0] = jnp.where(lane < D_HEAD, ksw, blk_hot).astype(BF16)
        nk_ref[0, 1] = jnp.where(lane < D_HEAD, pltpu.roll(ksw, D_HEAD, 1), 0.0).astype(BF16)
        kvc = proj(C_NKVC, LANES)
        nkc_ref[0] = kvc[:, :D_HEAD]
        nvc_ref[0] = kvc[:, D_HEAD:]
        ng_ref[0] = _sigmoid(proj(C_NG, LANES))

        head_pair(proj(C_FQ, 256), V_FQ, bd64, 1.0 / D_HEAD, None, 0, fq_ref)
        head_pair(proj(C_FK, 256), V_FK, bd64, 1.0 / D_HEAD, None, 0, fk_ref)

        lf = _log_sigmoid(proj(C_FF, LANES) + vec(V_FB))
        tri = tri_ref[...]
        p0, p1, p2 = _split3(lf)
        cum = _dot(tri, p0) + _dot(tri, p1) + _dot(tri, p2) + carry_ref[0:1, :] * keep
        carry_ref[0:1, :] = cum[ts - 1:ts, :]
        fkb_ref[0] = cum * (-LOG2E)

        cq = proj(C_MCQ, MLA_Q_RANK)
        cqn = (_rms_rows(cq) * vec(V_CQG, MLA_Q_RANK)).astype(BF16)
        qm = _dot(cqn, wuq_ref[0])
        ckv = proj(C_MCKV, MLA_KV_RANK)
        ckvn = (_rms_rows(ckv) * vec(V_CKVG)).astype(BF16)
        kk = _dot(ckvn, wuk_ref[0])
        vvt = _dot_nt(wuvt_ref[0], ckvn)
        kr = proj(C_MKR, LANES)
        kr2 = jnp.concatenate([kr, kr], axis=1)
        for pair in range(N_HEADS // 2):
            sl = slice(pair * 256, (pair + 1) * 256)
            yq = _group_rms(qm[:, sl], bd128, 1.0 / MLA_QK) * vec(V_MQ, 256)
            yk = _group_rms(kk[:, sl] + kr2, bd128, 1.0 / MLA_QK) * vec(V_MK, 256)
            for c in range(2):
                cs = slice(c * LANES, (c + 1) * LANES)
                mq_ref[0, 2 * pair + c] = _rope(yq[:, cs], tab_ref, 1, MLA_ROPE // 2).astype(BF16)
                mk_ref[0, 2 * pair + c] = _rope(yk[:, cs], tab_ref, 1, MLA_ROPE // 2).astype(BF16)

        head_pair(proj(C_DQ, 256), V_DQ, bd32, 1.0 / DIFF_QK, 2, DIFF_ROT // 2, dq_ref)
        head_pair(proj(C_DK, 256), V_DK, bd32, 1.0 / DIFF_QK, 2, DIFF_ROT // 2, dk_ref)

        value_store(vvt, mv_ref)
        vt_all = _dot_nt(wvt_ref[0], hb)
        value_store(vt_all[R_NV:R_FV], nv_ref)
        value_store(vt_all[R_FV:R_DV], fv_ref)
        value_store(vt_all[R_DV:W_VT_ROWS], dv_ref)

    _staggered(n, pl.num_programs(0) - 1, norm_next, process_prev, init)


def _projection(l, x, mod, tab, w1, wvt, vecs, bd, tri, wuq, wuk, wuvt):
    B, S, D = x.shape
    ts = TS_PROJ
    H = N_HEADS
    n_s = S // ts
    n_tiles = B * n_s
    nxt = lambda n: jnp.minimum(n, n_tiles - 1)
    prv = lambda n: jnp.maximum(n - 1, 0)
    hm = lambda d, dt: jax.ShapeDtypeStruct((B, H, S, d), dt)
    hm_spec = lambda nh, d: pl.BlockSpec((1, nh, ts, d), lambda n: (prv(n) // n_s, 0, prv(n) % n_s, 0))
    row_spec = lambda d: pl.BlockSpec((1, ts, d), lambda n: (prv(n) // n_s, prv(n) % n_s, 0))
    vt = lambda nh: jax.ShapeDtypeStruct((B, nh, LANES, S), BF16)
    vt_spec = lambda nh: pl.BlockSpec((1, nh, LANES, ts), lambda n: (prv(n) // n_s, 0, 0, prv(n) % n_s))
    out_shape = [
        jax.ShapeDtypeStruct((B, S, D), BF16),
        hm(D_HEAD, BF16),
        jax.ShapeDtypeStruct((B, 2, S, LANES), BF16),
        vt(2),
        jax.ShapeDtypeStruct((B, S, D_HEAD), F32),
        jax.ShapeDtypeStruct((B, S, D_HEAD), F32),
        jax.ShapeDtypeStruct((B, S, LANES), F32),
        hm(D_HEAD, BF16), hm(D_HEAD, BF16), vt(H),
        jax.ShapeDtypeStruct((B, S, LANES), F32),
        hm(LANES, BF16), hm(LANES, BF16), vt(H),
        hm(D_HEAD, BF16), hm(D_HEAD, BF16), vt(H),
    ]
    out_specs = [
        row_spec(D), hm_spec(H, D_HEAD), hm_spec(2, LANES), vt_spec(2),
        row_spec(D_HEAD), row_spec(D_HEAD), row_spec(LANES),
        hm_spec(H, D_HEAD), hm_spec(H, D_HEAD), vt_spec(H),
        row_spec(LANES),
        hm_spec(H, LANES), hm_spec(H, LANES), vt_spec(H),
        hm_spec(H, D_HEAD), hm_spec(H, D_HEAD), vt_spec(H),
    ]
    in_specs = [
        pl.BlockSpec((1, ts, D), lambda n: (nxt(n) // n_s, nxt(n) % n_s, 0)),
        pl.BlockSpec((1, 1, 6, D), lambda n: (l, nxt(n) // n_s, 0, 0)),
        pl.BlockSpec((9, 1, ts, LANES), lambda n: (0, prv(n) // n_s, prv(n) % n_s, 0)),
        pl.BlockSpec((1, D, W_IN_PAD), lambda n: (l, 0, 0)),
        pl.BlockSpec((1, W_VT_ROWS, D), lambda n: (l, 0, 0)),
        pl.BlockSpec((1, N_VEC, 256), lambda n: (l, 0, 0)),
        pl.BlockSpec((3, 256, 256), lambda n: (0, 0, 0)),
        pl.BlockSpec((ts, ts), lambda n: (0, 0)),
        pl.BlockSpec((1, MLA_Q_RANK, 512), lambda n: (l, 0, 0)),
        pl.BlockSpec((1, MLA_KV_RANK, 512), lambda n: (l, 0, 0)),
        pl.BlockSpec((1, 256, MLA_KV_RANK), lambda n: (l, 0, 0)),
    ]
    return pl.pallas_call(
        functools.partial(_proj_kernel, n_s),
        out_shape=out_shape,
        grid=(n_tiles + 1,),
        in_specs=in_specs,
        out_specs=out_specs,
        scratch_shapes=[pltpu.VMEM((2, ts, D), BF16), pltpu.VMEM((8, LANES), F32)],
        compiler_params=pltpu.CompilerParams(dimension_semantics=("arbitrary",),
                                             vmem_limit_bytes=VMEM_LIMIT),
        name="in_proj",
    )(x, mod, tab, w1, wvt, vecs, bd, tri, wuq, wuk, wuvt)


def _cmp_kernel(kc_ref, vc_ref, pe_ref, w1_ref, w2_ref, g_ref, ko_ref, vo_ref, x_scr):
    half = CMP_STRIDE * D_HEAD
    n = kc_ref.shape[1] // CMP_STRIDE

    def hidden(src_ref, j):
        for p in range(CMP_STRIDE):
            x_scr[:, p * D_HEAD:(p + 1) * D_HEAD] = src_ref[0, pl.ds(p, n, stride=CMP_STRIDE), :]
        x2 = x_scr[...]
        xa = _dot((x2 + pe_ref[0, j:j + 1, 0:half]).astype(BF16), w1_ref[0, j, 0:half, :])
        xb = _dot((x2 + pe_ref[0, j:j + 1, half:2 * half]).astype(BF16), w1_ref[0, j, half:2 * half, :])
        return _silu(xa + pltpu.roll(xb, n - 1, 0)).astype(BF16)

    kcmp = _rms_rows(_dot(hidden(kc_ref, 0), w2_ref[0, 0])) * g_ref[0, 0:1, 0:D_HEAD]
    ko_ref[0] = kcmp.astype(BF16)
    vo_ref[0] = _dot_nt(w2_ref[0, 1], hidden(vc_ref, 1)).astype(BF16)


def _compress(l, kc, vc, pe, w1, w2, g):
    B, S, _ = kc.shape
    n = S // CMP_STRIDE
    wide = CMP_STRIDE * D_HEAD
    return pl.pallas_call(
        _cmp_kernel,
        out_shape=[jax.ShapeDtypeStruct((B, n, D_HEAD), BF16), jax.ShapeDtypeStruct((B, D_HEAD, n), BF16)],
        grid=(B,),
        in_specs=[pl.BlockSpec((1, S, D_HEAD), lambda b: (b, 0, 0)),
                  pl.BlockSpec((1, S, D_HEAD), lambda b: (b, 0, 0)),
                  pl.BlockSpec((1, 2, 2 * wide), lambda b: (l, 0, 0)),
                  pl.BlockSpec((1, 2, 2 * wide, D_HEAD), lambda b: (l, 0, 0, 0)),
                  pl.BlockSpec((1, 2, D_HEAD, D_HEAD), lambda b: (l, 0, 0, 0)),
                  pl.BlockSpec((1, 1, LANES), lambda b: (l, 0, 0))],
        out_specs=[pl.BlockSpec((1, n, D_HEAD), lambda b: (b, 0, 0)),
                   pl.BlockSpec((1, D_HEAD, n), lambda b: (b, 0, 0))],
        scratch_shapes=[pltpu.VMEM((n, wide), F32)],
        name="nsa_compress",
    )(kc, vc, pe, w1, w2, g)


def _nsa_kernel(q_ref, kc_ref, vc_ref, k_ref, v_ref, g_ref, ov_ref, o_ref):
    i = pl.program_id(1)
    tq, tk = TQ_NSA, TK_NSA
    H = N_HEADS
    rows = H * tq
    n_blk = ov_ref.shape[0]
    q = q_ref[0].reshape(rows, D_HEAD)
    t4 = i * tq + lax.broadcasted_iota(jnp.int32, (1, rows), 1) % tq

    nc = kc_ref.shape[1]
    sc = _dot_nt(kc_ref[0], q)
    cend = lax.broadcasted_iota(jnp.int32, (nc, 1), 0) * CMP_STRIDE + (CMP_BLOCK - 1)
    sc = jnp.where(cend <= t4, sc, NEG_BIG)
    e = jnp.exp2(sc - jnp.max(sc, axis=0, keepdims=True))
    p = e / jnp.sum(e, axis=0, keepdims=True)
    p = jnp.where(t4 >= CMP_BLOCK - 1, p, 0.0)
    o_cmp = _dot(vc_ref[0], p.astype(BF16))
    psum = p[:, 0:tq] + p[:, tq:2 * tq] + p[:, 2 * tq:3 * tq] + p[:, 3 * tq:4 * tq]
    p0, p1, p2 = _split3(psum)
    ov = ov_ref[...]
    imp = _dot(ov, p0) + _dot(ov, p1) + _dot(ov, p2)

    blk = lax.broadcasted_iota(jnp.int32, (n_blk, 1), 0)
    tl = i * tq + lax.broadcasted_iota(jnp.int32, (1, tq), 1)
    cur = tl // SEL_BLOCK
    forced = (blk == 0) | (blk == cur) | (blk == cur - 1)
    score = jnp.where(blk * SEL_BLOCK > tl, -1.0, jnp.where(forced, FORCE_SCORE, imp))
    rank = jnp.zeros((n_blk, tq), F32)
    for r in range(n_blk):
        other = score[r:r + 1, :]
        ahead = (other > score) | ((other == score) & (blk > r))
        rank = rank + jnp.where(ahead, 1.0, 0.0)
    drop_t = jnp.where(rank < float(min(SEL_TOPN, n_blk)), 0.0, NEG_BIG)
    drop = jnp.concatenate([drop_t, jnp.zeros((LANES - n_blk, tq), F32)], axis=0).T.astype(BF16)
    qx = jnp.concatenate([q, jnp.concatenate([drop[:, 0:D_HEAD]] * H, axis=0)], axis=1)

    def slc_scores(j):
        return _dot_nt(k_ref[0, 0, pl.ds(j * tk, tk), :], qx)

    init = (jnp.full((1, rows), NEG_BIG, F32), jnp.zeros((LANES, rows), F32))

    def far_step(j, carry):
        return _softmax_step(slc_scores(j), v_ref[0, 0, :, pl.ds(j * tk, tk)], *carry)

    n_near = WINDOW // tk
    kl = lax.broadcasted_iota(jnp.int32, (tk, 1), 0)
    rl = lax.broadcasted_iota(jnp.int32, (1, rows), 1) % tq

    def near_step(rel, carry):
        j = i - rel
        s_s = slc_scores(j)
        s_w = _dot_nt(k_ref[0, 1, pl.ds(j * tk, tk), :], qx)
        if rel == 0:
            s_s = jnp.where(kl <= rl, s_s, NEG_BIG)
            s_w = jnp.where(kl <= rl, s_w, NEG_BIG)
        elif rel == n_near:
            s_w = jnp.where(kl > rl, s_w, NEG_BIG)
        c_s = _softmax_step(s_s, v_ref[0, 0, :, pl.ds(j * tk, tk)], *carry[0])
        c_w = _softmax_step(s_w, v_ref[0, 1, :, pl.ds(j * tk, tk)], *carry[1])
        return c_s, c_w

    lo = jnp.maximum(i - n_near, 0)
    carry = (lax.fori_loop(0, lo, far_step, init), init)
    for rel in range(n_near, 0, -1):
        carry = lax.cond(i >= rel, functools.partial(near_step, rel), lambda c: c, carry)
    (_, a_s), (_, a_w) = near_step(0, carry)
    o_slc = _softmax_finish(a_s)
    o_win = _softmax_finish(a_w)

    gt = g_ref[0].T
    outs = []
    for hd in range(H):
        r = slice(hd * tq, (hd + 1) * tq)
        outs.append(gt[hd:hd + 1] * o_cmp[:, r] + gt[H + hd:H + hd + 1] * o_slc[:, r]
                    + gt[2 * H + hd:2 * H + hd + 1] * o_win[:, r])
    o_ref[0] = jnp.concatenate(outs, axis=0).T.astype(BF16)


def _nsa_attention(q, kcmp, vcmp, ksw, vsw, gates, overlap):
    B, H, S, _ = q.shape
    tq = TQ_NSA
    nc = kcmp.shape[1]
    return pl.pallas_call(
        _nsa_kernel,
        out_shape=jax.ShapeDtypeStruct((B, S, MIX_WIDTH), BF16),
        grid=(B, S // tq),
        in_specs=[pl.BlockSpec((1, H, tq, D_HEAD), lambda b, i: (b, 0, i, 0)),
                  pl.BlockSpec((1, nc, D_HEAD), lambda b, i: (b, 0, 0)),
                  pl.BlockSpec((1, D_HEAD, nc), lambda b, i: (b, 0, 0)),
                  pl.BlockSpec((1, 2, S, LANES), lambda b, i: (b, 0, 0, 0)),
                  pl.BlockSpec((1, 2, LANES, S), lambda b, i: (b, 0, 0, 0)),
                  pl.BlockSpec((1, tq, LANES), lambda b, i: (b, i, 0)),
                  pl.BlockSpec(overlap.shape, lambda b, i: (0, 0))],
        out_specs=pl.BlockSpec((1, tq, MIX_WIDTH), lambda b, i: (b, i, 0)),
        compiler_params=pltpu.CompilerParams(vmem_limit_bytes=VMEM_LIMIT),
        name="nsa_attention",
    )(q, kcmp, vcmp, ksw, vsw, gates, overlap)


def _attn_kernel(mode, lam_init, *refs):
    if mode == "fox":
        q_ref, k_ref, v_ref, kb_ref, o_ref = refs
    elif mode == "diff":
        q_ref, k_ref, v_ref, lam_ref, og_ref, o_ref = refs
    else:
        q_ref, k_ref, v_ref, o_ref = refs
    i = pl.program_id(1)
    t = TQ_ATT
    H = N_HEADS
    qs = []
    for hd in range(H):
        q = q_ref[0, hd]
        if mode == "diff":
            lane = lax.broadcasted_iota(jnp.int32, (1, D_HEAD), 1)
            zero = jnp.zeros_like(q)
            q = jnp.concatenate([jnp.where(lane < DIFF_QK, q, zero), jnp.where(lane >= DIFF_QK, q, zero)], axis=0)
        qs.append(q)
    rows = qs[0].shape[0]

    def step(j, carry, mask=None):
        out = []
        for hd in range(H):
            s = _dot_nt(k_ref[0, hd, pl.ds(j * t, t), :], qs[hd])
            if mode == "fox":
                s = s + kb_ref[0, pl.ds(j * t, t), :][:, hd:hd + 1]
            if mask is not None:
                s = jnp.where(mask, s, NEG_BIG)
            out.append(_softmax_step(s, v_ref[0, hd, :, pl.ds(j * t, t)], *carry[hd]))
        return tuple(out)

    init = tuple((jnp.full((1, rows), NEG_BIG, F32), jnp.zeros((LANES, rows), F32)) for _ in range(H))
    carry = lax.fori_loop(0, i, step, init)
    kk = lax.broadcasted_iota(jnp.int32, (t, 1), 0)
    rr = lax.broadcasted_iota(jnp.int32, (1, rows), 1) % t
    carry = step(i, carry, mask=kk <= rr)

    outs = []
    for hd in range(H):
        o = _softmax_finish(carry[hd][1])
        if mode == "diff":
            lv = lam_ref[0]
            lam = (jnp.exp(jnp.sum(lv[0:1] * lv[1:2], axis=-1, keepdims=True))
                   - jnp.exp(jnp.sum(lv[2:3] * lv[3:4], axis=-1, keepdims=True)) + lam_init)
            o = o[:, 0:t] - lam * o[:, t:2 * t]
            o = o * lax.rsqrt(jnp.mean(o * o, axis=0, keepdims=True) + EPS) * og_ref[0] * (1.0 - lam_init)
        outs.append(o)
    o_ref[0] = jnp.concatenate(outs, axis=0).T.astype(BF16)


def _dense_attention(mode, l, lam_init, q, k, v, *extra):
    B, H, S, dk = q.shape
    t = TQ_ATT
    in_specs = [pl.BlockSpec((1, H, t, dk), lambda b, i: (b, 0, i, 0)),
                pl.BlockSpec((1, H, S, dk), lambda b, i: (b, 0, 0, 0)),
                pl.BlockSpec((1, H, LANES, S), lambda b, i: (b, 0, 0, 0))]
    if mode == "fox":
        in_specs += [pl.BlockSpec((1, S, LANES), lambda b, i: (b, 0, 0))]
    elif mode == "diff":
        in_specs += [pl.BlockSpec((1, 4, DIFF_QK), lambda b, i: (l, 0, 0)),
                     pl.BlockSpec((1, D_HEAD, 1), lambda b, i: (l, 0, 0))]
    return pl.pallas_call(
        functools.partial(_attn_kernel, mode, lam_init),
        out_shape=jax.ShapeDtypeStruct((B, S, MIX_WIDTH), BF16),
        grid=(B, S // t),
        in_specs=in_specs,
        out_specs=pl.BlockSpec((1, t, MIX_WIDTH), lambda b, i: (b, i, 0)),
        compiler_params=pltpu.CompilerParams(vmem_limit_bytes=VMEM_LIMIT),
        name=mode + "_attention",
    )(q, k, v, *extra)


def _merge_kernel(x_ref, h_ref, o0_ref, o1_ref, o2_ref, o3_ref, mod_ref, brw_ref, gw_ref, gb_ref, wo_ref,
                  out_ref):
    hb = h_ref[0]
    merged = None
    for m, o_ref in enumerate((o0_ref, o1_ref, o2_ref, o3_ref)):
        y = _dot(o_ref[0], brw_ref[0, m])
        cols = slice(m * D_MODEL, (m + 1) * D_MODEL)
        gate = _sigmoid(_dot(hb, gw_ref[0, :, cols]) + gb_ref[0, :, cols])
        merged = gate * y if merged is None else merged + gate * y
    out = _dot(merged.astype(BF16), wo_ref[0])
    out_ref[0] = x_ref[0] + mod_ref[0, 0, 2:3, :] * out


def _merge(l, x, hbf, o_nsa, o_fox, o_mla, o_diff, mod, brw, gw, gb, wo):
    B, S, D = x.shape
    ts = TS_DENSE
    row = lambda d: pl.BlockSpec((1, ts, d), lambda b, i: (b, i, 0))
    return pl.pallas_call(
        _merge_kernel,
        out_shape=jax.ShapeDtypeStruct((B, S, D), F32),
        grid=(B, S // ts),
        in_specs=[row(D), row(D), row(MIX_WIDTH), row(MIX_WIDTH), row(MIX_WIDTH), row(MIX_WIDTH),
                  pl.BlockSpec((1, 1, 6, D), lambda b, i: (l, b, 0, 0)),
                  pl.BlockSpec((1, 4, MIX_WIDTH, D), lambda b, i: (l, 0, 0, 0)),
                  pl.BlockSpec((1, D, 4 * D), lambda b, i: (l, 0, 0)),
                  pl.BlockSpec((1, 1, 4 * D), lambda b, i: (l, 0, 0)),
                  pl.BlockSpec((1, D, D), lambda b, i: (l, 0, 0))],
        out_specs=row(D),
        compiler_params=pltpu.CompilerParams(vmem_limit_bytes=VMEM_LIMIT),
        name="merge_out",
    )(x, hbf, o_nsa, o_fox, o_mla, o_diff, mod, brw, gw, gb, wo)


def _ffn_kernel(n_s, xn_ref, modn_ref, xp_ref, modp_ref, wup_ref, cw_ref, cb_ref, wd_ref, out_ref,
                hb_scr, carry_ref):
    n = pl.program_id(0)
    ts = xn_ref.shape[1]

    def norm_next():
        hb_scr[n % 2] = _modulated_norm(xn_ref[0], modn_ref, 3).astype(BF16)

    def init():
        carry_ref[...] = jnp.zeros_like(carry_ref)

    def process_prev():
        keep = jnp.where((n - 1) % n_s == 0, 0.0, 1.0)
        hb = hb_scr[(n + 1) % 2]
        row = lax.broadcasted_iota(jnp.int32, (ts, 1), 0)
        acc = None
        for c in range(D_FF // FF_CHUNK):
            cols = slice(c * FF_CHUNK, (c + 1) * FF_CHUNK)
            g = _dot(hb, wup_ref[0, :, cols])
            v = _dot(hb, wup_ref[0, :, D_FF + c * FF_CHUNK:D_FF + (c + 1) * FF_CHUNK])
            prev = carry_ref[c] * keep
            g1 = jnp.where(row == 0, prev[7:8, :], pltpu.roll(g, 1, 0))
            g2 = jnp.where(row == 0, prev[6:7, :], jnp.where(row == 1, prev[7:8, :], pltpu.roll(g, 2, 0)))
            carry_ref[c] = g[ts - 8:ts, :]
            conv = (cw_ref[0, 0:1, cols] * g2 + cw_ref[0, 1:2, cols] * g1 + cw_ref[0, 2:3, cols] * g
                    + cb_ref[0, :, cols])
            a = (_silu(conv) * v).astype(BF16)
            part = _dot(a, wd_ref[0, cols, :])
            acc = part if acc is None else acc + part
        out_ref[0] = xp_ref[0] + modp_ref[0, 0, 5:6, :] * acc

    _staggered(n, pl.num_programs(0) - 1, norm_next, process_prev, init)


def _ffn(l, x, mod, wup, cw, cb, wd):
    B, S, D = x.shape
    ts = TS_DENSE
    const = pl.Buffered(1)
    n_s = S // ts
    n_tiles = B * n_s
    nxt = lambda n: jnp.minimum(n, n_tiles - 1)
    prv = lambda n: jnp.maximum(n - 1, 0)
    tile = lambda f: pl.BlockSpec((1, ts, D), lambda n: (f(n) // n_s, f(n) % n_s, 0))
    mods = lambda f: pl.BlockSpec((1, 1, 6, D), lambda n: (l, f(n) // n_s, 0, 0))
    return pl.pallas_call(
        functools.partial(_ffn_kernel, n_s),
        out_shape=jax.ShapeDtypeStruct((B, S, D), F32),
        grid=(n_tiles + 1,),
        in_specs=[tile(nxt), mods(nxt), tile(prv), mods(prv),
                  pl.BlockSpec((1, D, 2 * D_FF), lambda n: (l, 0, 0), pipeline_mode=const),
                  pl.BlockSpec((1, 3, D_FF), lambda n: (l, 0, 0)),
                  pl.BlockSpec((1, 1, D_FF), lambda n: (l, 0, 0)),
                  pl.BlockSpec((1, D_FF, D), lambda n: (l, 0, 0), pipeline_mode=const)],
        out_specs=tile(prv),
        scratch_shapes=[pltpu.VMEM((2, ts, D), BF16), pltpu.VMEM((D_FF // FF_CHUNK, 8, FF_CHUNK), F32)],
        compiler_params=pltpu.CompilerParams(dimension_semantics=("arbitrary",),
                                             vmem_limit_bytes=VMEM_LIMIT),
        name="conv_ffn",
    )(x, mod, x, mod, wup, cw, cb, wd)


def _pad_cols(w, n):
    return jnp.pad(w, [(0, 0)] * (w.ndim - 1) + [(0, n - w.shape[-1])])


def _layout_w_in(w_in):
    o = 0
    seg = {}
    for name, n in (("nq", 256), ("nkc", 64), ("nvc", 64), ("nks", 64), ("nvs", 64), ("nkw", 64), ("nvw", 64),
                    ("ng", 12), ("fq", 256), ("fk", 256), ("fv", 256), ("ff", 4),
                    ("mcq", 256), ("mckv", 128), ("mkr", 32), ("dq", 256), ("dk", 256), ("dv", 256)):
        seg[name] = w_in[..., o:o + n]
        o += n
    parts = [seg["nq"], seg["nks"], seg["nkw"], seg["nkc"], seg["nvc"], _pad_cols(seg["ng"], LANES),
             seg["fq"], seg["fk"], _pad_cols(seg["ff"], LANES),
             seg["mcq"], seg["mckv"], _pad_cols(seg["mkr"], LANES),
             seg["dq"], seg["dk"]]
    w1 = jnp.concatenate(parts, axis=-1).astype(BF16)
    wvt = jnp.concatenate([seg["nvs"], seg["nvw"], seg["fv"], seg["dv"]], axis=-1)
    return w1, jnp.swapaxes(wvt, -1, -2).astype(BF16)


def _pack_vecs(nsa_qk_g, fox_qk_g, fox_f_b, mla_cq_g, mla_ckv_g, mla_qk_g, diff_qk_g):
    L = nsa_qk_g.shape[0]
    t4 = lambda g: jnp.tile(g, (1, 256 // g.shape[-1]))
    mla_pad = lambda g: _pad_cols(g, LANES)
    rows = [None] * N_VEC
    rows[V_NQ] = t4(nsa_qk_g[:, 0]) * (LOG2E * D_HEAD ** -0.5)
    rows[V_NK] = jnp.concatenate([nsa_qk_g[:, 2], nsa_qk_g[:, 3]], axis=-1)
    rows[V_FQ] = t4(fox_qk_g[:, 0]) * (LOG2E * D_HEAD ** -0.5)
    rows[V_FK] = t4(fox_qk_g[:, 1])
    rows[V_MQ] = t4(mla_pad(mla_qk_g[:, 0])) * (LOG2E * MLA_QK ** -0.5)
    rows[V_MK] = t4(mla_pad(mla_qk_g[:, 1]))
    rows[V_DQ] = t4(diff_qk_g[:, 0]) * (LOG2E * DIFF_QK ** -0.5)
    rows[V_DK] = t4(diff_qk_g[:, 1])
    rows[V_FB] = fox_f_b
    rows[V_CQG] = mla_cq_g
    rows[V_CKVG] = mla_ckv_g
    rows = [jnp.zeros((L, 256), F32) if r is None else _pad_cols(r.astype(F32), 256) for r in rows]
    return jnp.stack(rows, axis=1)


def _block_diag_ones():
    d = np.arange(256)
    mats = [(d[:, None] // g == d[None, :] // g) for g in (D_HEAD, LANES, DIFF_QK)]
    return jnp.asarray(np.stack(mats), dtype=BF16)


def _overlap_matrix(n_cmp_rows, n_blk):
    c = np.arange(n_cmp_rows)[None, :] * CMP_STRIDE
    b = np.arange(n_blk)[:, None] * SEL_BLOCK
    ov = (c < b + SEL_BLOCK) & (c + CMP_BLOCK > b)
    return jnp.asarray(ov, dtype=BF16)


def kernel(x, c, positions, ada_w, ada_b, w_in, nsa_qk_g, nsa_cmp_pe, nsa_cmp_w1, nsa_cmp_w2, fox_qk_g, fox_f_b,
           mla_cq_g, mla_ckv_g, mla_w_uq, mla_w_ukv, mla_qk_g, diff_qk_g, diff_lambda, diff_out_g, br_w, gate_w,
           gate_b, w_out, ffn_w_up, ffn_conv_w, ffn_conv_b, ffn_w_down):
    B, S, D = x.shape
    L = ada_w.shape[0]
    H = N_HEADS
    n_half = S // CMP_STRIDE

    mod = _modulation(c, ada_w, ada_b).reshape(L, B, 6, D)
    tab = _rope_tables(positions)

    w1, wvt = _layout_w_in(w_in)
    vecs = _pack_vecs(nsa_qk_g, fox_qk_g, fox_f_b, mla_cq_g, mla_ckv_g, mla_qk_g, diff_qk_g)
    bd = _block_diag_ones()
    tri = jnp.asarray(np.tril(np.ones((TS_PROJ, TS_PROJ), np.float32)), dtype=BF16)
    overlap = _overlap_matrix(n_half, S // SEL_BLOCK)
    wuq = _pad_cols(mla_w_uq.reshape(L, MLA_Q_RANK, H, MLA_QK), LANES).reshape(L, MLA_Q_RANK, H * LANES)
    ukv = mla_w_ukv.reshape(L, MLA_KV_RANK, H, MLA_NOPE + D_HEAD)
    wuk = jnp.pad(ukv[..., :MLA_NOPE], [(0, 0), (0, 0), (0, 0), (MLA_ROPE, LANES - MLA_QK)])
    wuk = wuk.reshape(L, MLA_KV_RANK, H * LANES).astype(BF16)
    wuvt = jnp.swapaxes(ukv[..., MLA_NOPE:].reshape(L, MLA_KV_RANK, H * D_HEAD), 1, 2).astype(BF16)
    wuq = wuq.astype(BF16)
    pe = nsa_cmp_pe.reshape(L, 2, CMP_BLOCK * D_HEAD)
    cw1 = nsa_cmp_w1.astype(BF16)
    cw2 = jnp.stack([nsa_cmp_w2[:, 0], jnp.swapaxes(nsa_cmp_w2[:, 1], 1, 2)], axis=1).astype(BF16)
    kcg = _pad_cols(nsa_qk_g[:, 1], LANES).reshape(L, 1, LANES)
    brw = br_w.astype(BF16)
    gw = gate_w.astype(BF16)
    gb = gate_b.reshape(L, 1, 4 * D)
    wo = w_out.astype(BF16)
    wup = ffn_w_up.astype(BF16)
    wd = ffn_w_down.astype(BF16)
    cb = ffn_conv_b.reshape(L, 1, D_FF)
    og = diff_out_g.reshape(L, D_HEAD, 1)

    for l in range(L):
        lam_init = 0.8 - 0.6 * math.exp(-0.3 * l)
        (hbf, nq, nk, nv, nkc, nvc, ng, fq, fk, fv, fkb, mq, mk, mv, dq, dk, dv) = _projection(
            l, x, mod, tab, w1, wvt, vecs, bd, tri, wuq, wuk, wuvt)
        kcmp, vcmp = _compress(l, nkc, nvc, pe, cw1, cw2, kcg)
        o_nsa = _nsa_attention(nq, kcmp, vcmp, nk, nv, ng, overlap)
        o_fox = _dense_attention("fox", l, lam_init, fq, fk, fv, fkb)
        o_mla = _dense_attention("mla", l, lam_init, mq, mk, mv)
        o_diff = _dense_attention("diff", l, lam_init, dq, dk, dv, diff_lambda, og)
        x = _merge(l, x, hbf, o_nsa, o_fox, o_mla, o_diff, mod, brw, gw, gb, wo)
        x = _ffn(l, x, mod, wup, ffn_conv_w, cb, wd)
    return x
```

```python
import functools
import math

import jax
import jax.numpy as jnp
import numpy as np
from jax import lax
from jax.experimental import pallas as pl
from jax.experimental.pallas import tpu as pltpu

F32 = jnp.float32
BF16 = jnp.bfloat16

D_MODEL = 1024
D_HEAD = 64
N_HEADS = 4
MIX_WIDTH = 256
NSA_ROT = 16
CMP_BLOCK = 32
CMP_STRIDE = 16
SEL_BLOCK = 64
SEL_TOPN = 16
WINDOW = 512
MLA_Q_RANK = 256
MLA_KV_RANK = 128
MLA_NOPE = 64
MLA_ROPE = 32
MLA_QK = MLA_ROPE + MLA_NOPE
DIFF_QK = 32
DIFF_ROT = 8
D_FF = 2816
ROPE_THETA = 500000.0
EPS = 1e-6
FORCE_SCORE = 1e4
NEG_BIG = -1e30

LANES = 128
VMEM_LIMIT = 56 * 1024 * 1024

C_NQ, C_NKSW, C_NKVC, C_NG, C_FF = 0, 256, 384, 512, 640
C_FQ, C_FK = 768, 1024
C_MCQ, C_MCKV, C_MKR = 1280, 1536, 1664
C_DQ, C_DK = 1792, 2048
W_IN_PAD = 2304
R_NV, R_FV, R_DV = 0, 128, 384
W_VT_ROWS = 640

V_NQ, V_NK, V_FQ, V_FK, V_MQ, V_MK, V_DQ, V_DK, V_FB, V_CQG, V_CKVG = range(11)
N_VEC = 16

TS_PROJ = 512
TS_DENSE = 512
TQ_ATT = 512
TQ_NSA = 256
TK_NSA = 256
FF_CHUNK = 2816
LOG2E = math.log2(math.e)


def _dot(a, b):
    return jnp.dot(a, b, preferred_element_type=F32)


def _dot_nt(a, b):
    return lax.dot_general(a, b, (((1,), (1,)), ((), ())), preferred_element_type=F32)


def _split3(x):
    hi = x.astype(BF16)
    r = x - hi.astype(F32)
    mid = r.astype(BF16)
    lo = (r - mid.astype(F32)).astype(BF16)
    return hi, mid, lo


def _rms_rows(x):
    return x * lax.rsqrt(jnp.mean(x * x, axis=-1, keepdims=True) + EPS)


def _modulated_norm(x, mod_ref, r):
    return _rms_rows(x) * (1.0 + mod_ref[0, 0, r + 1:r + 2, :]) + mod_ref[0, 0, r:r + 1, :]


def _group_rms(x, bd, inv_n):
    ss = _dot((x * x).astype(BF16), bd)
    return x * lax.rsqrt(ss * inv_n + EPS)


def _rope(xc, tab_ref, cfg, half):
    c = tab_ref[3 * cfg, 0]
    sa = tab_ref[3 * cfg + 1, 0]
    sb = tab_ref[3 * cfg + 2, 0]
    return xc * c + pltpu.roll(xc, LANES - half, 1) * sa + pltpu.roll(xc, half, 1) * sb


def _log_sigmoid(x):
    return jnp.minimum(x, 0.0) - jnp.log1p(jnp.exp(-jnp.abs(x)))


def _sigmoid(x):
    return 1.0 / (1.0 + jnp.exp(-x))


def _silu(x):
    return x * _sigmoid(x)


def _softmax_step(st, v1t, m, acc):
    m_new = jnp.maximum(m, jnp.max(st, axis=0, keepdims=True))
    p = jnp.exp2(st - m_new).astype(BF16)
    return m_new, jnp.exp2(m - m_new) * acc + _dot(v1t, p)


def _softmax_finish(acc):
    return acc[:D_HEAD] / acc[D_HEAD:D_HEAD + 1]


def _staggered(n, n_tiles, norm_next, process_prev, init):
    @pl.when(n == 0)
    def _():
        init()
        norm_next()

    @pl.when((n > 0) & (n < n_tiles))
    def _():
        norm_next()
        process_prev()

    @pl.when(n == n_tiles)
    def _():
        process_prev()


def _mod_kernel(c_ref, w_ref, b_ref, o_ref):
    c = c_ref[...]
    a = _silu(c).astype(BF16)
    o_ref[0] = _dot(a, w_ref[0].astype(BF16)) + b_ref[0]


def _modulation(c, ada_w, ada_b):
    L, D, N = ada_w.shape
    B = c.shape[0]
    tn = 1536
    return pl.pallas_call(
        _mod_kernel,
        out_shape=jax.ShapeDtypeStruct((L, B, N), F32),
        grid=(L, N // tn),
        in_specs=[pl.BlockSpec((B, D), lambda l, n: (0, 0)),
                  pl.BlockSpec((1, D, tn), lambda l, n: (l, 0, n)),
                  pl.BlockSpec((1, 1, tn), lambda l, n: (l, 0, n))],
        out_specs=pl.BlockSpec((1, B, tn), lambda l, n: (l, 0, n)),
        compiler_params=pltpu.CompilerParams(vmem_limit_bytes=VMEM_LIMIT),
        name="adaln_mod",
    )(c, ada_w, ada_b.reshape(L, 1, N))


_ROPE_CFGS = ((D_HEAD, NSA_ROT, 32), (LANES, MLA_ROPE, 0), (DIFF_QK, DIFF_ROT, 48))


def _rope_tab_kernel(pos_ref, frq_ref, o_ref):
    ang = pos_ref[0].astype(F32) * frq_ref[...]
    cs = jnp.cos(ang)
    sn = jnp.sin(ang)
    lane = lax.broadcasted_iota(jnp.int32, (1, LANES), 1)
    for cfg, (group, n_rot, src) in enumerate(_ROPE_CFGS):
        def spread(x):
            out = None
            for gi in range(LANES // group):
                shift = (gi * group - src) % LANES
                xs = x if shift == 0 else pltpu.roll(x, shift, 1)
                out = xs if out is None else jnp.where(lane >= gi * group, xs, out)
            return out

        d = lane % group
        c = spread(cs)
        s = spread(sn)
        o_ref[3 * cfg, 0] = jnp.where(d < n_rot, c, 1.0)
        o_ref[3 * cfg + 1, 0] = jnp.where(d < n_rot // 2, -s, 0.0)
        o_ref[3 * cfg + 2, 0] = jnp.where((d >= n_rot // 2) & (d < n_rot), s, 0.0)


def _rope_consts():
    row = jnp.zeros((LANES,), F32)
    for group, n_rot, src in _ROPE_CFGS:
        half = n_rot // 2
        inv_freq = ROPE_THETA ** (-jnp.arange(half, dtype=F32) / half)
        row = row.at[src:src + n_rot].set(jnp.tile(inv_freq, 2))
    return row.reshape(1, LANES)


def _rope_tables(positions):
    B, S = positions.shape
    ts = 512
    return pl.pallas_call(
        _rope_tab_kernel,
        out_shape=jax.ShapeDtypeStruct((9, B, S, LANES), F32),
        grid=(B, S // ts),
        in_specs=[pl.BlockSpec((1, ts, 1), lambda b, i: (b, i, 0)),
                  pl.BlockSpec((1, LANES), lambda b, i: (0, 0))],
        out_specs=pl.BlockSpec((9, 1, ts, LANES), lambda b, i: (0, b, i, 0)),
        name="rope_tables",
    )(positions.reshape(B, S, 1), _rope_consts())


def _proj_kernel(n_s, xn_ref, modn_ref, tab_ref, w_ref, wvt_ref, vec_ref, bd_ref, tri_ref, wuq_ref, wuk_ref,
                 wuvt_ref, hbf_ref, nq_ref, nk_ref, nv_ref, nkc_ref, nvc_ref, ng_ref,
                 fq_ref, fk_ref, fv_ref, fkb_ref,
                 mq_ref, mk_ref, mv_ref, dq_ref, dk_ref, dv_ref, hb_scr, carry_ref):
    n = pl.program_id(0)
    ts = xn_ref.shape[1]

    def norm_next():
        hb_scr[n % 2] = _modulated_norm(xn_ref[0], modn_ref, 0).astype(BF16)

    def init():
        carry_ref[...] = jnp.zeros_like(carry_ref)

    def process_prev():
        i = (n - 1) % n_s
        keep = jnp.where(i == 0, 0.0, 1.0)
        hb = hb_scr[(n + 1) % 2]
        hbf_ref[0] = hb

        def proj(c0, width):
            return _dot(hb, w_ref[0, :, c0:c0 + width])

        def vec(r, width=LANES):
            return vec_ref[0, r:r + 1, 0:width]

        bd64, bd128, bd32 = bd_ref[0], bd_ref[1], bd_ref[2]
        row = lax.broadcasted_iota(jnp.int32, (ts, 1), 0)
        lane = lax.broadcasted_iota(jnp.int32, (1, LANES), 1)

        def head_pair(seg, gain_row, bd, inv_n, cfg, half, out_ref):
            yn = _group_rms(seg, bd, inv_n) * vec(gain_row, 256)
            for c in range(2):
                y = yn[:, c * LANES:(c + 1) * LANES]
                if cfg is not None:
                    y = _rope(y, tab_ref, cfg, half)
                yb = y.astype(BF16)
                out_ref[0, 2 * c] = yb[:, :D_HEAD]
                out_ref[0, 2 * c + 1] = yb[:, D_HEAD:]

        ones_rows = jnp.where(lax.broadcasted_iota(jnp.int32, (D_HEAD, ts), 0) == 0, 1.0, 0.0).astype(BF16)

        def value_store(vt, out_ref):
            for hd in range(vt.shape[0] // D_HEAD):
                out_ref[0, hd] = jnp.concatenate([vt[hd * D_HEAD:(hd + 1) * D_HEAD].astype(BF16), ones_rows], axis=0)

        head_pair(proj(C_NQ, 256), V_NQ, bd64, 1.0 / D_HEAD, 0, NSA_ROT // 2, nq_ref)
        nsa_kv = proj(C_NKSW, 2 * LANES)
        ksw = _group_rms(nsa_kv[:, :LANES], bd64[0:LANES, 0:LANES], 1.0 / D_HEAD) * vec(V_NK)
        ksw = _rope(ksw, tab_ref, 0, NSA_ROT // 2)
        blk_hot = jnp.where(lane - D_HEAD == (i * ts + row) // SEL_BLOCK, 1.0, 0.0)
        nk_ref[0, 0] = jnp.where(lane < D_HEAD, ksw, blk_hot).astype(BF16)
        nk_ref[0, 1] = jnp.where(lane < D_HEAD, pltpu.roll(ksw, D_HEAD, 1), 0.0).astype(BF16)
        kvc = nsa_kv[:, LANES:]
        nkc_ref[0] = kvc[:, :D_HEAD]
        nvc_ref[0] = kvc[:, D_HEAD:]
        gates = proj(C_NG, 2 * LANES)
        ng_ref[0] = _sigmoid(gates[:, :LANES])

        head_pair(proj(C_FQ, 256), V_FQ, bd64, 1.0 / D_HEAD, None, 0, fq_ref)
        head_pair(proj(C_FK, 256), V_FK, bd64, 1.0 / D_HEAD, None, 0, fk_ref)

        lf = _log_sigmoid(gates[:, LANES:] + vec(V_FB))
        tri = tri_ref[...]
        p0, p1, p2 = _split3(lf)
        c01 = _dot(tri, jnp.concatenate([p0, p1], axis=1))
        cum = c01[:, :LANES] + c01[:, LANES:] + _dot(tri, p2) + carry_ref[0:1, :] * keep
        carry_ref[0:1, :] = cum[ts - 1:ts, :]
        fkb_ref[0] = cum * (-LOG2E)

        cq = proj(C_MCQ, MLA_Q_RANK)
        cqn = (_rms_rows(cq) * vec(V_CQG, MLA_Q_RANK)).astype(BF16)
        qm = _dot(cqn, wuq_ref[0])
        mla_kv = proj(C_MCKV, 2 * LANES)
        ckvn = (_rms_rows(mla_kv[:, :MLA_KV_RANK]) * vec(V_CKVG)).astype(BF16)
        kk = _dot(ckvn, wuk_ref[0])
        vvt = _dot_nt(wuvt_ref[0], ckvn)
        kr = mla_kv[:, MLA_KV_RANK:]
        kr2 = jnp.concatenate([kr, kr], axis=1)
        for pair in range(N_HEADS // 2):
            sl = slice(pair * 256, (pair + 1) * 256)
            yq = _group_rms(qm[:, sl], bd128, 1.0 / MLA_QK) * vec(V_MQ, 256)
            yk = _group_rms(kk[:, sl] + kr2, bd128, 1.0 / MLA_QK) * vec(V_MK, 256)
            for c in range(2):
                cs = slice(c * LANES, (c + 1) * LANES)
                mq_ref[0, 2 * pair + c] = _rope(yq[:, cs], tab_ref, 1, MLA_ROPE // 2).astype(BF16)
                mk_ref[0, 2 * pair + c] = _rope(yk[:, cs], tab_ref, 1, MLA_ROPE // 2).astype(BF16)

        head_pair(proj(C_DQ, 256), V_DQ, bd32, 1.0 / DIFF_QK, 2, DIFF_ROT // 2, dq_ref)
        head_pair(proj(C_DK, 256), V_DK, bd32, 1.0 / DIFF_QK, 2, DIFF_ROT // 2, dk_ref)

        value_store(vvt, mv_ref)
        vt_all = _dot_nt(wvt_ref[0], hb)
        value_store(vt_all[R_NV:R_FV], nv_ref)
        value_store(vt_all[R_FV:R_DV], fv_ref)
        value_store(vt_all[R_DV:W_VT_ROWS], dv_ref)

    _staggered(n, pl.num_programs(0) - 1, norm_next, process_prev, init)


def _projection(l, x, mod, tab, w1, wvt, vecs, bd, tri, wuq, wuk, wuvt):
    B, S, D = x.shape
    ts = TS_PROJ
    H = N_HEADS
    n_s = S // ts
    n_tiles = B * n_s
    nxt = lambda n: jnp.minimum(n, n_tiles - 1)
    prv = lambda n: jnp.maximum(n - 1, 0)
    hm = lambda d, dt: jax.ShapeDtypeStruct((B, H, S, d), dt)
    hm_spec = lambda nh, d: pl.BlockSpec((1, nh, ts, d), lambda n: (prv(n) // n_s, 0, prv(n) % n_s, 0))
    row_spec = lambda d: pl.BlockSpec((1, ts, d), lambda n: (prv(n) // n_s, prv(n) % n_s, 0))
    vt = lambda nh: jax.ShapeDtypeStruct((B, nh, LANES, S), BF16)
    vt_spec = lambda nh: pl.BlockSpec((1, nh, LANES, ts), lambda n: (prv(n) // n_s, 0, 0, prv(n) % n_s))
    out_shape = [
        jax.ShapeDtypeStruct((B, S, D), BF16),
        hm(D_HEAD, BF16),
        jax.ShapeDtypeStruct((B, 2, S, LANES), BF16),
        vt(2),
        jax.ShapeDtypeStruct((B, S, D_HEAD), F32),
        jax.ShapeDtypeStruct((B, S, D_HEAD), F32),
        jax.ShapeDtypeStruct((B, S, LANES), F32),
        hm(D_HEAD, BF16), hm(D_HEAD, BF16), vt(H),
        jax.ShapeDtypeStruct((B, S, LANES), F32),
        hm(LANES, BF16), hm(LANES, BF16), vt(H),
        hm(D_HEAD, BF16), hm(D_HEAD, BF16), vt(H),
    ]
    out_specs = [
        row_spec(D), hm_spec(H, D_HEAD), hm_spec(2, LANES), vt_spec(2),
        row_spec(D_HEAD), row_spec(D_HEAD), row_spec(LANES),
        hm_spec(H, D_HEAD), hm_spec(H, D_HEAD), vt_spec(H),
        row_spec(LANES),
        hm_spec(H, LANES), hm_spec(H, LANES), vt_spec(H),
        hm_spec(H, D_HEAD), hm_spec(H, D_HEAD), vt_spec(H),
    ]
    in_specs = [
        pl.BlockSpec((1, ts, D), lambda n: (nxt(n) // n_s, nxt(n) % n_s, 0)),
        pl.BlockSpec((1, 1, 6, D), lambda n: (l, nxt(n) // n_s, 0, 0)),
        pl.BlockSpec((9, 1, ts, LANES), lambda n: (0, prv(n) // n_s, prv(n) % n_s, 0)),
        pl.BlockSpec((1, D, W_IN_PAD), lambda n: (l, 0, 0)),
        pl.BlockSpec((1, W_VT_ROWS, D), lambda n: (l, 0, 0)),
        pl.BlockSpec((1, N_VEC, 256), lambda n: (l, 0, 0)),
        pl.BlockSpec((3, 256, 256), lambda n: (0, 0, 0)),
        pl.BlockSpec((ts, ts), lambda n: (0, 0)),
        pl.BlockSpec((1, MLA_Q_RANK, 512), lambda n: (l, 0, 0)),
        pl.BlockSpec((1, MLA_KV_RANK, 512), lambda n: (l, 0, 0)),
        pl.BlockSpec((1, 256, MLA_KV_RANK), lambda n: (l, 0, 0)),
    ]
    return pl.pallas_call(
        functools.partial(_proj_kernel, n_s),
        out_shape=out_shape,
        grid=(n_tiles + 1,),
        in_specs=in_specs,
        out_specs=out_specs,
        scratch_shapes=[pltpu.VMEM((2, ts, D), BF16), pltpu.VMEM((8, LANES), F32)],
        compiler_params=pltpu.CompilerParams(dimension_semantics=("arbitrary",),
                                             vmem_limit_bytes=VMEM_LIMIT),
        name="in_proj",
    )(x, mod, tab, w1, wvt, vecs, bd, tri, wuq, wuk, wuvt)


def _cmp_kernel(kc_ref, vc_ref, pe_ref, w1_ref, w2_ref, g_ref, ko_ref, vo_ref, x_scr):
    half = CMP_STRIDE * D_HEAD
    n = kc_ref.shape[1] // CMP_STRIDE

    def hidden(src_ref, j):
        for p in range(CMP_STRIDE):
            x_scr[:, p * D_HEAD:(p + 1) * D_HEAD] = src_ref[0, pl.ds(p, n, stride=CMP_STRIDE), :]
        x2 = x_scr[...]
        xa = _dot((x2 + pe_ref[0, j:j + 1, 0:half]).astype(BF16), w1_ref[0, j, 0:half, :])
        xb = _dot((x2 + pe_ref[0, j:j + 1, half:2 * half]).astype(BF16), w1_ref[0, j, half:2 * half, :])
        return _silu(xa + pltpu.roll(xb, n - 1, 0)).astype(BF16)

    kcmp = _rms_rows(_dot(hidden(kc_ref, 0), w2_ref[0, 0])) * g_ref[0, 0:1, 0:D_HEAD]
    ko_ref[0] = kcmp.astype(BF16)
    vo_ref[0] = _dot_nt(w2_ref[0, 1], hidden(vc_ref, 1)).astype(BF16)


def _compress(l, kc, vc, pe, w1, w2, g):
    B, S, _ = kc.shape
    n = S // CMP_STRIDE
    wide = CMP_STRIDE * D_HEAD
    return pl.pallas_call(
        _cmp_kernel,
        out_shape=[jax.ShapeDtypeStruct((B, n, D_HEAD), BF16), jax.ShapeDtypeStruct((B, D_HEAD, n), BF16)],
        grid=(B,),
        in_specs=[pl.BlockSpec((1, S, D_HEAD), lambda b: (b, 0, 0)),
                  pl.BlockSpec((1, S, D_HEAD), lambda b: (b, 0, 0)),
                  pl.BlockSpec((1, 2, 2 * wide), lambda b: (l, 0, 0)),
                  pl.BlockSpec((1, 2, 2 * wide, D_HEAD), lambda b: (l, 0, 0, 0)),
                  pl.BlockSpec((1, 2, D_HEAD, D_HEAD), lambda b: (l, 0, 0, 0)),
                  pl.BlockSpec((1, 1, LANES), lambda b: (l, 0, 0))],
        out_specs=[pl.BlockSpec((1, n, D_HEAD), lambda b: (b, 0, 0)),
                   pl.BlockSpec((1, D_HEAD, n), lambda b: (b, 0, 0))],
        scratch_shapes=[pltpu.VMEM((n, wide), F32)],
        name="nsa_compress",
    )(kc, vc, pe, w1, w2, g)


def _nsa_kernel(q_ref, kc_ref, vc_ref, k_ref, v_ref, g_ref, ov_ref, o_ref):
    i = pl.program_id(1)
    tq, tk = TQ_NSA, TK_NSA
    H = N_HEADS
    rows = H * tq
    n_blk = ov_ref.shape[0]
    q = q_ref[0].reshape(rows, D_HEAD)
    t4 = i * tq + lax.broadcasted_iota(jnp.int32, (1, rows), 1) % tq

    nc = kc_ref.shape[1]
    sc = _dot_nt(kc_ref[0], q)
    cend = lax.broadcasted_iota(jnp.int32, (nc, 1), 0) * CMP_STRIDE + (CMP_BLOCK - 1)
    sc = jnp.where(cend <= t4, sc, NEG_BIG)
    e = jnp.exp2(sc - jnp.max(sc, axis=0, keepdims=True))
    p = e / jnp.sum(e, axis=0, keepdims=True)
    p = jnp.where(t4 >= CMP_BLOCK - 1, p, 0.0)
    o_cmp = _dot(vc_ref[0], p.astype(BF16))
    psum = p[:, 0:tq] + p[:, tq:2 * tq] + p[:, 2 * tq:3 * tq] + p[:, 3 * tq:4 * tq]
    p0, p1, p2 = _split3(psum)
    ov = ov_ref[...]
    imp = _dot(ov, p0) + _dot(ov, p1) + _dot(ov, p2)

    blk = lax.broadcasted_iota(jnp.int32, (n_blk, 1), 0)
    tl = i * tq + lax.broadcasted_iota(jnp.int32, (1, tq), 1)
    cur = tl // SEL_BLOCK
    forced = (blk == 0) | (blk == cur) | (blk == cur - 1)
    score = jnp.where(blk * SEL_BLOCK > tl, -1.0, jnp.where(forced, FORCE_SCORE, imp))
    rank = jnp.zeros((n_blk, tq), F32)
    for r in range(n_blk):
        other = score[r:r + 1, :]
        ahead = (other > score) | ((other == score) & (blk > r))
        rank = rank + jnp.where(ahead, 1.0, 0.0)
    drop_t = jnp.where(rank < float(min(SEL_TOPN, n_blk)), 0.0, NEG_BIG)
    drop = jnp.concatenate([drop_t, jnp.zeros((LANES - n_blk, tq), F32)], axis=0).T.astype(BF16)
    qx = jnp.concatenate([q, jnp.concatenate([drop[:, 0:D_HEAD]] * H, axis=0)], axis=1)

    def slc_scores(j):
        return _dot_nt(k_ref[0, 0, pl.ds(j * tk, tk), :], qx)

    init = (jnp.full((1, rows), NEG_BIG, F32), jnp.zeros((LANES, rows), F32))

    def far_step(j, carry):
        return _softmax_step(slc_scores(j), v_ref[0, 0, :, pl.ds(j * tk, tk)], *carry)

    n_near = WINDOW // tk
    kl = lax.broadcasted_iota(jnp.int32, (tk, 1), 0)
    rl = lax.broadcasted_iota(jnp.int32, (1, rows), 1) % tq

    def near_step(rel, carry):
        j = i - rel
        s_s = slc_scores(j)
        s_w = _dot_nt(k_ref[0, 1, pl.ds(j * tk, tk), :], qx)
        if rel == 0:
            s_s = jnp.where(kl <= rl, s_s, NEG_BIG)
            s_w = jnp.where(kl <= rl, s_w, NEG_BIG)
        elif rel == n_near:
            s_w = jnp.where(kl > rl, s_w, NEG_BIG)
        c_s = _softmax_step(s_s, v_ref[0, 0, :, pl.ds(j * tk, tk)], *carry[0])
        c_w = _softmax_step(s_w, v_ref[0, 1, :, pl.ds(j * tk, tk)], *carry[1])
        return c_s, c_w

    lo = jnp.maximum(i - n_near, 0)
    carry = (lax.fori_loop(0, lo, far_step, init), init)
    for rel in range(n_near, 0, -1):
        carry = lax.cond(i >= rel, functools.partial(near_step, rel), lambda c: c, carry)
    (_, a_s), (_, a_w) = near_step(0, carry)
    o_slc = _softmax_finish(a_s)
    o_win = _softmax_finish(a_w)

    gt = g_ref[0].T
    outs = []
    for hd in range(H):
        r = slice(hd * tq, (hd + 1) * tq)
        outs.append(gt[hd:hd + 1] * o_cmp[:, r] + gt[H + hd:H + hd + 1] * o_slc[:, r]
                    + gt[2 * H + hd:2 * H + hd + 1] * o_win[:, r])
    o_ref[0] = jnp.concatenate(outs, axis=0).T.astype(BF16)


def _nsa_attention(q, kcmp, vcmp, ksw, vsw, gates, overlap):
    B, H, S, _ = q.shape
    tq = TQ_NSA
    nc = kcmp.shape[1]
    return pl.pallas_call(
        _nsa_kernel,
        out_shape=jax.ShapeDtypeStruct((B, S, MIX_WIDTH), BF16),
        grid=(B, S // tq),
        in_specs=[pl.BlockSpec((1, H, tq, D_HEAD), lambda b, i: (b, 0, i, 0)),
                  pl.BlockSpec((1, nc, D_HEAD), lambda b, i: (b, 0, 0)),
                  pl.BlockSpec((1, D_HEAD, nc), lambda b, i: (b, 0, 0)),
                  pl.BlockSpec((1, 2, S, LANES), lambda b, i: (b, 0, 0, 0)),
                  pl.BlockSpec((1, 2, LANES, S), lambda b, i: (b, 0, 0, 0)),
                  pl.BlockSpec((1, tq, LANES), lambda b, i: (b, i, 0)),
                  pl.BlockSpec(overlap.shape, lambda b, i: (0, 0))],
        out_specs=pl.BlockSpec((1, tq, MIX_WIDTH), lambda b, i: (b, i, 0)),
        compiler_params=pltpu.CompilerParams(vmem_limit_bytes=VMEM_LIMIT),
        name="nsa_attention",
    )(q, kcmp, vcmp, ksw, vsw, gates, overlap)


def _attn_kernel(mode, lam_init, *refs):
    if mode == "fox":
        q_ref, k_ref, v_ref, kb_ref, o_ref = refs
    elif mode == "diff":
        q_ref, k_ref, v_ref, lam_ref, og_ref, o_ref = refs
    else:
        q_ref, k_ref, v_ref, o_ref = refs
    i = pl.program_id(1)
    t = TQ_ATT
    H = N_HEADS
    qs = []
    for hd in range(H):
        q = q_ref[0, hd]
        if mode == "diff":
            lane = lax.broadcasted_iota(jnp.int32, (1, D_HEAD), 1)
            zero = jnp.zeros_like(q)
            q = jnp.concatenate([jnp.where(lane < DIFF_QK, q, zero), jnp.where(lane >= DIFF_QK, q, zero)], axis=0)
        qs.append(q)
    rows = qs[0].shape[0]

    def step(j, carry, mask=None):
        out = []
        for hd in range(H):
            s = _dot_nt(k_ref[0, hd, pl.ds(j * t, t), :], qs[hd])
            if mode == "fox":
                s = s + kb_ref[0, pl.ds(j * t, t), :][:, hd:hd + 1]
            if mask is not None:
                s = jnp.where(mask, s, NEG_BIG)
            out.append(_softmax_step(s, v_ref[0, hd, :, pl.ds(j * t, t)], *carry[hd]))
        return tuple(out)

    init = tuple((jnp.full((1, rows), NEG_BIG, F32), jnp.zeros((LANES, rows), F32)) for _ in range(H))
    carry = lax.fori_loop(0, i, step, init)
    kk = lax.broadcasted_iota(jnp.int32, (t, 1), 0)
    rr = lax.broadcasted_iota(jnp.int32, (1, rows), 1) % t
    carry = step(i, carry, mask=kk <= rr)

    outs = []
    for hd in range(H):
        o = _softmax_finish(carry[hd][1])
        if mode == "diff":
            lv = lam_ref[0]
            lam = (jnp.exp(jnp.sum(lv[0:1] * lv[1:2], axis=-1, keepdims=True))
                   - jnp.exp(jnp.sum(lv[2:3] * lv[3:4], axis=-1, keepdims=True)) + lam_init)
            o = o[:, 0:t] - lam * o[:, t:2 * t]
            o = o * lax.rsqrt(jnp.mean(o * o, axis=0, keepdims=True) + EPS) * og_ref[0] * (1.0 - lam_init)
        outs.append(o)
    o_ref[0] = jnp.concatenate(outs, axis=0).T.astype(BF16)


def _dense_attention(mode, l, lam_init, q, k, v, *extra):
    B, H, S, dk = q.shape
    t = TQ_ATT
    in_specs = [pl.BlockSpec((1, H, t, dk), lambda b, i: (b, 0, i, 0)),
                pl.BlockSpec((1, H, S, dk), lambda b, i: (b, 0, 0, 0)),
                pl.BlockSpec((1, H, LANES, S), lambda b, i: (b, 0, 0, 0))]
    if mode == "fox":
        in_specs += [pl.BlockSpec((1, S, LANES), lambda b, i: (b, 0, 0))]
    elif mode == "diff":
        in_specs += [pl.BlockSpec((1, 4, DIFF_QK), lambda b, i: (l, 0, 0)),
                     pl.BlockSpec((1, D_HEAD, 1), lambda b, i: (l, 0, 0))]
    return pl.pallas_call(
        functools.partial(_attn_kernel, mode, lam_init),
        out_shape=jax.ShapeDtypeStruct((B, S, MIX_WIDTH), BF16),
        grid=(B, S // t),
        in_specs=in_specs,
        out_specs=pl.BlockSpec((1, t, MIX_WIDTH), lambda b, i: (b, i, 0)),
        compiler_params=pltpu.CompilerParams(vmem_limit_bytes=VMEM_LIMIT),
        name=mode + "_attention",
    )(q, k, v, *extra)


def _merge_kernel(x_ref, h_ref, o0_ref, o1_ref, o2_ref, o3_ref, mod_ref, brw_ref, gw_ref, gb_ref, wo_ref,
                  out_ref):
    hb = h_ref[0]
    merged = None
    for m, o_ref in enumerate((o0_ref, o1_ref, o2_ref, o3_ref)):
        y = _dot(o_ref[0], brw_ref[0, m])
        cols = slice(m * D_MODEL, (m + 1) * D_MODEL)
        gate = _sigmoid(_dot(hb, gw_ref[0, :, cols]) + gb_ref[0, :, cols])
        merged = gate * y if merged is None else merged + gate * y
    out = _dot(merged.astype(BF16), wo_ref[0])
    out_ref[0] = x_ref[0] + mod_ref[0, 0, 2:3, :] * out


def _merge(l, x, hbf, o_nsa, o_fox, o_mla, o_diff, mod, brw, gw, gb, wo):
    B, S, D = x.shape
    ts = TS_DENSE
    row = lambda d: pl.BlockSpec((1, ts, d), lambda b, i: (b, i, 0))
    return pl.pallas_call(
        _merge_kernel,
        out_shape=jax.ShapeDtypeStruct((B, S, D), F32),
        grid=(B, S // ts),
        in_specs=[row(D), row(D), row(MIX_WIDTH), row(MIX_WIDTH), row(MIX_WIDTH), row(MIX_WIDTH),
                  pl.BlockSpec((1, 1, 6, D), lambda b, i: (l, b, 0, 0)),
                  pl.BlockSpec((1, 4, MIX_WIDTH, D), lambda b, i: (l, 0, 0, 0)),
                  pl.BlockSpec((1, D, 4 * D), lambda b, i: (l, 0, 0)),
                  pl.BlockSpec((1, 1, 4 * D), lambda b, i: (l, 0, 0)),
                  pl.BlockSpec((1, D, D), lambda b, i: (l, 0, 0))],
        out_specs=row(D),
        compiler_params=pltpu.CompilerParams(vmem_limit_bytes=VMEM_LIMIT),
        name="merge_out",
    )(x, hbf, o_nsa, o_fox, o_mla, o_diff, mod, brw, gw, gb, wo)


def _ffn_kernel(n_s, xn_ref, modn_ref, xp_ref, modp_ref, wup_ref, cw_ref, cb_ref, wd_ref, out_ref,
                hb_scr, carry_ref):
    n = pl.program_id(0)
    ts = xn_ref.shape[1]

    def norm_next():
        hb_scr[n % 2] = _modulated_norm(xn_ref[0], modn_ref, 3).astype(BF16)

    def init():
        carry_ref[...] = jnp.zeros_like(carry_ref)

    def process_prev():
        keep = jnp.where((n - 1) % n_s == 0, 0.0, 1.0)
        hb = hb_scr[(n + 1) % 2]
        row = lax.broadcasted_iota(jnp.int32, (ts, 1), 0)
        acc = None
        for c in range(D_FF // FF_CHUNK):
            cols = slice(c * FF_CHUNK, (c + 1) * FF_CHUNK)
            g = _dot(hb, wup_ref[0, :, cols])
            v = _dot(hb, wup_ref[0, :, D_FF + c * FF_CHUNK:D_FF + (c + 1) * FF_CHUNK])
            prev = carry_ref[c] * keep
            g1 = jnp.where(row == 0, prev[7:8, :], pltpu.roll(g, 1, 0))
            g2 = jnp.where(row == 0, prev[6:7, :], jnp.where(row == 1, prev[7:8, :], pltpu.roll(g, 2, 0)))
            carry_ref[c] = g[ts - 8:ts, :]
            conv = (cw_ref[0, 0:1, cols] * g2 + cw_ref[0, 1:2, cols] * g1 + cw_ref[0, 2:3, cols] * g
                    + cb_ref[0, :, cols])
            a = (_silu(conv) * v).astype(BF16)
            part = _dot(a, wd_ref[0, cols, :])
            acc = part if acc is None else acc + part
        out_ref[0] = xp_ref[0] + modp_ref[0, 0, 5:6, :] * acc

    _staggered(n, pl.num_programs(0) - 1, norm_next, process_prev, init)


def _ffn(l, x, mod, wup, cw, cb, wd):
    B, S, D = x.shape
    ts = TS_DENSE
    const = pl.Buffered(1)
    n_s = S // ts
    n_tiles = B * n_s
    nxt = lambda n: jnp.minimum(n, n_tiles - 1)
    prv = lambda n: jnp.maximum(n - 1, 0)
    tile = lambda f: pl.BlockSpec((1, ts, D), lambda n: (f(n) // n_s, f(n) % n_s, 0))
    mods = lambda f: pl.BlockSpec((1, 1, 6, D), lambda n: (l, f(n) // n_s, 0, 0))
    return pl.pallas_call(
        functools.partial(_ffn_kernel, n_s),
        out_shape=jax.ShapeDtypeStruct((B, S, D), F32),
        grid=(n_tiles + 1,),
        in_specs=[tile(nxt), mods(nxt), tile(prv), mods(prv),
                  pl.BlockSpec((1, D, 2 * D_FF), lambda n: (l, 0, 0), pipeline_mode=const),
                  pl.BlockSpec((1, 3, D_FF), lambda n: (l, 0, 0)),
                  pl.BlockSpec((1, 1, D_FF), lambda n: (l, 0, 0)),
                  pl.BlockSpec((1, D_FF, D), lambda n: (l, 0, 0), pipeline_mode=const)],
        out_specs=tile(prv),
        scratch_shapes=[pltpu.VMEM((2, ts, D), BF16), pltpu.VMEM((D_FF // FF_CHUNK, 8, FF_CHUNK), F32)],
        compiler_params=pltpu.CompilerParams(dimension_semantics=("arbitrary",),
                                             vmem_limit_bytes=VMEM_LIMIT),
        name="conv_ffn",
    )(x, mod, x, mod, wup, cw, cb, wd)


def _pad_cols(w, n):
    return jnp.pad(w, [(0, 0)] * (w.ndim - 1) + [(0, n - w.shape[-1])])


def _layout_w_in(w_in):
    o = 0
    seg = {}
    for name, n in (("nq", 256), ("nkc", 64), ("nvc", 64), ("nks", 64), ("nvs", 64), ("nkw", 64), ("nvw", 64),
                    ("ng", 12), ("fq", 256), ("fk", 256), ("fv", 256), ("ff", 4),
                    ("mcq", 256), ("mckv", 128), ("mkr", 32), ("dq", 256), ("dk", 256), ("dv", 256)):
        seg[name] = w_in[..., o:o + n]
        o += n
    parts = [seg["nq"], seg["nks"], seg["nkw"], seg["nkc"], seg["nvc"],
             _pad_cols(seg["ng"], LANES), _pad_cols(seg["ff"], LANES),
             seg["fq"], seg["fk"],
             seg["mcq"], seg["mckv"], _pad_cols(seg["mkr"], LANES),
             seg["dq"], seg["dk"]]
    w1 = jnp.concatenate(parts, axis=-1).astype(BF16)
    wvt = jnp.concatenate([seg["nvs"], seg["nvw"], seg["fv"], seg["dv"]], axis=-1)
    return w1, jnp.swapaxes(wvt, -1, -2).astype(BF16)


def _pack_vecs(nsa_qk_g, fox_qk_g, fox_f_b, mla_cq_g, mla_ckv_g, mla_qk_g, diff_qk_g):
    L = nsa_qk_g.shape[0]
    t4 = lambda g: jnp.tile(g, (1, 256 // g.shape[-1]))
    mla_pad = lambda g: _pad_cols(g, LANES)
    rows = [None] * N_VEC
    rows[V_NQ] = t4(nsa_qk_g[:, 0]) * (LOG2E * D_HEAD ** -0.5)
    rows[V_NK] = jnp.concatenate([nsa_qk_g[:, 2], nsa_qk_g[:, 3]], axis=-1)
    rows[V_FQ] = t4(fox_qk_g[:, 0]) * (LOG2E * D_HEAD ** -0.5)
    rows[V_FK] = t4(fox_qk_g[:, 1])
    rows[V_MQ] = t4(mla_pad(mla_qk_g[:, 0])) * (LOG2E * MLA_QK ** -0.5)
    rows[V_MK] = t4(mla_pad(mla_qk_g[:, 1]))
    rows[V_DQ] = t4(diff_qk_g[:, 0]) * (LOG2E * DIFF_QK ** -0.5)
    rows[V_DK] = t4(diff_qk_g[:, 1])
    rows[V_FB] = fox_f_b
    rows[V_CQG] = mla_cq_g
    rows[V_CKVG] = mla_ckv_g
    rows = [jnp.zeros((L, 256), F32) if r is None else _pad_cols(r.astype(F32), 256) for r in rows]
    return jnp.stack(rows, axis=1)


def _block_diag_ones():
    d = np.arange(256)
    mats = [(d[:, None] // g == d[None, :] // g) for g in (D_HEAD, LANES, DIFF_QK)]
    return jnp.asarray(np.stack(mats), dtype=BF16)


def _overlap_matrix(n_cmp_rows, n_blk):
    c = np.arange(n_cmp_rows)[None, :] * CMP_STRIDE
    b = np.arange(n_blk)[:, None] * SEL_BLOCK
    ov = (c < b + SEL_BLOCK) & (c + CMP_BLOCK > b)
    return jnp.asarray(ov, dtype=BF16)


def kernel(x, c, positions, ada_w, ada_b, w_in, nsa_qk_g, nsa_cmp_pe, nsa_cmp_w1, nsa_cmp_w2, fox_qk_g, fox_f_b,
           mla_cq_g, mla_ckv_g, mla_w_uq, mla_w_ukv, mla_qk_g, diff_qk_g, diff_lambda, diff_out_g, br_w, gate_w,
           gate_b, w_out, ffn_w_up, ffn_conv_w, ffn_conv_b, ffn_w_down):
    B, S, D = x.shape
    L = ada_w.shape[0]
    H = N_HEADS
    n_half = S // CMP_STRIDE

    mod = _modulation(c, ada_w, ada_b).reshape(L, B, 6, D)
    tab = _rope_tables(positions)

    w1, wvt = _layout_w_in(w_in)
    vecs = _pack_vecs(nsa_qk_g, fox_qk_g, fox_f_b, mla_cq_g, mla_ckv_g, mla_qk_g, diff_qk_g)
    bd = _block_diag_ones()
    tri = jnp.asarray(np.tril(np.ones((TS_PROJ, TS_PROJ), np.float32)), dtype=BF16)
    overlap = _overlap_matrix(n_half, S // SEL_BLOCK)
    wuq = _pad_cols(mla_w_uq.reshape(L, MLA_Q_RANK, H, MLA_QK), LANES).reshape(L, MLA_Q_RANK, H * LANES)
    ukv = mla_w_ukv.reshape(L, MLA_KV_RANK, H, MLA_NOPE + D_HEAD)
    wuk = jnp.pad(ukv[..., :MLA_NOPE], [(0, 0), (0, 0), (0, 0), (MLA_ROPE, LANES - MLA_QK)])
    wuk = wuk.reshape(L, MLA_KV_RANK, H * LANES).astype(BF16)
    wuvt = jnp.swapaxes(ukv[..., MLA_NOPE:].reshape(L, MLA_KV_RANK, H * D_HEAD), 1, 2).astype(BF16)
    wuq = wuq.astype(BF16)
    pe = nsa_cmp_pe.reshape(L, 2, CMP_BLOCK * D_HEAD)
    cw1 = nsa_cmp_w1.astype(BF16)
    cw2 = jnp.stack([nsa_cmp_w2[:, 0], jnp.swapaxes(nsa_cmp_w2[:, 1], 1, 2)], axis=1).astype(BF16)
    kcg = _pad_cols(nsa_qk_g[:, 1], LANES).reshape(L, 1, LANES)
    brw = br_w.astype(BF16)
    gw = gate_w.astype(BF16)
    gb = gate_b.reshape(L, 1, 4 * D)
    wo = w_out.astype(BF16)
    wup = ffn_w_up.astype(BF16)
    wd = ffn_w_down.astype(BF16)
    cb = ffn_conv_b.reshape(L, 1, D_FF)
    og = diff_out_g.reshape(L, D_HEAD, 1)

    for l in range(L):
        lam_init = 0.8 - 0.6 * math.exp(-0.3 * l)
        (hbf, nq, nk, nv, nkc, nvc, ng, fq, fk, fv, fkb, mq, mk, mv, dq, dk, dv) = _projection(
            l, x, mod, tab, w1, wvt, vecs, bd, tri, wuq, wuk, wuvt)
        kcmp, vcmp = _compress(l, nkc, nvc, pe, cw1, cw2, kcg)
        o_nsa = _nsa_attention(nq, kcmp, vcmp, nk, nv, ng, overlap)
        o_fox = _dense_attention("fox", l, lam_init, fq, fk, fv, fkb)
        o_mla = _dense_attention("mla", l, lam_init, mq, mk, mv)
        o_diff = _dense_attention("diff", l, lam_init, dq, dk, dv, diff_lambda, og)
        x = _merge(l, x, hbf, o_nsa, o_fox, o_mla, o_diff, mod, brw, gw, gb, wo)
        x = _ffn(l, x, mod, wup, ffn_conv_w, cb, wd)
    return x
```

```python
import functools
import math

import jax
import jax.numpy as jnp
import numpy as np
from jax import lax
from jax.experimental import pallas as pl
from jax.experimental.pallas import tpu as pltpu

F32 = jnp.float32
BF16 = jnp.bfloat16

D_MODEL = 1024
D_HEAD = 64
N_HEADS = 4
MIX_WIDTH = 256
NSA_ROT = 16
CMP_BLOCK = 32
CMP_STRIDE = 16
SEL_BLOCK = 64
SEL_TOPN = 16
WINDOW = 512
MLA_Q_RANK = 256
MLA_KV_RANK = 128
MLA_NOPE = 64
MLA_ROPE = 32
MLA_QK = MLA_ROPE + MLA_NOPE
DIFF_QK = 32
DIFF_ROT = 8
D_FF = 2816
ROPE_THETA = 500000.0
EPS = 1e-6
FORCE_SCORE = 1e4
NEG_BIG = -1e30

LANES = 128
V_ROWS = 80
VMEM_LIMIT = 56 * 1024 * 1024

C_NQ, C_NKSW, C_NKVC, C_NG, C_FF = 0, 256, 384, 512, 640
C_FQ, C_FK = 768, 1024
C_MCQ, C_MCKV, C_MKR = 1280, 1536, 1664
C_DQ, C_DK = 1792, 2048
W_IN_PAD = 2304
R_NV, R_FV, R_DV = 0, 128, 384
W_VT_ROWS = 640

V_NQ, V_NK, V_FQ, V_FK, V_MQ, V_MK, V_DQ, V_DK, V_FB, V_CQG, V_CKVG = range(11)
N_VEC = 16

TS_PROJ = 512
TS_DENSE = 512
TQ_ATT = 512
TQ_NSA = 256
TK_NSA = 256
FF_CHUNK = 2816
LOG2E = math.log2(math.e)


def _dot(a, b):
    return jnp.dot(a, b, preferred_element_type=F32)


def _dot_nt(a, b):
    return lax.dot_general(a, b, (((1,), (1,)), ((), ())), preferred_element_type=F32)


def _split3(x):
    hi = x.astype(BF16)
    r = x - hi.astype(F32)
    mid = r.astype(BF16)
    lo = (r - mid.astype(F32)).astype(BF16)
    return hi, mid, lo


def _rms_rows(x):
    return x * lax.rsqrt(jnp.mean(x * x, axis=-1, keepdims=True) + EPS)


def _modulated_norm(x, mod_ref, r):
    return _rms_rows(x) * (1.0 + mod_ref[0, 0, r + 1:r + 2, :]) + mod_ref[0, 0, r:r + 1, :]


def _group_rms(x, bd, inv_n):
    ss = _dot((x * x).astype(BF16), bd)
    return x * lax.rsqrt(ss * inv_n + EPS)


def _rope(xc, tab_ref, cfg, half):
    c = tab_ref[3 * cfg, 0]
    sa = tab_ref[3 * cfg + 1, 0]
    sb = tab_ref[3 * cfg + 2, 0]
    return xc * c + pltpu.roll(xc, LANES - half, 1) * sa + pltpu.roll(xc, half, 1) * sb


def _log_sigmoid(x):
    return jnp.minimum(x, 0.0) - jnp.log1p(jnp.exp(-jnp.abs(x)))


def _sigmoid(x):
    return 1.0 / (1.0 + jnp.exp(-x))


def _silu(x):
    return x * _sigmoid(x)


def _softmax_step(st, v1t, m, acc):
    m_new = jnp.maximum(m, jnp.max(st, axis=0, keepdims=True))
    p = jnp.exp2(st - m_new).astype(BF16)
    return m_new, jnp.exp2(m - m_new) * acc + _dot(v1t, p)


def _softmax_finish(acc):
    return acc[:D_HEAD] / acc[D_HEAD:D_HEAD + 1]


def _staggered(n, n_tiles, norm_next, process_prev, init):
    @pl.when(n == 0)
    def _():
        init()
        norm_next()

    @pl.when((n > 0) & (n < n_tiles))
    def _():
        norm_next()
        process_prev()

    @pl.when(n == n_tiles)
    def _():
        process_prev()


def _mod_kernel(c_ref, w_ref, b_ref, o_ref):
    c = c_ref[...]
    a = _silu(c).astype(BF16)
    o_ref[0] = _dot(a, w_ref[0].astype(BF16)) + b_ref[0]


def _modulation(c, ada_w, ada_b):
    L, D, N = ada_w.shape
    B = c.shape[0]
    tn = 1536
    return pl.pallas_call(
        _mod_kernel,
        out_shape=jax.ShapeDtypeStruct((L, B, N), F32),
        grid=(L, N // tn),
        in_specs=[pl.BlockSpec((B, D), lambda l, n: (0, 0)),
                  pl.BlockSpec((1, D, tn), lambda l, n: (l, 0, n)),
                  pl.BlockSpec((1, 1, tn), lambda l, n: (l, 0, n))],
        out_specs=pl.BlockSpec((1, B, tn), lambda l, n: (l, 0, n)),
        compiler_params=pltpu.CompilerParams(vmem_limit_bytes=VMEM_LIMIT),
        name="adaln_mod",
    )(c, ada_w, ada_b.reshape(L, 1, N))


_ROPE_CFGS = ((D_HEAD, NSA_ROT, 32), (LANES, MLA_ROPE, 0), (DIFF_QK, DIFF_ROT, 48))


def _rope_tab_kernel(pos_ref, frq_ref, o_ref):
    ang = pos_ref[0].astype(F32) * frq_ref[...]
    cs = jnp.cos(ang)
    sn = jnp.sin(ang)
    lane = lax.broadcasted_iota(jnp.int32, (1, LANES), 1)
    for cfg, (group, n_rot, src) in enumerate(_ROPE_CFGS):
        def spread(x):
            out = None
            for gi in range(LANES // group):
                shift = (gi * group - src) % LANES
                xs = x if shift == 0 else pltpu.roll(x, shift, 1)
                out = xs if out is None else jnp.where(lane >= gi * group, xs, out)
            return out

        d = lane % group
        c = spread(cs)
        s = spread(sn)
        o_ref[3 * cfg, 0] = jnp.where(d < n_rot, c, 1.0)
        o_ref[3 * cfg + 1, 0] = jnp.where(d < n_rot // 2, -s, 0.0)
        o_ref[3 * cfg + 2, 0] = jnp.where((d >= n_rot // 2) & (d < n_rot), s, 0.0)


def _rope_consts():
    row = jnp.zeros((LANES,), F32)
    for group, n_rot, src in _ROPE_CFGS:
        half = n_rot // 2
        inv_freq = ROPE_THETA ** (-jnp.arange(half, dtype=F32) / half)
        row = row.at[src:src + n_rot].set(jnp.tile(inv_freq, 2))
    return row.reshape(1, LANES)


def _rope_tables(positions):
    B, S = positions.shape
    ts = 512
    return pl.pallas_call(
        _rope_tab_kernel,
        out_shape=jax.ShapeDtypeStruct((9, B, S, LANES), F32),
        grid=(B, S // ts),
        in_specs=[pl.BlockSpec((1, ts, 1), lambda b, i: (b, i, 0)),
                  pl.BlockSpec((1, LANES), lambda b, i: (0, 0))],
        out_specs=pl.BlockSpec((9, 1, ts, LANES), lambda b, i: (0, b, i, 0)),
        name="rope_tables",
    )(positions.reshape(B, S, 1), _rope_consts())


def _proj_kernel(n_s, xn_ref, modn_ref, tab_ref, w_ref, wvt_ref, vec_ref, bd_ref, tri_ref, wuq_ref, wuk_ref,
                 wuvt_ref, hbf_ref, nq_ref, nk_ref, nv_ref, nkc_ref, nvc_ref, ng_ref,
                 fq_ref, fk_ref, fv_ref, fkb_ref,
                 mq_ref, mk_ref, mv_ref, dq_ref, dk_ref, dv_ref, hb_scr, carry_ref):
    n = pl.program_id(0)
    ts = xn_ref.shape[1]

    def norm_next():
        hb_scr[n % 2] = _modulated_norm(xn_ref[0], modn_ref, 0).astype(BF16)

    def init():
        carry_ref[...] = jnp.zeros_like(carry_ref)

    def process_prev():
        i = (n - 1) % n_s
        keep = jnp.where(i == 0, 0.0, 1.0)
        hb = hb_scr[(n + 1) % 2]
        hbf_ref[0] = hb

        def proj(c0, width):
            return _dot(hb, w_ref[0, :, c0:c0 + width])

        def vec(r, width=LANES):
            return vec_ref[0, r:r + 1, 0:width]

        bd64, bd128, bd32 = bd_ref[0], bd_ref[1], bd_ref[2]
        row = lax.broadcasted_iota(jnp.int32, (ts, 1), 0)
        lane = lax.broadcasted_iota(jnp.int32, (1, LANES), 1)

        def head_pair(seg, gain_row, bd, inv_n, cfg, half, out_ref):
            yn = _group_rms(seg, bd, inv_n) * vec(gain_row, 256)
            for c in range(2):
                y = yn[:, c * LANES:(c + 1) * LANES]
                if cfg is not None:
                    y = _rope(y, tab_ref, cfg, half)
                yb = y.astype(BF16)
                out_ref[0, 2 * c] = yb[:, :D_HEAD]
                out_ref[0, 2 * c + 1] = yb[:, D_HEAD:]

        ones_rows = jnp.where(lax.broadcasted_iota(jnp.int32, (V_ROWS - D_HEAD, ts), 0) == 0, 1.0, 0.0).astype(BF16)

        def value_store(vt, out_ref):
            for hd in range(vt.shape[0] // D_HEAD):
                out_ref[0, hd] = jnp.concatenate([vt[hd * D_HEAD:(hd + 1) * D_HEAD].astype(BF16), ones_rows], axis=0)

        head_pair(proj(C_NQ, 256), V_NQ, bd64, 1.0 / D_HEAD, 0, NSA_ROT // 2, nq_ref)
        nsa_kv = proj(C_NKSW, 2 * LANES)
        ksw = _group_rms(nsa_kv[:, :LANES], bd64[0:LANES, 0:LANES], 1.0 / D_HEAD) * vec(V_NK)
        ksw = _rope(ksw, tab_ref, 0, NSA_ROT // 2)
        blk_hot = jnp.where(lane - D_HEAD == (i * ts + row) // SEL_BLOCK, 1.0, 0.0)
        nk_ref[0, 0] = jnp.where(lane < D_HEAD, ksw, blk_hot).astype(BF16)
        nk_ref[0, 1] = jnp.where(lane < D_HEAD, pltpu.roll(ksw, D_HEAD, 1), 0.0).astype(BF16)
        kvc = nsa_kv[:, LANES:]
        nkc_ref[0] = kvc[:, :D_HEAD]
        nvc_ref[0] = kvc[:, D_HEAD:]
        gates = proj(C_NG, 2 * LANES)
        ng_ref[0] = _sigmoid(gates[:, :LANES])

        head_pair(proj(C_FQ, 256), V_FQ, bd64, 1.0 / D_HEAD, None, 0, fq_ref)
        head_pair(proj(C_FK, 256), V_FK, bd64, 1.0 / D_HEAD, None, 0, fk_ref)

        lf = _log_sigmoid(gates[:, LANES:] + vec(V_FB))
        tri = tri_ref[...]
        p0, p1, p2 = _split3(lf)
        c01 = _dot(tri, jnp.concatenate([p0, p1], axis=1))
        cum = c01[:, :LANES] + c01[:, LANES:] + _dot(tri, p2) + carry_ref[0:1, :] * keep
        carry_ref[0:1, :] = cum[ts - 1:ts, :]
        fkb_ref[0] = cum * (-LOG2E)

        cq = proj(C_MCQ, MLA_Q_RANK)
        cqn = (_rms_rows(cq) * vec(V_CQG, MLA_Q_RANK)).astype(BF16)
        qm = _dot(cqn, wuq_ref[0])
        mla_kv = proj(C_MCKV, 2 * LANES)
        ckvn = (_rms_rows(mla_kv[:, :MLA_KV_RANK]) * vec(V_CKVG)).astype(BF16)
        kk = _dot(ckvn, wuk_ref[0])
        vvt = _dot_nt(wuvt_ref[0], ckvn)
        kr = mla_kv[:, MLA_KV_RANK:]
        kr2 = jnp.concatenate([kr, kr], axis=1)
        for pair in range(N_HEADS // 2):
            sl = slice(pair * 256, (pair + 1) * 256)
            yq = _group_rms(qm[:, sl], bd128, 1.0 / MLA_QK) * vec(V_MQ, 256)
            yk = _group_rms(kk[:, sl] + kr2, bd128, 1.0 / MLA_QK) * vec(V_MK, 256)
            for c in range(2):
                cs = slice(c * LANES, (c + 1) * LANES)
                mq_ref[0, 2 * pair + c] = _rope(yq[:, cs], tab_ref, 1, MLA_ROPE // 2).astype(BF16)
                mk_ref[0, 2 * pair + c] = _rope(yk[:, cs], tab_ref, 1, MLA_ROPE // 2).astype(BF16)

        head_pair(proj(C_DQ, 256), V_DQ, bd32, 1.0 / DIFF_QK, 2, DIFF_ROT // 2, dq_ref)
        head_pair(proj(C_DK, 256), V_DK, bd32, 1.0 / DIFF_QK, 2, DIFF_ROT // 2, dk_ref)

        value_store(vvt, mv_ref)
        vt_all = _dot_nt(wvt_ref[0], hb)
        value_store(vt_all[R_NV:R_FV], nv_ref)
        value_store(vt_all[R_FV:R_DV], fv_ref)
        value_store(vt_all[R_DV:W_VT_ROWS], dv_ref)

    _staggered(n, pl.num_programs(0) - 1, norm_next, process_prev, init)


def _projection(l, x, mod, tab, w1, wvt, vecs, bd, tri, wuq, wuk, wuvt):
    B, S, D = x.shape
    ts = TS_PROJ
    H = N_HEADS
    n_s = S // ts
    n_tiles = B * n_s
    nxt = lambda n: jnp.minimum(n, n_tiles - 1)
    prv = lambda n: jnp.maximum(n - 1, 0)
    hm = lambda d, dt: jax.ShapeDtypeStruct((B, H, S, d), dt)
    hm_spec = lambda nh, d: pl.BlockSpec((1, nh, ts, d), lambda n: (prv(n) // n_s, 0, prv(n) % n_s, 0))
    row_spec = lambda d: pl.BlockSpec((1, ts, d), lambda n: (prv(n) // n_s, prv(n) % n_s, 0))
    vt = lambda nh: jax.ShapeDtypeStruct((B, nh, V_ROWS, S), BF16)
    vt_spec = lambda nh: pl.BlockSpec((1, nh, V_ROWS, ts), lambda n: (prv(n) // n_s, 0, 0, prv(n) % n_s))
    out_shape = [
        jax.ShapeDtypeStruct((B, S, D), BF16),
        hm(D_HEAD, BF16),
        jax.ShapeDtypeStruct((B, 2, S, LANES), BF16),
        vt(2),
        jax.ShapeDtypeStruct((B, S, D_HEAD), F32),
        jax.ShapeDtypeStruct((B, S, D_HEAD), F32),
        jax.ShapeDtypeStruct((B, S, LANES), F32),
        hm(D_HEAD, BF16), hm(D_HEAD, BF16), vt(H),
        jax.ShapeDtypeStruct((B, S, LANES), F32),
        hm(LANES, BF16), hm(LANES, BF16), vt(H),
        hm(D_HEAD, BF16), hm(D_HEAD, BF16), vt(H),
    ]
    out_specs = [
        row_spec(D), hm_spec(H, D_HEAD), hm_spec(2, LANES), vt_spec(2),
        row_spec(D_HEAD), row_spec(D_HEAD), row_spec(LANES),
        hm_spec(H, D_HEAD), hm_spec(H, D_HEAD), vt_spec(H),
        row_spec(LANES),
        hm_spec(H, LANES), hm_spec(H, LANES), vt_spec(H),
        hm_spec(H, D_HEAD), hm_spec(H, D_HEAD), vt_spec(H),
    ]
    in_specs = [
        pl.BlockSpec((1, ts, D), lambda n: (nxt(n) // n_s, nxt(n) % n_s, 0)),
        pl.BlockSpec((1, 1, 6, D), lambda n: (l, nxt(n) // n_s, 0, 0)),
        pl.BlockSpec((9, 1, ts, LANES), lambda n: (0, prv(n) // n_s, prv(n) % n_s, 0)),
        pl.BlockSpec((1, D, W_IN_PAD), lambda n: (l, 0, 0)),
        pl.BlockSpec((1, W_VT_ROWS, D), lambda n: (l, 0, 0)),
        pl.BlockSpec((1, N_VEC, 256), lambda n: (l, 0, 0)),
        pl.BlockSpec((3, 256, 256), lambda n: (0, 0, 0)),
        pl.BlockSpec((ts, ts), lambda n: (0, 0)),
        pl.BlockSpec((1, MLA_Q_RANK, 512), lambda n: (l, 0, 0)),
        pl.BlockSpec((1, MLA_KV_RANK, 512), lambda n: (l, 0, 0)),
        pl.BlockSpec((1, 256, MLA_KV_RANK), lambda n: (l, 0, 0)),
    ]
    return pl.pallas_call(
        functools.partial(_proj_kernel, n_s),
        out_shape=out_shape,
        grid=(n_tiles + 1,),
        in_specs=in_specs,
        out_specs=out_specs,
        scratch_shapes=[pltpu.VMEM((2, ts, D), BF16), pltpu.VMEM((8, LANES), F32)],
        compiler_params=pltpu.CompilerParams(dimension_semantics=("arbitrary",),
                                             vmem_limit_bytes=VMEM_LIMIT),
        name="in_proj",
    )(x, mod, tab, w1, wvt, vecs, bd, tri, wuq, wuk, wuvt)


def _cmp_kernel(kc_ref, vc_ref, pe_ref, w1_ref, w2_ref, g_ref, ko_ref, vo_ref, x_scr):
    half = CMP_STRIDE * D_HEAD
    n = kc_ref.shape[1] // CMP_STRIDE

    def hidden(src_ref, j):
        for p in range(CMP_STRIDE):
            x_scr[:, p * D_HEAD:(p + 1) * D_HEAD] = src_ref[0, pl.ds(p, n, stride=CMP_STRIDE), :]
        x2 = x_scr[...]
        xa = _dot((x2 + pe_ref[0, j:j + 1, 0:half]).astype(BF16), w1_ref[0, j, 0:half, :])
        xb = _dot((x2 + pe_ref[0, j:j + 1, half:2 * half]).astype(BF16), w1_ref[0, j, half:2 * half, :])
        return _silu(xa + pltpu.roll(xb, n - 1, 0)).astype(BF16)

    kcmp = _rms_rows(_dot(hidden(kc_ref, 0), w2_ref[0, 0])) * g_ref[0, 0:1, 0:D_HEAD]
    ko_ref[0] = kcmp.astype(BF16)
    vo_ref[0] = _dot_nt(w2_ref[0, 1], hidden(vc_ref, 1)).astype(BF16)


def _compress(l, kc, vc, pe, w1, w2, g):
    B, S, _ = kc.shape
    n = S // CMP_STRIDE
    wide = CMP_STRIDE * D_HEAD
    return pl.pallas_call(
        _cmp_kernel,
        out_shape=[jax.ShapeDtypeStruct((B, n, D_HEAD), BF16), jax.ShapeDtypeStruct((B, D_HEAD, n), BF16)],
        grid=(B,),
        in_specs=[pl.BlockSpec((1, S, D_HEAD), lambda b: (b, 0, 0)),
                  pl.BlockSpec((1, S, D_HEAD), lambda b: (b, 0, 0)),
                  pl.BlockSpec((1, 2, 2 * wide), lambda b: (l, 0, 0)),
                  pl.BlockSpec((1, 2, 2 * wide, D_HEAD), lambda b: (l, 0, 0, 0)),
                  pl.BlockSpec((1, 2, D_HEAD, D_HEAD), lambda b: (l, 0, 0, 0)),
                  pl.BlockSpec((1, 1, LANES), lambda b: (l, 0, 0))],
        out_specs=[pl.BlockSpec((1, n, D_HEAD), lambda b: (b, 0, 0)),
                   pl.BlockSpec((1, D_HEAD, n), lambda b: (b, 0, 0))],
        scratch_shapes=[pltpu.VMEM((n, wide), F32)],
        name="nsa_compress",
    )(kc, vc, pe, w1, w2, g)


def _nsa_kernel(q_ref, kc_ref, vc_ref, k_ref, v_ref, g_ref, ov_ref, o_ref):
    i = pl.program_id(1)
    tq, tk = TQ_NSA, TK_NSA
    H = N_HEADS
    rows = H * tq
    n_blk = ov_ref.shape[0]
    q = q_ref[0].reshape(rows, D_HEAD)
    t4 = i * tq + lax.broadcasted_iota(jnp.int32, (1, rows), 1) % tq

    nc = kc_ref.shape[1]
    sc = _dot_nt(kc_ref[0], q)
    cend = lax.broadcasted_iota(jnp.int32, (nc, 1), 0) * CMP_STRIDE + (CMP_BLOCK - 1)
    sc = jnp.where(cend <= t4, sc, NEG_BIG)
    e = jnp.exp2(sc - jnp.max(sc, axis=0, keepdims=True))
    p = e / jnp.sum(e, axis=0, keepdims=True)
    p = jnp.where(t4 >= CMP_BLOCK - 1, p, 0.0)
    o_cmp = _dot(vc_ref[0], p.astype(BF16))
    psum = p[:, 0:tq] + p[:, tq:2 * tq] + p[:, 2 * tq:3 * tq] + p[:, 3 * tq:4 * tq]
    p0, p1, p2 = _split3(psum)
    ov = ov_ref[...]
    imp = _dot(ov, p0) + _dot(ov, p1) + _dot(ov, p2)

    blk = lax.broadcasted_iota(jnp.int32, (n_blk, 1), 0)
    tl = i * tq + lax.broadcasted_iota(jnp.int32, (1, tq), 1)
    cur = tl // SEL_BLOCK
    forced = (blk == 0) | (blk == cur) | (blk == cur - 1)
    score = jnp.where(blk * SEL_BLOCK > tl, -1.0, jnp.where(forced, FORCE_SCORE, imp))
    rank = jnp.zeros((n_blk, tq), F32)
    for r in range(n_blk):
        other = score[r:r + 1, :]
        ahead = (other > score) | ((other == score) & (blk > r))
        rank = rank + jnp.where(ahead, 1.0, 0.0)
    drop_t = jnp.where(rank < float(min(SEL_TOPN, n_blk)), 0.0, NEG_BIG)
    drop = jnp.concatenate([drop_t, jnp.zeros((LANES - n_blk, tq), F32)], axis=0).T.astype(BF16)
    qx = jnp.concatenate([q, jnp.concatenate([drop[:, 0:D_HEAD]] * H, axis=0)], axis=1)

    def slc_scores(j):
        return _dot_nt(k_ref[0, 0, pl.ds(j * tk, tk), :], qx)

    init = (jnp.full((1, rows), NEG_BIG, F32), jnp.zeros((V_ROWS, rows), F32))

    def far_step(j, carry):
        return _softmax_step(slc_scores(j), v_ref[0, 0, :, pl.ds(j * tk, tk)], *carry)

    n_near = WINDOW // tk
    kl = lax.broadcasted_iota(jnp.int32, (tk, 1), 0)
    rl = lax.broadcasted_iota(jnp.int32, (1, rows), 1) % tq

    def near_step(rel, carry):
        j = i - rel
        s_s = slc_scores(j)
        s_w = _dot_nt(k_ref[0, 1, pl.ds(j * tk, tk), :], qx)
        if rel == 0:
            s_s = jnp.where(kl <= rl, s_s, NEG_BIG)
            s_w = jnp.where(kl <= rl, s_w, NEG_BIG)
        elif rel == n_near:
            s_w = jnp.where(kl > rl, s_w, NEG_BIG)
        c_s = _softmax_step(s_s, v_ref[0, 0, :, pl.ds(j * tk, tk)], *carry[0])
        c_w = _softmax_step(s_w, v_ref[0, 1, :, pl.ds(j * tk, tk)], *carry[1])
        return c_s, c_w

    lo = jnp.maximum(i - n_near, 0)
    carry = (lax.fori_loop(0, lo, far_step, init), init)
    for rel in range(n_near, 0, -1):
        carry = lax.cond(i >= rel, functools.partial(near_step, rel), lambda c: c, carry)
    (_, a_s), (_, a_w) = near_step(0, carry)
    o_slc = _softmax_finish(a_s)
    o_win = _softmax_finish(a_w)

    gt = g_ref[0].T
    outs = []
    for hd in range(H):
        r = slice(hd * tq, (hd + 1) * tq)
        outs.append(gt[hd:hd + 1] * o_cmp[:, r] + gt[H + hd:H + hd + 1] * o_slc[:, r]
                    + gt[2 * H + hd:2 * H + hd + 1] * o_win[:, r])
    o_ref[0] = jnp.concatenate(outs, axis=0).T.astype(BF16)


def _nsa_attention(q, kcmp, vcmp, ksw, vsw, gates, overlap):
    B, H, S, _ = q.shape
    tq = TQ_NSA
    nc = kcmp.shape[1]
    return pl.pallas_call(
        _nsa_kernel,
        out_shape=jax.ShapeDtypeStruct((B, S, MIX_WIDTH), BF16),
        grid=(B, S // tq),
        in_specs=[pl.BlockSpec((1, H, tq, D_HEAD), lambda b, i: (b, 0, i, 0)),
                  pl.BlockSpec((1, nc, D_HEAD), lambda b, i: (b, 0, 0)),
                  pl.BlockSpec((1, D_HEAD, nc), lambda b, i: (b, 0, 0)),
                  pl.BlockSpec((1, 2, S, LANES), lambda b, i: (b, 0, 0, 0)),
                  pl.BlockSpec((1, 2, V_ROWS, S), lambda b, i: (b, 0, 0, 0)),
                  pl.BlockSpec((1, tq, LANES), lambda b, i: (b, i, 0)),
                  pl.BlockSpec(overlap.shape, lambda b, i: (0, 0))],
        out_specs=pl.BlockSpec((1, tq, MIX_WIDTH), lambda b, i: (b, i, 0)),
        compiler_params=pltpu.CompilerParams(vmem_limit_bytes=VMEM_LIMIT),
        name="nsa_attention",
    )(q, kcmp, vcmp, ksw, vsw, gates, overlap)


def _foxmla_kernel(fq_ref, fk_ref, fv_ref, kb_ref, mq_ref, mk_ref, mv_ref, of_ref, om_ref):
    i = pl.program_id(1)
    t = TQ_ATT
    H = N_HEADS
    chains = [(fq_ref, fk_ref, fv_ref, hd, True) for hd in range(H)] + \
             [(mq_ref, mk_ref, mv_ref, hd, False) for hd in range(H)]
    qs = [q_ref[0, hd] for q_ref, _, _, hd, _ in chains]

    def step(j, carry, mask=None):
        out = []
        for c, (_, k_ref, v_ref, hd, biased) in enumerate(chains):
            s = _dot_nt(k_ref[0, hd, pl.ds(j * t, t), :], qs[c])
            if biased:
                s = s + kb_ref[0, pl.ds(j * t, t), :][:, hd:hd + 1]
            if mask is not None:
                s = jnp.where(mask, s, NEG_BIG)
            out.append(_softmax_step(s, v_ref[0, hd, :, pl.ds(j * t, t)], *carry[c]))
        return tuple(out)

    init = tuple((jnp.full((1, t), NEG_BIG, F32), jnp.zeros((V_ROWS, t), F32)) for _ in chains)
    carry = lax.fori_loop(0, i, step, init)
    kk = lax.broadcasted_iota(jnp.int32, (t, 1), 0)
    rr = lax.broadcasted_iota(jnp.int32, (1, t), 1)
    carry = step(i, carry, mask=kk <= rr)
    outs = [_softmax_finish(acc) for _, acc in carry]
    of_ref[0] = jnp.concatenate(outs[:H], axis=0).T.astype(BF16)
    om_ref[0] = jnp.concatenate(outs[H:], axis=0).T.astype(BF16)


def _foxmla_attention(fq, fk, fv, fkb, mq, mk, mv):
    B, H, S, _ = fq.shape
    t = TQ_ATT
    qspec = lambda d: pl.BlockSpec((1, H, t, d), lambda b, i: (b, 0, i, 0))
    kspec = lambda d: pl.BlockSpec((1, H, S, d), lambda b, i: (b, 0, 0, 0))
    vspec = pl.BlockSpec((1, H, V_ROWS, S), lambda b, i: (b, 0, 0, 0))
    ospec = pl.BlockSpec((1, t, MIX_WIDTH), lambda b, i: (b, i, 0))
    return pl.pallas_call(
        _foxmla_kernel,
        out_shape=[jax.ShapeDtypeStruct((B, S, MIX_WIDTH), BF16)] * 2,
        grid=(B, S // t),
        in_specs=[qspec(D_HEAD), kspec(D_HEAD), vspec, pl.BlockSpec((1, S, LANES), lambda b, i: (b, 0, 0)),
                  qspec(LANES), kspec(LANES), vspec],
        out_specs=[ospec, ospec],
        compiler_params=pltpu.CompilerParams(vmem_limit_bytes=VMEM_LIMIT),
        name="foxmla_attention",
    )(fq, fk, fv, fkb, mq, mk, mv)


def _diff_kernel(lam_init, q_ref, k_ref, v_ref, lam_ref, og_ref, o_ref):
    i = pl.program_id(1)
    t = TQ_ATT
    H = N_HEADS
    lane = lax.broadcasted_iota(jnp.int32, (1, D_HEAD), 1)
    qs = []
    for hd in range(H):
        q = q_ref[0, hd]
        zero = jnp.zeros_like(q)
        qs.append(jnp.concatenate([jnp.where(lane < DIFF_QK, q, zero), jnp.where(lane >= DIFF_QK, q, zero)], axis=0))
    rows = 2 * t

    def step(j, carry, mask=None):
        out = []
        for hd in range(H):
            s = _dot_nt(k_ref[0, hd, pl.ds(j * t, t), :], qs[hd])
            if mask is not None:
                s = jnp.where(mask, s, NEG_BIG)
            out.append(_softmax_step(s, v_ref[0, hd, :, pl.ds(j * t, t)], *carry[hd]))
        return tuple(out)

    init = tuple((jnp.full((1, rows), NEG_BIG, F32), jnp.zeros((V_ROWS, rows), F32)) for _ in range(H))
    carry = lax.fori_loop(0, i, step, init)
    kk = lax.broadcasted_iota(jnp.int32, (t, 1), 0)
    rr = lax.broadcasted_iota(jnp.int32, (1, rows), 1) % t
    carry = step(i, carry, mask=kk <= rr)

    lv = lam_ref[0]
    lam = (jnp.exp(jnp.sum(lv[0:1] * lv[1:2], axis=-1, keepdims=True))
           - jnp.exp(jnp.sum(lv[2:3] * lv[3:4], axis=-1, keepdims=True)) + lam_init)
    outs = []
    for hd in range(H):
        o = _softmax_finish(carry[hd][1])
        o = o[:, 0:t] - lam * o[:, t:2 * t]
        outs.append(o * lax.rsqrt(jnp.mean(o * o, axis=0, keepdims=True) + EPS) * og_ref[0] * (1.0 - lam_init))
    o_ref[0] = jnp.concatenate(outs, axis=0).T.astype(BF16)


def _diff_attention(l, lam_init, q, k, v, lam_vec, out_g):
    B, H, S, dk = q.shape
    t = TQ_ATT
    return pl.pallas_call(
        functools.partial(_diff_kernel, lam_init),
        out_shape=jax.ShapeDtypeStruct((B, S, MIX_WIDTH), BF16),
        grid=(B, S // t),
        in_specs=[pl.BlockSpec((1, H, t, dk), lambda b, i: (b, 0, i, 0)),
                  pl.BlockSpec((1, H, S, dk), lambda b, i: (b, 0, 0, 0)),
                  pl.BlockSpec((1, H, V_ROWS, S), lambda b, i: (b, 0, 0, 0)),
                  pl.BlockSpec((1, 4, DIFF_QK), lambda b, i: (l, 0, 0)),
                  pl.BlockSpec((1, D_HEAD, 1), lambda b, i: (l, 0, 0))],
        out_specs=pl.BlockSpec((1, t, MIX_WIDTH), lambda b, i: (b, i, 0)),
        compiler_params=pltpu.CompilerParams(vmem_limit_bytes=VMEM_LIMIT),
        name="diff_attention",
    )(q, k, v, lam_vec, out_g)


def _merge_kernel(x_ref, h_ref, o0_ref, o1_ref, o2_ref, o3_ref, mod_ref, brw_ref, gw_ref, gb_ref, wo_ref,
                  out_ref):
    hb = h_ref[0]
    merged = None
    for m, o_ref in enumerate((o0_ref, o1_ref, o2_ref, o3_ref)):
        y = _dot(o_ref[0], brw_ref[0, m])
        cols = slice(m * D_MODEL, (m + 1) * D_MODEL)
        gate = _sigmoid(_dot(hb, gw_ref[0, :, cols]) + gb_ref[0, :, cols])
        merged = gate * y if merged is None else merged + gate * y
    out = _dot(merged.astype(BF16), wo_ref[0])
    out_ref[0] = x_ref[0] + mod_ref[0, 0, 2:3, :] * out


def _merge(l, x, hbf, o_nsa, o_fox, o_mla, o_diff, mod, brw, gw, gb, wo):
    B, S, D = x.shape
    ts = TS_DENSE
    row = lambda d: pl.BlockSpec((1, ts, d), lambda b, i: (b, i, 0))
    return pl.pallas_call(
        _merge_kernel,
        out_shape=jax.ShapeDtypeStruct((B, S, D), F32),
        grid=(B, S // ts),
        in_specs=[row(D), row(D), row(MIX_WIDTH), row(MIX_WIDTH), row(MIX_WIDTH), row(MIX_WIDTH),
                  pl.BlockSpec((1, 1, 6, D), lambda b, i: (l, b, 0, 0)),
                  pl.BlockSpec((1, 4, MIX_WIDTH, D), lambda b, i: (l, 0, 0, 0)),
                  pl.BlockSpec((1, D, 4 * D), lambda b, i: (l, 0, 0)),
                  pl.BlockSpec((1, 1, 4 * D), lambda b, i: (l, 0, 0)),
                  pl.BlockSpec((1, D, D), lambda b, i: (l, 0, 0))],
        out_specs=row(D),
        compiler_params=pltpu.CompilerParams(vmem_limit_bytes=VMEM_LIMIT),
        name="merge_out",
    )(x, hbf, o_nsa, o_fox, o_mla, o_diff, mod, brw, gw, gb, wo)


def _ffn_kernel(n_s, xn_ref, modn_ref, xp_ref, modp_ref, wup_ref, cw_ref, cb_ref, wd_ref, out_ref,
                hb_scr, carry_ref):
    n = pl.program_id(0)
    ts = xn_ref.shape[1]

    def norm_next():
        hb_scr[n % 2] = _modulated_norm(xn_ref[0], modn_ref, 3).astype(BF16)

    def init():
        carry_ref[...] = jnp.zeros_like(carry_ref)

    def process_prev():
        keep = jnp.where((n - 1) % n_s == 0, 0.0, 1.0)
        hb = hb_scr[(n + 1) % 2]
        row = lax.broadcasted_iota(jnp.int32, (ts, 1), 0)
        acc = None
        for c in range(D_FF // FF_CHUNK):
            cols = slice(c * FF_CHUNK, (c + 1) * FF_CHUNK)
            g = _dot(hb, wup_ref[0, :, cols])
            v = _dot(hb, wup_ref[0, :, D_FF + c * FF_CHUNK:D_FF + (c + 1) * FF_CHUNK])
            prev = carry_ref[c] * keep
            g1 = jnp.where(row == 0, prev[7:8, :], pltpu.roll(g, 1, 0))
            g2 = jnp.where(row == 0, prev[6:7, :], jnp.where(row == 1, prev[7:8, :], pltpu.roll(g, 2, 0)))
            carry_ref[c] = g[ts - 8:ts, :]
            conv = (cw_ref[0, 0:1, cols] * g2 + cw_ref[0, 1:2, cols] * g1 + cw_ref[0, 2:3, cols] * g
                    + cb_ref[0, :, cols])
            a = (_silu(conv) * v).astype(BF16)
            part = _dot(a, wd_ref[0, cols, :])
            acc = part if acc is None else acc + part
        out_ref[0] = xp_ref[0] + modp_ref[0, 0, 5:6, :] * acc

    _staggered(n, pl.num_programs(0) - 1, norm_next, process_prev, init)


def _ffn(l, x, mod, wup, cw, cb, wd):
    B, S, D = x.shape
    ts = TS_DENSE
    const = pl.Buffered(1)
    n_s = S // ts
    n_tiles = B * n_s
    nxt = lambda n: jnp.minimum(n, n_tiles - 1)
    prv = lambda n: jnp.maximum(n - 1, 0)
    tile = lambda f: pl.BlockSpec((1, ts, D), lambda n: (f(n) // n_s, f(n) % n_s, 0))
    mods = lambda f: pl.BlockSpec((1, 1, 6, D), lambda n: (l, f(n) // n_s, 0, 0))
    return pl.pallas_call(
        functools.partial(_ffn_kernel, n_s),
        out_shape=jax.ShapeDtypeStruct((B, S, D), F32),
        grid=(n_tiles + 1,),
        in_specs=[tile(nxt), mods(nxt), tile(prv), mods(prv),
                  pl.BlockSpec((1, D, 2 * D_FF), lambda n: (l, 0, 0), pipeline_mode=const),
                  pl.BlockSpec((1, 3, D_FF), lambda n: (l, 0, 0)),
                  pl.BlockSpec((1, 1, D_FF), lambda n: (l, 0, 0)),
                  pl.BlockSpec((1, D_FF, D), lambda n: (l, 0, 0), pipeline_mode=const)],
        out_specs=tile(prv),
        scratch_shapes=[pltpu.VMEM((2, ts, D), BF16), pltpu.VMEM((D_FF // FF_CHUNK, 8, FF_CHUNK), F32)],
        compiler_params=pltpu.CompilerParams(dimension_semantics=("arbitrary",),
                                             vmem_limit_bytes=VMEM_LIMIT),
        name="conv_ffn",
    )(x, mod, x, mod, wup, cw, cb, wd)


def _pad_cols(w, n):
    return jnp.pad(w, [(0, 0)] * (w.ndim - 1) + [(0, n - w.shape[-1])])


def _layout_w_in(w_in):
    o = 0
    seg = {}
    for name, n in (("nq", 256), ("nkc", 64), ("nvc", 64), ("nks", 64), ("nvs", 64), ("nkw", 64), ("nvw", 64),
                    ("ng", 12), ("fq", 256), ("fk", 256), ("fv", 256), ("ff", 4),
                    ("mcq", 256), ("mckv", 128), ("mkr", 32), ("dq", 256), ("dk", 256), ("dv", 256)):
        seg[name] = w_in[..., o:o + n]
        o += n
    parts = [seg["nq"], seg["nks"], seg["nkw"], seg["nkc"], seg["nvc"],
             _pad_cols(seg["ng"], LANES), _pad_cols(seg["ff"], LANES),
             seg["fq"], seg["fk"],
             seg["mcq"], seg["mckv"], _pad_cols(seg["mkr"], LANES),
             seg["dq"], seg["dk"]]
    w1 = jnp.concatenate(parts, axis=-1).astype(BF16)
    wvt = jnp.concatenate([seg["nvs"], seg["nvw"], seg["fv"], seg["dv"]], axis=-1)
    return w1, jnp.swapaxes(wvt, -1, -2).astype(BF16)


def _pack_vecs(nsa_qk_g, fox_qk_g, fox_f_b, mla_cq_g, mla_ckv_g, mla_qk_g, diff_qk_g):
    L = nsa_qk_g.shape[0]
    t4 = lambda g: jnp.tile(g, (1, 256 // g.shape[-1]))
    mla_pad = lambda g: _pad_cols(g, LANES)
    rows = [None] * N_VEC
    rows[V_NQ] = t4(nsa_qk_g[:, 0]) * (LOG2E * D_HEAD ** -0.5)
    rows[V_NK] = jnp.concatenate([nsa_qk_g[:, 2], nsa_qk_g[:, 3]], axis=-1)
    rows[V_FQ] = t4(fox_qk_g[:, 0]) * (LOG2E * D_HEAD ** -0.5)
    rows[V_FK] = t4(fox_qk_g[:, 1])
    rows[V_MQ] = t4(mla_pad(mla_qk_g[:, 0])) * (LOG2E * MLA_QK ** -0.5)
    rows[V_MK] = t4(mla_pad(mla_qk_g[:, 1]))
    rows[V_DQ] = t4(diff_qk_g[:, 0]) * (LOG2E * DIFF_QK ** -0.5)
    rows[V_DK] = t4(diff_qk_g[:, 1])
    rows[V_FB] = fox_f_b
    rows[V_CQG] = mla_cq_g
    rows[V_CKVG] = mla_ckv_g
    rows = [jnp.zeros((L, 256), F32) if r is None else _pad_cols(r.astype(F32), 256) for r in rows]
    return jnp.stack(rows, axis=1)


def _block_diag_ones():
    d = np.arange(256)
    mats = [(d[:, None] // g == d[None, :] // g) for g in (D_HEAD, LANES, DIFF_QK)]
    return jnp.asarray(np.stack(mats), dtype=BF16)


def _overlap_matrix(n_cmp_rows, n_blk):
    c = np.arange(n_cmp_rows)[None, :] * CMP_STRIDE
    b = np.arange(n_blk)[:, None] * SEL_BLOCK
    ov = (c < b + SEL_BLOCK) & (c + CMP_BLOCK > b)
    return jnp.asarray(ov, dtype=BF16)


def kernel(x, c, positions, ada_w, ada_b, w_in, nsa_qk_g, nsa_cmp_pe, nsa_cmp_w1, nsa_cmp_w2, fox_qk_g, fox_f_b,
           mla_cq_g, mla_ckv_g, mla_w_uq, mla_w_ukv, mla_qk_g, diff_qk_g, diff_lambda, diff_out_g, br_w, gate_w,
           gate_b, w_out, ffn_w_up, ffn_conv_w, ffn_conv_b, ffn_w_down):
    B, S, D = x.shape
    L = ada_w.shape[0]
    H = N_HEADS
    n_half = S // CMP_STRIDE

    mod = _modulation(c, ada_w, ada_b).reshape(L, B, 6, D)
    tab = _rope_tables(positions)

    w1, wvt = _layout_w_in(w_in)
    vecs = _pack_vecs(nsa_qk_g, fox_qk_g, fox_f_b, mla_cq_g, mla_ckv_g, mla_qk_g, diff_qk_g)
    bd = _block_diag_ones()
    tri = jnp.asarray(np.tril(np.ones((TS_PROJ, TS_PROJ), np.float32)), dtype=BF16)
    overlap = _overlap_matrix(n_half, S // SEL_BLOCK)
    wuq = _pad_cols(mla_w_uq.reshape(L, MLA_Q_RANK, H, MLA_QK), LANES).reshape(L, MLA_Q_RANK, H * LANES)
    ukv = mla_w_ukv.reshape(L, MLA_KV_RANK, H, MLA_NOPE + D_HEAD)
    wuk = jnp.pad(ukv[..., :MLA_NOPE], [(0, 0), (0, 0), (0, 0), (MLA_ROPE, LANES - MLA_QK)])
    wuk = wuk.reshape(L, MLA_KV_RANK, H * LANES).astype(BF16)
    wuvt = jnp.swapaxes(ukv[..., MLA_NOPE:].reshape(L, MLA_KV_RANK, H * D_HEAD), 1, 2).astype(BF16)
    wuq = wuq.astype(BF16)
    pe = nsa_cmp_pe.reshape(L, 2, CMP_BLOCK * D_HEAD)
    cw1 = nsa_cmp_w1.astype(BF16)
    cw2 = jnp.stack([nsa_cmp_w2[:, 0], jnp.swapaxes(nsa_cmp_w2[:, 1], 1, 2)], axis=1).astype(BF16)
    kcg = _pad_cols(nsa_qk_g[:, 1], LANES).reshape(L, 1, LANES)
    brw = br_w.astype(BF16)
    gw = gate_w.astype(BF16)
    gb = gate_b.reshape(L, 1, 4 * D)
    wo = w_out.astype(BF16)
    wup = ffn_w_up.astype(BF16)
    wd = ffn_w_down.astype(BF16)
    cb = ffn_conv_b.reshape(L, 1, D_FF)
    og = diff_out_g.reshape(L, D_HEAD, 1)

    for l in range(L):
        lam_init = 0.8 - 0.6 * math.exp(-0.3 * l)
        (hbf, nq, nk, nv, nkc, nvc, ng, fq, fk, fv, fkb, mq, mk, mv, dq, dk, dv) = _projection(
            l, x, mod, tab, w1, wvt, vecs, bd, tri, wuq, wuk, wuvt)
        kcmp, vcmp = _compress(l, nkc, nvc, pe, cw1, cw2, kcg)
        o_nsa = _nsa_attention(nq, kcmp, vcmp, nk, nv, ng, overlap)
        o_fox, o_mla = _foxmla_attention(fq, fk, fv, fkb, mq, mk, mv)
        o_diff = _diff_attention(l, lam_init, dq, dk, dv, diff_lambda, og)
        x = _merge(l, x, hbf, o_nsa, o_fox, o_mla, o_diff, mod, brw, gw, gb, wo)
        x = _ffn(l, x, mod, wup, ffn_conv_w, cb, wd)
    return x
```

```python
import functools
import math

import jax
import jax.numpy as jnp
import numpy as np
from jax import lax
from jax.experimental import pallas as pl
from jax.experimental.pallas import tpu as pltpu

F32 = jnp.float32
BF16 = jnp.bfloat16

D_MODEL = 1024
D_HEAD = 64
N_HEADS = 4
MIX_WIDTH = 256
NSA_ROT = 16
CMP_BLOCK = 32
CMP_STRIDE = 16
SEL_BLOCK = 64
SEL_TOPN = 16
WINDOW = 512
MLA_Q_RANK = 256
MLA_KV_RANK = 128
MLA_NOPE = 64
MLA_ROPE = 32
MLA_QK = MLA_ROPE + MLA_NOPE
DIFF_QK = 32
DIFF_ROT = 8
D_FF = 2816
ROPE_THETA = 500000.0
EPS = 1e-6
FORCE_SCORE = 1e4
NEG_BIG = -1e30

LANES = 128
V_ROWS = 80
VMEM_LIMIT = 56 * 1024 * 1024

C_NQ, C_NKSW, C_NKVC, C_NG, C_FF = 0, 256, 384, 512, 640
C_FQ, C_FK = 768, 1024
C_MCQ, C_MCKV, C_MKR = 1280, 1536, 1664
C_DQ, C_DK = 1792, 2048
W_IN_PAD = 2304
R_NV, R_FV, R_DV = 0, 128, 384
W_VT_ROWS = 640

V_NQ, V_NK, V_FQ, V_FK, V_MQ, V_MK, V_DQ, V_DK, V_FB, V_CQG, V_CKVG = range(11)
N_VEC = 16

TS_PROJ = 512
TS_DENSE = 512
TQ_ATT = 512
TQ_NSA = 256
TK_NSA = 256
FF_CHUNK = 2816
LOG2E = math.log2(math.e)


def _dot(a, b):
    return jnp.dot(a, b, preferred_element_type=F32)


def _dot_nt(a, b):
    return lax.dot_general(a, b, (((1,), (1,)), ((), ())), preferred_element_type=F32)


def _split3(x):
    hi = x.astype(BF16)
    r = x - hi.astype(F32)
    mid = r.astype(BF16)
    lo = (r - mid.astype(F32)).astype(BF16)
    return hi, mid, lo


def _rms_rows(x):
    return x * lax.rsqrt(jnp.mean(x * x, axis=-1, keepdims=True) + EPS)


def _modulated_norm(x, mod_ref, r):
    return _rms_rows(x) * (1.0 + mod_ref[0, 0, r + 1:r + 2, :]) + mod_ref[0, 0, r:r + 1, :]


def _group_rms(x, bd, inv_n):
    ss = _dot((x * x).astype(BF16), bd)
    return x * lax.rsqrt(ss * inv_n + EPS)


def _rope(xc, tab_ref, cfg, half):
    c = tab_ref[3 * cfg, 0]
    sa = tab_ref[3 * cfg + 1, 0]
    sb = tab_ref[3 * cfg + 2, 0]
    return xc * c + pltpu.roll(xc, LANES - half, 1) * sa + pltpu.roll(xc, half, 1) * sb


def _log_sigmoid(x):
    return jnp.minimum(x, 0.0) - jnp.log1p(jnp.exp(-jnp.abs(x)))


def _sigmoid(x):
    return 1.0 / (1.0 + jnp.exp(-x))


def _silu(x):
    return x * _sigmoid(x)


def _softmax_step(st, v1t, m, acc):
    m_new = jnp.maximum(m, jnp.max(st, axis=0, keepdims=True))
    p = jnp.exp2(st - m_new).astype(BF16)
    return m_new, jnp.exp2(m - m_new) * acc + _dot(v1t, p)


def _softmax_finish(acc):
    return acc[:D_HEAD] / acc[D_HEAD:D_HEAD + 1]


def _staggered(n, n_tiles, norm_next, process_prev, init):
    @pl.when(n == 0)
    def _():
        init()
        norm_next()

    @pl.when((n > 0) & (n < n_tiles))
    def _():
        norm_next()
        process_prev()

    @pl.when(n == n_tiles)
    def _():
        process_prev()


def _mod_kernel(c_ref, w_ref, b_ref, o_ref):
    c = c_ref[...]
    a = _silu(c).astype(BF16)
    o_ref[0] = _dot(a, w_ref[0].astype(BF16)) + b_ref[0]


def _modulation(c, ada_w, ada_b):
    L, D, N = ada_w.shape
    B = c.shape[0]
    tn = 1536
    return pl.pallas_call(
        _mod_kernel,
        out_shape=jax.ShapeDtypeStruct((L, B, N), F32),
        grid=(L, N // tn),
        in_specs=[pl.BlockSpec((B, D), lambda l, n: (0, 0)),
                  pl.BlockSpec((1, D, tn), lambda l, n: (l, 0, n)),
                  pl.BlockSpec((1, 1, tn), lambda l, n: (l, 0, n))],
        out_specs=pl.BlockSpec((1, B, tn), lambda l, n: (l, 0, n)),
        compiler_params=pltpu.CompilerParams(vmem_limit_bytes=VMEM_LIMIT),
        name="adaln_mod",
    )(c, ada_w, ada_b.reshape(L, 1, N))


_ROPE_CFGS = ((D_HEAD, NSA_ROT, 32), (LANES, MLA_ROPE, 0), (DIFF_QK, DIFF_ROT, 48))


def _rope_tab_kernel(pos_ref, frq_ref, o_ref):
    ang = pos_ref[0].astype(F32) * frq_ref[...]
    cs = jnp.cos(ang)
    sn = jnp.sin(ang)
    lane = lax.broadcasted_iota(jnp.int32, (1, LANES), 1)
    for cfg, (group, n_rot, src) in enumerate(_ROPE_CFGS):
        def spread(x):
            out = None
            for gi in range(LANES // group):
                shift = (gi * group - src) % LANES
                xs = x if shift == 0 else pltpu.roll(x, shift, 1)
                out = xs if out is None else jnp.where(lane >= gi * group, xs, out)
            return out

        d = lane % group
        c = spread(cs)
        s = spread(sn)
        o_ref[3 * cfg, 0] = jnp.where(d < n_rot, c, 1.0)
        o_ref[3 * cfg + 1, 0] = jnp.where(d < n_rot // 2, -s, 0.0)
        o_ref[3 * cfg + 2, 0] = jnp.where((d >= n_rot // 2) & (d < n_rot), s, 0.0)


def _rope_consts():
    row = jnp.zeros((LANES,), F32)
    for group, n_rot, src in _ROPE_CFGS:
        half = n_rot // 2
        inv_freq = ROPE_THETA ** (-jnp.arange(half, dtype=F32) / half)
        row = row.at[src:src + n_rot].set(jnp.tile(inv_freq, 2))
    return row.reshape(1, LANES)


def _rope_tables(positions):
    B, S = positions.shape
    ts = 512
    return pl.pallas_call(
        _rope_tab_kernel,
        out_shape=jax.ShapeDtypeStruct((9, B, S, LANES), F32),
        grid=(B, S // ts),
        in_specs=[pl.BlockSpec((1, ts, 1), lambda b, i: (b, i, 0)),
                  pl.BlockSpec((1, LANES), lambda b, i: (0, 0))],
        out_specs=pl.BlockSpec((9, 1, ts, LANES), lambda b, i: (0, b, i, 0)),
        name="rope_tables",
    )(positions.reshape(B, S, 1), _rope_consts())


def _proj_kernel(n_s, xn_ref, modn_ref, tab_ref, w_ref, wvt_ref, vec_ref, bd_ref, tri_ref, wuq_ref, wuk_ref,
                 wuvt_ref, hbf_ref, nq_ref, nk_ref, nv_ref, nkc_ref, nvc_ref, ng_ref,
                 fq_ref, fk_ref, fv_ref, fkb_ref,
                 mq_ref, mk_ref, mv_ref, dq_ref, dk_ref, dv_ref, hb_scr, carry_ref):
    n = pl.program_id(0)
    ts = xn_ref.shape[1]

    def norm_next():
        hb_scr[n % 2] = _modulated_norm(xn_ref[0], modn_ref, 0).astype(BF16)

    def init():
        carry_ref[...] = jnp.zeros_like(carry_ref)

    def process_prev():
        i = (n - 1) % n_s
        keep = jnp.where(i == 0, 0.0, 1.0)
        hb = hb_scr[(n + 1) % 2]
        hbf_ref[0] = hb

        def proj(c0, width):
            return _dot(hb, w_ref[0, :, c0:c0 + width])

        def vec(r, width=LANES):
            return vec_ref[0, r:r + 1, 0:width]

        bd64, bd128, bd32 = bd_ref[0], bd_ref[1], bd_ref[2]
        row = lax.broadcasted_iota(jnp.int32, (ts, 1), 0)
        lane = lax.broadcasted_iota(jnp.int32, (1, LANES), 1)

        def head_pair(seg, gain_row, bd, inv_n, cfg, half, out_ref):
            yn = _group_rms(seg, bd, inv_n) * vec(gain_row, 256)
            for c in range(2):
                y = yn[:, c * LANES:(c + 1) * LANES]
                if cfg is not None:
                    y = _rope(y, tab_ref, cfg, half)
                yb = y.astype(BF16)
                out_ref[0, 2 * c] = yb[:, :D_HEAD]
                out_ref[0, 2 * c + 1] = yb[:, D_HEAD:]

        ones_rows = jnp.where(lax.broadcasted_iota(jnp.int32, (V_ROWS - D_HEAD, ts), 0) == 0, 1.0, 0.0).astype(BF16)

        def value_store(vt, out_ref):
            for hd in range(vt.shape[0] // D_HEAD):
                out_ref[0, hd] = jnp.concatenate([vt[hd * D_HEAD:(hd + 1) * D_HEAD].astype(BF16), ones_rows], axis=0)

        head_pair(proj(C_NQ, 256), V_NQ, bd64, 1.0 / D_HEAD, 0, NSA_ROT // 2, nq_ref)
        nsa_kv = proj(C_NKSW, 2 * LANES)
        ksw = _group_rms(nsa_kv[:, :LANES], bd64[0:LANES, 0:LANES], 1.0 / D_HEAD) * vec(V_NK)
        ksw = _rope(ksw, tab_ref, 0, NSA_ROT // 2)
        blk_hot = jnp.where(lane - D_HEAD == (i * ts + row) // SEL_BLOCK, 1.0, 0.0)
        nk_ref[0, 0] = jnp.where(lane < D_HEAD, ksw, blk_hot).astype(BF16)
        nk_ref[0, 1] = jnp.where(lane < D_HEAD, pltpu.roll(ksw, D_HEAD, 1), 0.0).astype(BF16)
        kvc = nsa_kv[:, LANES:]
        nkc_ref[0] = kvc[:, :D_HEAD]
        nvc_ref[0] = kvc[:, D_HEAD:]
        gates = proj(C_NG, 2 * LANES)
        ng_ref[0] = _sigmoid(gates[:, :LANES])

        head_pair(proj(C_FQ, 256), V_FQ, bd64, 1.0 / D_HEAD, None, 0, fq_ref)
        head_pair(proj(C_FK, 256), V_FK, bd64, 1.0 / D_HEAD, None, 0, fk_ref)

        lf = _log_sigmoid(gates[:, LANES:] + vec(V_FB))
        tri = tri_ref[...]
        p0, p1, p2 = _split3(lf)
        c01 = _dot(tri, jnp.concatenate([p0, p1], axis=1))
        cum = c01[:, :LANES] + c01[:, LANES:] + _dot(tri, p2) + carry_ref[0:1, :] * keep
        carry_ref[0:1, :] = cum[ts - 1:ts, :]
        fkb_ref[0] = cum * (-LOG2E)

        cq = proj(C_MCQ, MLA_Q_RANK)
        cqn = (_rms_rows(cq) * vec(V_CQG, MLA_Q_RANK)).astype(BF16)
        qm = _dot(cqn, wuq_ref[0])
        mla_kv = proj(C_MCKV, 2 * LANES)
        ckvn = (_rms_rows(mla_kv[:, :MLA_KV_RANK]) * vec(V_CKVG)).astype(BF16)
        kk = _dot(ckvn, wuk_ref[0])
        vvt = _dot_nt(wuvt_ref[0], ckvn)
        kr = mla_kv[:, MLA_KV_RANK:]
        kr2 = jnp.concatenate([kr, kr], axis=1)
        for pair in range(N_HEADS // 2):
            sl = slice(pair * 256, (pair + 1) * 256)
            yq = _group_rms(qm[:, sl], bd128, 1.0 / MLA_QK) * vec(V_MQ, 256)
            yk = _group_rms(kk[:, sl] + kr2, bd128, 1.0 / MLA_QK) * vec(V_MK, 256)
            for c in range(2):
                cs = slice(c * LANES, (c + 1) * LANES)
                mq_ref[0, 2 * pair + c] = _rope(yq[:, cs], tab_ref, 1, MLA_ROPE // 2).astype(BF16)
                mk_ref[0, 2 * pair + c] = _rope(yk[:, cs], tab_ref, 1, MLA_ROPE // 2).astype(BF16)

        head_pair(proj(C_DQ, 256), V_DQ, bd32, 1.0 / DIFF_QK, 2, DIFF_ROT // 2, dq_ref)
        head_pair(proj(C_DK, 256), V_DK, bd32, 1.0 / DIFF_QK, 2, DIFF_ROT // 2, dk_ref)

        value_store(vvt, mv_ref)
        vt_all = _dot_nt(wvt_ref[0], hb)
        value_store(vt_all[R_NV:R_FV], nv_ref)
        value_store(vt_all[R_FV:R_DV], fv_ref)
        value_store(vt_all[R_DV:W_VT_ROWS], dv_ref)

    _staggered(n, pl.num_programs(0) - 1, norm_next, process_prev, init)


def _projection(l, x, mod, tab, w1, wvt, vecs, bd, tri, wuq, wuk, wuvt):
    B, S, D = x.shape
    ts = TS_PROJ
    H = N_HEADS
    n_s = S // ts
    n_tiles = B * n_s
    nxt = lambda n: jnp.minimum(n, n_tiles - 1)
    prv = lambda n: jnp.maximum(n - 1, 0)
    hm = lambda d, dt: jax.ShapeDtypeStruct((B, H, S, d), dt)
    hm_spec = lambda nh, d: pl.BlockSpec((1, nh, ts, d), lambda n: (prv(n) // n_s, 0, prv(n) % n_s, 0))
    row_spec = lambda d: pl.BlockSpec((1, ts, d), lambda n: (prv(n) // n_s, prv(n) % n_s, 0))
    vt = lambda nh: jax.ShapeDtypeStruct((B, nh, V_ROWS, S), BF16)
    vt_spec = lambda nh: pl.BlockSpec((1, nh, V_ROWS, ts), lambda n: (prv(n) // n_s, 0, 0, prv(n) % n_s))
    out_shape = [
        jax.ShapeDtypeStruct((B, S, D), BF16),
        hm(D_HEAD, BF16),
        jax.ShapeDtypeStruct((B, 2, S, LANES), BF16),
        vt(2),
        jax.ShapeDtypeStruct((B, S, D_HEAD), F32),
        jax.ShapeDtypeStruct((B, S, D_HEAD), F32),
        jax.ShapeDtypeStruct((B, S, LANES), F32),
        hm(D_HEAD, BF16), hm(D_HEAD, BF16), vt(H),
        jax.ShapeDtypeStruct((B, S, LANES), F32),
        hm(LANES, BF16), hm(LANES, BF16), vt(H),
        hm(D_HEAD, BF16), hm(D_HEAD, BF16), vt(H),
    ]
    out_specs = [
        row_spec(D), hm_spec(H, D_HEAD), hm_spec(2, LANES), vt_spec(2),
        row_spec(D_HEAD), row_spec(D_HEAD), row_spec(LANES),
        hm_spec(H, D_HEAD), hm_spec(H, D_HEAD), vt_spec(H),
        row_spec(LANES),
        hm_spec(H, LANES), hm_spec(H, LANES), vt_spec(H),
        hm_spec(H, D_HEAD), hm_spec(H, D_HEAD), vt_spec(H),
    ]
    in_specs = [
        pl.BlockSpec((1, ts, D), lambda n: (nxt(n) // n_s, nxt(n) % n_s, 0)),
        pl.BlockSpec((1, 1, 6, D), lambda n: (l, nxt(n) // n_s, 0, 0)),
        pl.BlockSpec((9, 1, ts, LANES), lambda n: (0, prv(n) // n_s, prv(n) % n_s, 0)),
        pl.BlockSpec((1, D, W_IN_PAD), lambda n: (l, 0, 0)),
        pl.BlockSpec((1, W_VT_ROWS, D), lambda n: (l, 0, 0)),
        pl.BlockSpec((1, N_VEC, 256), lambda n: (l, 0, 0)),
        pl.BlockSpec((3, 256, 256), lambda n: (0, 0, 0)),
        pl.BlockSpec((ts, ts), lambda n: (0, 0)),
        pl.BlockSpec((1, MLA_Q_RANK, 512), lambda n: (l, 0, 0)),
        pl.BlockSpec((1, MLA_KV_RANK, 512), lambda n: (l, 0, 0)),
        pl.BlockSpec((1, 256, MLA_KV_RANK), lambda n: (l, 0, 0)),
    ]
    return pl.pallas_call(
        functools.partial(_proj_kernel, n_s),
        out_shape=out_shape,
        grid=(n_tiles + 1,),
        in_specs=in_specs,
        out_specs=out_specs,
        scratch_shapes=[pltpu.VMEM((2, ts, D), BF16), pltpu.VMEM((8, LANES), F32)],
        compiler_params=pltpu.CompilerParams(dimension_semantics=("arbitrary",),
                                             vmem_limit_bytes=VMEM_LIMIT),
        name="in_proj",
    )(x, mod, tab, w1, wvt, vecs, bd, tri, wuq, wuk, wuvt)


def _cmp_kernel(kc_ref, vc_ref, pe_ref, w1_ref, w2_ref, g_ref, ko_ref, vo_ref, x_scr):
    half = CMP_STRIDE * D_HEAD
    n = kc_ref.shape[1] // CMP_STRIDE

    def hidden(src_ref, j):
        for p in range(CMP_STRIDE):
            x_scr[:, p * D_HEAD:(p + 1) * D_HEAD] = src_ref[0, pl.ds(p, n, stride=CMP_STRIDE), :]
        x2 = x_scr[...]
        xa = _dot((x2 + pe_ref[0, j:j + 1, 0:half]).astype(BF16), w1_ref[0, j, 0:half, :])
        xb = _dot((x2 + pe_ref[0, j:j + 1, half:2 * half]).astype(BF16), w1_ref[0, j, half:2 * half, :])
        return _silu(xa + pltpu.roll(xb, n - 1, 0)).astype(BF16)

    kcmp = _rms_rows(_dot(hidden(kc_ref, 0), w2_ref[0, 0])) * g_ref[0, 0:1, 0:D_HEAD]
    ko_ref[0] = kcmp.astype(BF16)
    vo_ref[0] = _dot_nt(w2_ref[0, 1], hidden(vc_ref, 1)).astype(BF16)


def _compress(l, kc, vc, pe, w1, w2, g):
    B, S, _ = kc.shape
    n = S // CMP_STRIDE
    wide = CMP_STRIDE * D_HEAD
    return pl.pallas_call(
        _cmp_kernel,
        out_shape=[jax.ShapeDtypeStruct((B, n, D_HEAD), BF16), jax.ShapeDtypeStruct((B, D_HEAD, n), BF16)],
        grid=(B,),
        in_specs=[pl.BlockSpec((1, S, D_HEAD), lambda b: (b, 0, 0)),
                  pl.BlockSpec((1, S, D_HEAD), lambda b: (b, 0, 0)),
                  pl.BlockSpec((1, 2, 2 * wide), lambda b: (l, 0, 0)),
                  pl.BlockSpec((1, 2, 2 * wide, D_HEAD), lambda b: (l, 0, 0, 0)),
                  pl.BlockSpec((1, 2, D_HEAD, D_HEAD), lambda b: (l, 0, 0, 0)),
                  pl.BlockSpec((1, 1, LANES), lambda b: (l, 0, 0))],
        out_specs=[pl.BlockSpec((1, n, D_HEAD), lambda b: (b, 0, 0)),
                   pl.BlockSpec((1, D_HEAD, n), lambda b: (b, 0, 0))],
        scratch_shapes=[pltpu.VMEM((n, wide), F32)],
        name="nsa_compress",
    )(kc, vc, pe, w1, w2, g)


def _nsa_kernel(q_ref, kc_ref, vc_ref, k_ref, v_ref, g_ref, ov_ref, o_ref):
    i = pl.program_id(1)
    tq, tk = TQ_NSA, TK_NSA
    H = N_HEADS
    rows = H * tq
    n_blk = ov_ref.shape[0]
    q = q_ref[0].reshape(rows, D_HEAD)
    t4 = i * tq + lax.broadcasted_iota(jnp.int32, (1, rows), 1) % tq

    nc = kc_ref.shape[1]
    sc = _dot_nt(kc_ref[0], q)
    cend = lax.broadcasted_iota(jnp.int32, (nc, 1), 0) * CMP_STRIDE + (CMP_BLOCK - 1)
    sc = jnp.where(cend <= t4, sc, NEG_BIG)
    e = jnp.exp2(sc - jnp.max(sc, axis=0, keepdims=True))
    p = e / jnp.sum(e, axis=0, keepdims=True)
    p = jnp.where(t4 >= CMP_BLOCK - 1, p, 0.0)
    o_cmp = _dot(vc_ref[0], p.astype(BF16))
    psum = p[:, 0:tq] + p[:, tq:2 * tq] + p[:, 2 * tq:3 * tq] + p[:, 3 * tq:4 * tq]
    p0, p1, p2 = _split3(psum)
    ov = ov_ref[...]
    imp = _dot(ov, p0) + _dot(ov, p1) + _dot(ov, p2)

    blk = lax.broadcasted_iota(jnp.int32, (n_blk, 1), 0)
    tl = i * tq + lax.broadcasted_iota(jnp.int32, (1, tq), 1)
    cur = tl // SEL_BLOCK
    forced = (blk == 0) | (blk == cur) | (blk == cur - 1)
    score = jnp.where(blk * SEL_BLOCK > tl, -1.0, jnp.where(forced, FORCE_SCORE, imp))
    rank = jnp.zeros((n_blk, tq), F32)
    for r in range(n_blk):
        other = score[r:r + 1, :]
        ahead = (other > score) | ((other == score) & (blk > r))
        rank = rank + jnp.where(ahead, 1.0, 0.0)
    drop_t = jnp.where(rank < float(min(SEL_TOPN, n_blk)), 0.0, NEG_BIG)
    drop = jnp.concatenate([drop_t, jnp.zeros((LANES - n_blk, tq), F32)], axis=0).T.astype(BF16)
    qx = jnp.concatenate([q, jnp.concatenate([drop[:, 0:D_HEAD]] * H, axis=0)], axis=1)

    def slc_scores(j):
        return _dot_nt(k_ref[0, 0, pl.ds(j * tk, tk), :], qx)

    init = (jnp.full((1, rows), NEG_BIG, F32), jnp.zeros((V_ROWS, rows), F32))

    def far_step(j, carry):
        return _softmax_step(slc_scores(j), v_ref[0, 0, :, pl.ds(j * tk, tk)], *carry)

    n_near = WINDOW // tk
    kl = lax.broadcasted_iota(jnp.int32, (tk, 1), 0)
    rl = lax.broadcasted_iota(jnp.int32, (1, rows), 1) % tq

    def near_step(rel, carry):
        j = i - rel
        s_s = slc_scores(j)
        s_w = _dot_nt(k_ref[0, 1, pl.ds(j * tk, tk), :], qx)
        if rel == 0:
            s_s = jnp.where(kl <= rl, s_s, NEG_BIG)
            s_w = jnp.where(kl <= rl, s_w, NEG_BIG)
        elif rel == n_near:
            s_w = jnp.where(kl > rl, s_w, NEG_BIG)
        c_s = _softmax_step(s_s, v_ref[0, 0, :, pl.ds(j * tk, tk)], *carry[0])
        c_w = _softmax_step(s_w, v_ref[0, 1, :, pl.ds(j * tk, tk)], *carry[1])
        return c_s, c_w

    lo = jnp.maximum(i - n_near, 0)
    carry = (lax.fori_loop(0, lo, far_step, init), init)
    for rel in range(n_near, 0, -1):
        carry = lax.cond(i >= rel, functools.partial(near_step, rel), lambda c: c, carry)
    (_, a_s), (_, a_w) = near_step(0, carry)
    o_slc = _softmax_finish(a_s)
    o_win = _softmax_finish(a_w)

    gt = g_ref[0].T
    outs = []
    for hd in range(H):
        r = slice(hd * tq, (hd + 1) * tq)
        outs.append(gt[hd:hd + 1] * o_cmp[:, r] + gt[H + hd:H + hd + 1] * o_slc[:, r]
                    + gt[2 * H + hd:2 * H + hd + 1] * o_win[:, r])
    o_ref[0] = jnp.concatenate(outs, axis=0).T.astype(BF16)


def _nsa_attention(q, kcmp, vcmp, ksw, vsw, gates, overlap):
    B, H, S, _ = q.shape
    tq = TQ_NSA
    nc = kcmp.shape[1]
    return pl.pallas_call(
        _nsa_kernel,
        out_shape=jax.ShapeDtypeStruct((B, S, MIX_WIDTH), BF16),
        grid=(B, S // tq),
        in_specs=[pl.BlockSpec((1, H, tq, D_HEAD), lambda b, i: (b, 0, i, 0)),
                  pl.BlockSpec((1, nc, D_HEAD), lambda b, i: (b, 0, 0)),
                  pl.BlockSpec((1, D_HEAD, nc), lambda b, i: (b, 0, 0)),
                  pl.BlockSpec((1, 2, S, LANES), lambda b, i: (b, 0, 0, 0)),
                  pl.BlockSpec((1, 2, V_ROWS, S), lambda b, i: (b, 0, 0, 0)),
                  pl.BlockSpec((1, tq, LANES), lambda b, i: (b, i, 0)),
                  pl.BlockSpec(overlap.shape, lambda b, i: (0, 0))],
        out_specs=pl.BlockSpec((1, tq, MIX_WIDTH), lambda b, i: (b, i, 0)),
        compiler_params=pltpu.CompilerParams(vmem_limit_bytes=VMEM_LIMIT),
        name="nsa_attention",
    )(q, kcmp, vcmp, ksw, vsw, gates, overlap)


def _dense_kernel(lam_init, fq_ref, fk_ref, fv_ref, kb_ref, mq_ref, mk_ref, mv_ref, dq_ref, dk_ref, dv_ref,
                  lam_ref, og_ref, of_ref, om_ref, od_ref):
    i = pl.program_id(1)
    t = TQ_ATT
    H = N_HEADS
    lane = lax.broadcasted_iota(jnp.int32, (1, D_HEAD), 1)
    chains, qs = [], []
    for hd in range(H):
        chains.append((fk_ref, fv_ref, hd, True))
        qs.append(fq_ref[0, hd])
    for hd in range(H):
        chains.append((mk_ref, mv_ref, hd, False))
        qs.append(mq_ref[0, hd])
    for hd in range(H):
        q = dq_ref[0, hd]
        zero = jnp.zeros_like(q)
        chains.append((dk_ref, dv_ref, hd, False))
        qs.append(jnp.concatenate([jnp.where(lane < DIFF_QK, q, zero), jnp.where(lane >= DIFF_QK, q, zero)], axis=0))

    def step(j, carry, diagonal=False):
        out = []
        kk = lax.broadcasted_iota(jnp.int32, (t, 1), 0)
        for c, (k_ref, v_ref, hd, biased) in enumerate(chains):
            s = _dot_nt(k_ref[0, hd, pl.ds(j * t, t), :], qs[c])
            if biased:
                s = s + kb_ref[0, pl.ds(j * t, t), :][:, hd:hd + 1]
            if diagonal:
                rr = lax.broadcasted_iota(jnp.int32, (1, s.shape[1]), 1) % t
                s = jnp.where(kk <= rr, s, NEG_BIG)
            out.append(_softmax_step(s, v_ref[0, hd, :, pl.ds(j * t, t)], *carry[c]))
        return tuple(out)

    init = tuple((jnp.full((1, q.shape[0]), NEG_BIG, F32), jnp.zeros((V_ROWS, q.shape[0]), F32)) for q in qs)
    carry = lax.fori_loop(0, i, step, init)
    carry = step(i, carry, diagonal=True)
    outs = [_softmax_finish(acc) for _, acc in carry]
    of_ref[0] = jnp.concatenate(outs[:H], axis=0).T.astype(BF16)
    om_ref[0] = jnp.concatenate(outs[H:2 * H], axis=0).T.astype(BF16)
    lv = lam_ref[0]
    lam = (jnp.exp(jnp.sum(lv[0:1] * lv[1:2], axis=-1, keepdims=True))
           - jnp.exp(jnp.sum(lv[2:3] * lv[3:4], axis=-1, keepdims=True)) + lam_init)
    douts = []
    for o in outs[2 * H:]:
        o = o[:, 0:t] - lam * o[:, t:2 * t]
        douts.append(o * lax.rsqrt(jnp.mean(o * o, axis=0, keepdims=True) + EPS) * og_ref[0] * (1.0 - lam_init))
    od_ref[0] = jnp.concatenate(douts, axis=0).T.astype(BF16)


def _dense_attention(l, lam_init, fq, fk, fv, fkb, mq, mk, mv, dq, dk, dv, lam_vec, out_g):
    B, H, S, _ = fq.shape
    t = TQ_ATT
    qspec = lambda d: pl.BlockSpec((1, H, t, d), lambda b, i: (b, 0, i, 0))
    kspec = lambda d: pl.BlockSpec((1, H, S, d), lambda b, i: (b, 0, 0, 0))
    vspec = pl.BlockSpec((1, H, V_ROWS, S), lambda b, i: (b, 0, 0, 0))
    ospec = pl.BlockSpec((1, t, MIX_WIDTH), lambda b, i: (b, i, 0))
    return pl.pallas_call(
        functools.partial(_dense_kernel, lam_init),
        out_shape=[jax.ShapeDtypeStruct((B, S, MIX_WIDTH), BF16)] * 3,
        grid=(B, S // t),
        in_specs=[qspec(D_HEAD), kspec(D_HEAD), vspec, pl.BlockSpec((1, S, LANES), lambda b, i: (b, 0, 0)),
                  qspec(LANES), kspec(LANES), vspec,
                  qspec(D_HEAD), kspec(D_HEAD), vspec,
                  pl.BlockSpec((1, 4, DIFF_QK), lambda b, i: (l, 0, 0)),
                  pl.BlockSpec((1, D_HEAD, 1), lambda b, i: (l, 0, 0))],
        out_specs=[ospec, ospec, ospec],
        compiler_params=pltpu.CompilerParams(vmem_limit_bytes=VMEM_LIMIT),
        name="dense_attention",
    )(fq, fk, fv, fkb, mq, mk, mv, dq, dk, dv, lam_vec, out_g)


def _merge_kernel(x_ref, h_ref, o0_ref, o1_ref, o2_ref, o3_ref, mod_ref, brw_ref, gw_ref, gb_ref, wo_ref,
                  out_ref):
    hb = h_ref[0]
    merged = None
    for m, o_ref in enumerate((o0_ref, o1_ref, o2_ref, o3_ref)):
        y = _dot(o_ref[0], brw_ref[0, m])
        cols = slice(m * D_MODEL, (m + 1) * D_MODEL)
        gate = _sigmoid(_dot(hb, gw_ref[0, :, cols]) + gb_ref[0, :, cols])
        merged = gate * y if merged is None else merged + gate * y
    out = _dot(merged.astype(BF16), wo_ref[0])
    out_ref[0] = x_ref[0] + mod_ref[0, 0, 2:3, :] * out


def _merge(l, x, hbf, o_nsa, o_fox, o_mla, o_diff, mod, brw, gw, gb, wo):
    B, S, D = x.shape
    ts = TS_DENSE
    row = lambda d: pl.BlockSpec((1, ts, d), lambda b, i: (b, i, 0))
    return pl.pallas_call(
        _merge_kernel,
        out_shape=jax.ShapeDtypeStruct((B, S, D), F32),
        grid=(B, S // ts),
        in_specs=[row(D), row(D), row(MIX_WIDTH), row(MIX_WIDTH), row(MIX_WIDTH), row(MIX_WIDTH),
                  pl.BlockSpec((1, 1, 6, D), lambda b, i: (l, b, 0, 0)),
                  pl.BlockSpec((1, 4, MIX_WIDTH, D), lambda b, i: (l, 0, 0, 0)),
                  pl.BlockSpec((1, D, 4 * D), lambda b, i: (l, 0, 0)),
                  pl.BlockSpec((1, 1, 4 * D), lambda b, i: (l, 0, 0)),
                  pl.BlockSpec((1, D, D), lambda b, i: (l, 0, 0))],
        out_specs=row(D),
        compiler_params=pltpu.CompilerParams(vmem_limit_bytes=VMEM_LIMIT),
        name="merge_out",
    )(x, hbf, o_nsa, o_fox, o_mla, o_diff, mod, brw, gw, gb, wo)


def _ffn_kernel(n_s, xn_ref, modn_ref, xp_ref, modp_ref, wup_ref, cw_ref, cb_ref, wd_ref, out_ref,
                hb_scr, carry_ref):
    n = pl.program_id(0)
    ts = xn_ref.shape[1]

    def norm_next():
        hb_scr[n % 2] = _modulated_norm(xn_ref[0], modn_ref, 3).astype(BF16)

    def init():
        carry_ref[...] = jnp.zeros_like(carry_ref)

    def process_prev():
        keep = jnp.where((n - 1) % n_s == 0, 0.0, 1.0)
        hb = hb_scr[(n + 1) % 2]
        row = lax.broadcasted_iota(jnp.int32, (ts, 1), 0)
        acc = None
        for c in range(D_FF // FF_CHUNK):
            cols = slice(c * FF_CHUNK, (c + 1) * FF_CHUNK)
            g = _dot(hb, wup_ref[0, :, cols])
            v = _dot(hb, wup_ref[0, :, D_FF + c * FF_CHUNK:D_FF + (c + 1) * FF_CHUNK])
            prev = carry_ref[c] * keep
            g1 = jnp.where(row == 0, prev[7:8, :], pltpu.roll(g, 1, 0))
            g2 = jnp.where(row == 0, prev[6:7, :], jnp.where(row == 1, prev[7:8, :], pltpu.roll(g, 2, 0)))
            carry_ref[c] = g[ts - 8:ts, :]
            conv = (cw_ref[0, 0:1, cols] * g2 + cw_ref[0, 1:2, cols] * g1 + cw_ref[0, 2:3, cols] * g
                    + cb_ref[0, :, cols])
            a = (_silu(conv) * v).astype(BF16)
            part = _dot(a, wd_ref[0, cols, :])
            acc = part if acc is None else acc + part
        out_ref[0] = xp_ref[0] + modp_ref[0, 0, 5:6, :] * acc

    _staggered(n, pl.num_programs(0) - 1, norm_next, process_prev, init)


def _ffn(l, x, mod, wup, cw, cb, wd):
    B, S, D = x.shape
    ts = TS_DENSE
    const = pl.Buffered(1)
    n_s = S // ts
    n_tiles = B * n_s
    nxt = lambda n: jnp.minimum(n, n_tiles - 1)
    prv = lambda n: jnp.maximum(n - 1, 0)
    tile = lambda f: pl.BlockSpec((1, ts, D), lambda n: (f(n) // n_s, f(n) % n_s, 0))
    mods = lambda f: pl.BlockSpec((1, 1, 6, D), lambda n: (l, f(n) // n_s, 0, 0))
    return pl.pallas_call(
        functools.partial(_ffn_kernel, n_s),
        out_shape=jax.ShapeDtypeStruct((B, S, D), F32),
        grid=(n_tiles + 1,),
        in_specs=[tile(nxt), mods(nxt), tile(prv), mods(prv),
                  pl.BlockSpec((1, D, 2 * D_FF), lambda n: (l, 0, 0), pipeline_mode=const),
                  pl.BlockSpec((1, 3, D_FF), lambda n: (l, 0, 0)),
                  pl.BlockSpec((1, 1, D_FF), lambda n: (l, 0, 0)),
                  pl.BlockSpec((1, D_FF, D), lambda n: (l, 0, 0), pipeline_mode=const)],
        out_specs=tile(prv),
        scratch_shapes=[pltpu.VMEM((2, ts, D), BF16), pltpu.VMEM((D_FF // FF_CHUNK, 8, FF_CHUNK), F32)],
        compiler_params=pltpu.CompilerParams(dimension_semantics=("arbitrary",),
                                             vmem_limit_bytes=VMEM_LIMIT),
        name="conv_ffn",
    )(x, mod, x, mod, wup, cw, cb, wd)


def _pad_cols(w, n):
    return jnp.pad(w, [(0, 0)] * (w.ndim - 1) + [(0, n - w.shape[-1])])


def _layout_w_in(w_in):
    o = 0
    seg = {}
    for name, n in (("nq", 256), ("nkc", 64), ("nvc", 64), ("nks", 64), ("nvs", 64), ("nkw", 64), ("nvw", 64),
                    ("ng", 12), ("fq", 256), ("fk", 256), ("fv", 256), ("ff", 4),
                    ("mcq", 256), ("mckv", 128), ("mkr", 32), ("dq", 256), ("dk", 256), ("dv", 256)):
        seg[name] = w_in[..., o:o + n]
        o += n
    parts = [seg["nq"], seg["nks"], seg["nkw"], seg["nkc"], seg["nvc"],
             _pad_cols(seg["ng"], LANES), _pad_cols(seg["ff"], LANES),
             seg["fq"], seg["fk"],
             seg["mcq"], seg["mckv"], _pad_cols(seg["mkr"], LANES),
             seg["dq"], seg["dk"]]
    w1 = jnp.concatenate(parts, axis=-1).astype(BF16)
    wvt = jnp.concatenate([seg["nvs"], seg["nvw"], seg["fv"], seg["dv"]], axis=-1)
    return w1, jnp.swapaxes(wvt, -1, -2).astype(BF16)


def _pack_vecs(nsa_qk_g, fox_qk_g, fox_f_b, mla_cq_g, mla_ckv_g, mla_qk_g, diff_qk_g):
    L = nsa_qk_g.shape[0]
    t4 = lambda g: jnp.tile(g, (1, 256 // g.shape[-1]))
    mla_pad = lambda g: _pad_cols(g, LANES)
    rows = [None] * N_VEC
    rows[V_NQ] = t4(nsa_qk_g[:, 0]) * (LOG2E * D_HEAD ** -0.5)
    rows[V_NK] = jnp.concatenate([nsa_qk_g[:, 2], nsa_qk_g[:, 3]], axis=-1)
    rows[V_FQ] = t4(fox_qk_g[:, 0]) * (LOG2E * D_HEAD ** -0.5)
    rows[V_FK] = t4(fox_qk_g[:, 1])
    rows[V_MQ] = t4(mla_pad(mla_qk_g[:, 0])) * (LOG2E * MLA_QK ** -0.5)
    rows[V_MK] = t4(mla_pad(mla_qk_g[:, 1]))
    rows[V_DQ] = t4(diff_qk_g[:, 0]) * (LOG2E * DIFF_QK ** -0.5)
    rows[V_DK] = t4(diff_qk_g[:, 1])
    rows[V_FB] = fox_f_b
    rows[V_CQG] = mla_cq_g
    rows[V_CKVG] = mla_ckv_g
    rows = [jnp.zeros((L, 256), F32) if r is None else _pad_cols(r.astype(F32), 256) for r in rows]
    return jnp.stack(rows, axis=1)


def _block_diag_ones():
    d = np.arange(256)
    mats = [(d[:, None] // g == d[None, :] // g) for g in (D_HEAD, LANES, DIFF_QK)]
    return jnp.asarray(np.stack(mats), dtype=BF16)


def _overlap_matrix(n_cmp_rows, n_blk):
    c = np.arange(n_cmp_rows)[None, :] * CMP_STRIDE
    b = np.arange(n_blk)[:, None] * SEL_BLOCK
    ov = (c < b + SEL_BLOCK) & (c + CMP_BLOCK > b)
    return jnp.asarray(ov, dtype=BF16)


def kernel(x, c, positions, ada_w, ada_b, w_in, nsa_qk_g, nsa_cmp_pe, nsa_cmp_w1, nsa_cmp_w2, fox_qk_g, fox_f_b,
           mla_cq_g, mla_ckv_g, mla_w_uq, mla_w_ukv, mla_qk_g, diff_qk_g, diff_lambda, diff_out_g, br_w, gate_w,
           gate_b, w_out, ffn_w_up, ffn_conv_w, ffn_conv_b, ffn_w_down):
    B, S, D = x.shape
    L = ada_w.shape[0]
    H = N_HEADS
    n_half = S // CMP_STRIDE

    mod = _modulation(c, ada_w, ada_b).reshape(L, B, 6, D)
    tab = _rope_tables(positions)

    w1, wvt = _layout_w_in(w_in)
    vecs = _pack_vecs(nsa_qk_g, fox_qk_g, fox_f_b, mla_cq_g, mla_ckv_g, mla_qk_g, diff_qk_g)
    bd = _block_diag_ones()
    tri = jnp.asarray(np.tril(np.ones((TS_PROJ, TS_PROJ), np.float32)), dtype=BF16)
    overlap = _overlap_matrix(n_half, S // SEL_BLOCK)
    wuq = _pad_cols(mla_w_uq.reshape(L, MLA_Q_RANK, H, MLA_QK), LANES).reshape(L, MLA_Q_RANK, H * LANES)
    ukv = mla_w_ukv.reshape(L, MLA_KV_RANK, H, MLA_NOPE + D_HEAD)
    wuk = jnp.pad(ukv[..., :MLA_NOPE], [(0, 0), (0, 0), (0, 0), (MLA_ROPE, LANES - MLA_QK)])
    wuk = wuk.reshape(L, MLA_KV_RANK, H * LANES).astype(BF16)
    wuvt = jnp.swapaxes(ukv[..., MLA_NOPE:].reshape(L, MLA_KV_RANK, H * D_HEAD), 1, 2).astype(BF16)
    wuq = wuq.astype(BF16)
    pe = nsa_cmp_pe.reshape(L, 2, CMP_BLOCK * D_HEAD)
    cw1 = nsa_cmp_w1.astype(BF16)
    cw2 = jnp.stack([nsa_cmp_w2[:, 0], jnp.swapaxes(nsa_cmp_w2[:, 1], 1, 2)], axis=1).astype(BF16)
    kcg = _pad_cols(nsa_qk_g[:, 1], LANES).reshape(L, 1, LANES)
    brw = br_w.astype(BF16)
    gw = gate_w.astype(BF16)
    gb = gate_b.reshape(L, 1, 4 * D)
    wo = w_out.astype(BF16)
    wup = ffn_w_up.astype(BF16)
    wd = ffn_w_down.astype(BF16)
    cb = ffn_conv_b.reshape(L, 1, D_FF)
    og = diff_out_g.reshape(L, D_HEAD, 1)

    for l in range(L):
        lam_init = 0.8 - 0.6 * math.exp(-0.3 * l)
        (hbf, nq, nk, nv, nkc, nvc, ng, fq, fk, fv, fkb, mq, mk, mv, dq, dk, dv) = _projection(
            l, x, mod, tab, w1, wvt, vecs, bd, tri, wuq, wuk, wuvt)
        kcmp, vcmp = _compress(l, nkc, nvc, pe, cw1, cw2, kcg)
        o_nsa = _nsa_attention(nq, kcmp, vcmp, nk, nv, ng, overlap)
        o_fox, o_mla, o_diff = _dense_attention(l, lam_init, fq, fk, fv, fkb, mq, mk, mv, dq, dk, dv, diff_lambda, og)
        x = _merge(l, x, hbf, o_nsa, o_fox, o_mla, o_diff, mod, brw, gw, gb, wo)
        x = _ffn(l, x, mod, wup, ffn_conv_w, cb, wd)
    return x
```
